```python
import math
import jax, jax.numpy as jnp
from jax import lax
import numpy as np

D_MODEL = 2048
BATCH = 4
SEQ = 2048
DEPTH = 4
DEC_BATCH = 128
DEC_SEQ = 1
PAST_LEN = 16384
PAGE_SIZE = 128

N_MIXERS = 4
NORM_EPS = 1e-6
S5_GROUP = 16
S5_GROUPS = D_MODEL // S5_GROUP
S5_STATE = 64
RW_HEAD = 64
RW_HEADS = D_MODEL // RW_HEAD
RW_DECAY_LORA = 96
RW_A_LORA = 96
RW_GATE_LORA = 256
RW_LN_EPS = 64e-5
GLA_HEADS = 4
GLA_KEY = D_MODEL // 2
GLA_VAL = D_MODEL
GLA_DK = GLA_KEY // GLA_HEADS
GLA_DV = GLA_VAL // GLA_HEADS
GLA_GATE_RANK = 16
GLA_GATE_TEMP = 16.0
HG_EXPAND = 128
HG_HEADS = D_MODEL // HG_EXPAND
HG_DK = HG_EXPAND
HG_DV = D_MODEL // HG_HEADS
CHUNK = 64
D_FF = 5632
CONV_W = 3

kernel_name = 'hybrid_s5_rwkv7_gla_hgrn2_convffn_step'


def _rmsnorm(x, g):
    xf = x.astype(jnp.float32)
    y = xf * lax.rsqrt(jnp.mean(xf * xf, axis=-1, keepdims=True) + NORM_EPS)
    return (y * g.astype(jnp.float32)).astype(x.dtype)


def _headnorm(o, g):
    o = o * lax.rsqrt(jnp.mean(o * o, axis=-1, keepdims=True) + NORM_EPS)
    return o * g.astype(jnp.float32)


def _chunked_gated_la(q, k, v, log_g, s0):
    b, t, h, dk = q.shape
    dv = v.shape[-1]
    c = min(CHUNK, t)
    pad = (-t) % c
    if pad:
        widths = ((0, 0), (0, pad), (0, 0), (0, 0))
        q, k, v, log_g = [jnp.pad(a, widths) for a in (q, k, v, log_g)]
    n = (t + pad) // c

    def blocks(a):
        return a.reshape(b, n, c, h, a.shape[-1]).transpose(1, 0, 3, 2, 4)

    causal = jnp.tril(jnp.ones((c, c), dtype=bool))[:, :, None]

    def step(s, blk):
        qb, kb, vb, gb = blk
        cum = jnp.cumsum(gb, axis=2)
        o_inter = jnp.einsum('bhtd,bhdv->bhtv', qb * jnp.exp(cum), s)
        diff = cum[:, :, :, None, :] - cum[:, :, None, :, :]
        decay = jnp.exp(jnp.where(causal, diff, -jnp.inf))
        att = jnp.einsum('bhtd,bhsd,bhtsd->bhts', qb, kb, decay)
        o = o_inter + jnp.einsum('bhts,bhsv->bhtv', att, vb)
        last = cum[:, :, -1:, :]
        s = jnp.exp(last[:, :, 0, :])[..., None] * s + jnp.einsum('bhsd,bhsv->bhdv', kb * jnp.exp(last - cum), vb)
        return s, o

    s, o = lax.scan(step, s0.astype(jnp.float32), (blocks(q), blocks(k), blocks(v), blocks(log_g)))
    o = o.transpose(1, 0, 3, 2, 4).reshape(b, n * c, h, dv)[:, :t]
    return o, s


def _s5_mixer(u, h_re, h_im, lam_re, lam_im, log_dt, b_re, b_im, c_re, c_im, d, w_glu):
    f32 = jnp.float32
    bsz, t, _ = u.shape
    uf = u.astype(f32)
    lr = jnp.minimum(lam_re.astype(f32), -1e-4)
    li = lam_im.astype(f32)
    dt = jnp.exp(log_dt.astype(f32))[:, None]
    mag = jnp.exp(lr * dt)
    ab_re = mag * jnp.cos(li * dt)
    ab_im = mag * jnp.sin(li * dt)
    den = lr * lr + li * li
    f_re = ((ab_re - 1.0) * lr + ab_im * li) / den
    f_im = (ab_im * lr - (ab_re - 1.0) * li) / den
    br = b_re.astype(f32)
    bi = b_im.astype(f32)
    bb_re = f_re[..., None] * br - f_im[..., None] * bi
    bb_im = f_re[..., None] * bi + f_im[..., None] * br
    ug = uf.reshape(bsz, t, S5_GROUPS, S5_GROUP)
    bu_re = jnp.einsum('gnp,btgp->btgn', bb_re, ug)
    bu_im = jnp.einsum('gnp,btgp->btgn', bb_im, ug)
    h_re = h_re.astype(f32)
    h_im = h_im.astype(f32)
    bu_re = bu_re.at[:, 0].add(ab_re * h_re - ab_im * h_im)
    bu_im = bu_im.at[:, 0].add(ab_re * h_im + ab_im * h_re)
    a_re = jnp.broadcast_to(ab_re, bu_re.shape)
    a_im = jnp.broadcast_to(ab_im, bu_im.shape)

    def combine(e1, e2):
        a1r, a1i, b1r, b1i = e1
        a2r, a2i, b2r, b2i = e2
        return (a2r * a1r - a2i * a1i, a2r * a1i + a2i * a1r,
                a2r * b1r - a2i * b1i + b2r, a2r * b1i + a2i * b1r + b2i)

    _, _, hs_re, hs_im = lax.associative_scan(combine, (a_re, a_im, bu_re, bu_im), axis=1)
    y = jnp.einsum('gpn,btgn->btgp', c_re.astype(f32), hs_re) - jnp.einsum('gpn,btgn->btgp', c_im.astype(f32), hs_im)
    y = y.reshape(bsz, t, D_MODEL) + d.astype(f32) * uf
    z = jax.nn.gelu(y).astype(u.dtype)
    out = z * jax.nn.sigmoid(z @ w_glu)
    return out, hs_re[:, -1], hs_im[:, -1]


def _rwkv7_mixer(xn, shift_prev, wkv0, mix, w_r, w_k, w_v, w_o, w0, w1, w2, a0, a1, a2, g1, g2, k_k, k_a, r_k, ln_w, ln_b):
    f32 = jnp.float32
    bsz, t, _ = xn.shape
    x_prev = jnp.concatenate([shift_prev[:, None].astype(xn.dtype), xn[:, :-1]], axis=1)
    xx = x_prev - xn
    xr = xn + xx * mix[0]
    xw = xn + xx * mix[1]
    xk = xn + xx * mix[2]
    xv = xn + xx * mix[3]
    xa = xn + xx * mix[4]
    xg = xn + xx * mix[5]
    r = (xr @ w_r).astype(f32)
    k = (xk @ w_k).astype(f32)
    v = (xv @ w_v).astype(f32)
    w = -jax.nn.softplus(-(w0.astype(f32) + (jnp.tanh(xw @ w1) @ w2).astype(f32))) - 0.5
    decay = jnp.exp(-jnp.exp(w))
    a = jax.nn.sigmoid(a0.astype(f32) + ((xa @ a1) @ a2).astype(f32))
    g = jax.nn.sigmoid(xg @ g1) @ g2
    heads = lambda z: z.reshape(bsz, t, RW_HEADS, RW_HEAD)
    kk = heads(k * k_k.astype(f32))
    kk = kk / jnp.maximum(jnp.linalg.norm(kk, axis=-1, keepdims=True), 1e-12)
    k = k * (1.0 + (a - 1.0) * k_a.astype(f32))
    rh = heads(r)
    kh = heads(k)
    vh = heads(v)
    ah = heads(a)
    wh = heads(decay)
    seq_first = lambda z: jnp.moveaxis(z, 1, 0)

    def step(s, inp):
        r_t, w_t, k_t, v_t, a_t, b_t = inp
        sa = jnp.einsum('bhij,bhj->bhi', s, a_t)
        s = s * w_t[:, :, None, :] + sa[..., None] * b_t[:, :, None, :] + v_t[..., None] * k_t[:, :, None, :]
        return s, jnp.einsum('bhij,bhj->bhi', s, r_t)

    xs = (seq_first(rh), seq_first(wh), seq_first(kh), seq_first(vh), seq_first(-kk), seq_first(kk * ah))
    s, o = lax.scan(step, wkv0.astype(f32), xs)
    o = jnp.moveaxis(o, 0, 1)
    mu = jnp.mean(o, axis=-1, keepdims=True)
    var = jnp.mean(jnp.square(o - mu), axis=-1, keepdims=True)
    o = ((o - mu) * lax.rsqrt(var + RW_LN_EPS)).reshape(bsz, t, D_MODEL) * ln_w.astype(f32) + ln_b.astype(f32)
    bonus = jnp.sum(rh * kh * r_k.astype(f32), axis=-1, keepdims=True) * vh
    o = o + bonus.reshape(bsz, t, D_MODEL)
    out = (o.astype(xn.dtype) * g) @ w_o
    return out, xn[:, -1], s


def _gla_mixer(xn, s0, w_q, w_k, w_v, w_gk1, w_gk2, b_gk, w_g, norm_w, w_o):
    f32 = jnp.float32
    bsz, t, _ = xn.shape
    q = (xn @ w_q).astype(f32).reshape(bsz, t, GLA_HEADS, GLA_DK) * (GLA_DK ** -0.5)
    k = (xn @ w_k).astype(f32).reshape(bsz, t, GLA_HEADS, GLA_DK)
    v = (xn @ w_v).astype(f32).reshape(bsz, t, GLA_HEADS, GLA_DV)
    log_g = jax.nn.log_sigmoid(((xn @ w_gk1) @ w_gk2).astype(f32) + b_gk.astype(f32)) / GLA_GATE_TEMP
    o, s = _chunked_gated_la(q, k, v, log_g.reshape(bsz, t, GLA_HEADS, GLA_DK), s0)
    o = _headnorm(o, norm_w).reshape(bsz, t, GLA_VAL)
    out = (o.astype(xn.dtype) * jax.nn.silu(xn @ w_g)) @ w_o
    return out, s


def _hgrn2_mixer(xn, s0, layer_idx, w_q, w_f, w_i, w_g, lb_param, norm_w, w_o):
    f32 = jnp.float32
    bsz, t, _ = xn.shape
    p = jax.nn.softmax(lb_param.astype(f32), axis=0)
    lb = (jnp.cumsum(p, axis=0) - p[0])[layer_idx]
    q = (xn @ w_q).astype(f32)
    z = (xn @ w_f).astype(f32)
    i = (xn @ w_i).astype(f32)
    log_f = jnp.logaddexp(jnp.log(lb), jnp.log1p(-lb) + jax.nn.log_sigmoid(z))
    k = jnp.exp(jnp.log1p(-lb) + jax.nn.log_sigmoid(-z))
    hk = lambda a: a.reshape(bsz, t, HG_HEADS, HG_DK)
    o, s = _chunked_gated_la(hk(q) * (HG_DK ** -0.5), hk(k), i.reshape(bsz, t, HG_HEADS, HG_DV), hk(log_f), s0)
    o = _headnorm(o, norm_w).reshape(bsz, t, D_MODEL)
    out = (o.astype(xn.dtype) * jax.nn.silu(xn @ w_g)) @ w_o
    return out, s


def _conv_ffn(xn, prev, w_up, w_gate, w_conv, b_conv, w_down):
    u = xn @ w_up
    gate = xn @ w_gate
    full = jnp.concatenate([prev.astype(u.dtype), u], axis=1)
    t = u.shape[1]
    c = sum(full[:, j:j + t] * w_conv[j] for j in range(CONV_W)) + b_conv
    out = (jax.nn.gelu(c) * gate) @ w_down
    return out, full[:, -(CONV_W - 1):]


def _trunk(x, s5_re, s5_im, rw_shift, rw_wkv, gla_s, hg_s, ffn_conv, p):
    h = x
    conv_new = []
    for i in range(DEPTH):
        xn = _rmsnorm(h, p['norm_mix'][i])
        kind = i % N_MIXERS
        if kind == 0:
            mix, s5_re, s5_im = _s5_mixer(xn, s5_re, s5_im, p['s5_lambda_re'], p['s5_lambda_im'], p['s5_log_dt'],
                                          p['s5_b_re'], p['s5_b_im'], p['s5_c_re'], p['s5_c_im'], p['s5_d'], p['s5_w_glu'])
        elif kind == 1:
            mix, rw_shift, rw_wkv = _rwkv7_mixer(xn, rw_shift, rw_wkv, p['rw_mix'], p['rw_w_r'], p['rw_w_k'], p['rw_w_v'],
                                                 p['rw_w_o'], p['rw_w0'], p['rw_w1'], p['rw_w2'], p['rw_a0'], p['rw_a1'],
                                                 p['rw_a2'], p['rw_g1'], p['rw_g2'], p['rw_k_k'], p['rw_k_a'], p['rw_r_k'],
                                                 p['rw_ln_w'], p['rw_ln_b'])
        elif kind == 2:
            mix, gla_s = _gla_mixer(xn, gla_s, p['gla_w_q'], p['gla_w_k'], p['gla_w_v'], p['gla_w_gk1'], p['gla_w_gk2'],
                                    p['gla_b_gk'], p['gla_w_g'], p['gla_norm'], p['gla_w_o'])
        else:
            mix, hg_s = _hgrn2_mixer(xn, hg_s, i, p['hg_w_q'], p['hg_w_f'], p['hg_w_i'], p['hg_w_g'], p['hg_lb'],
                                     p['hg_norm'], p['hg_w_o'])
        h = h + mix.astype(h.dtype)
        f, c = _conv_ffn(_rmsnorm(h, p['norm_ffn'][i]), ffn_conv[i], p['ffn_w_up'][i], p['ffn_w_gate'][i],
                         p['ffn_w_conv'][i], p['ffn_b_conv'][i], p['ffn_w_down'][i])
        h = h + f.astype(h.dtype)
        conv_new.append(c)
    y = _rmsnorm(h, p['norm_final'])
    return y, s5_re, s5_im, rw_shift, rw_wkv, gla_s, hg_s, jnp.stack(conv_new)


def setup_inputs(seed: int = 0) -> dict:
    key = jax.random.key(seed)
    ks = iter(jax.random.split(key, 96))
    f32 = jnp.float32
    nrm = lambda shape, scale: jax.random.normal(next(ks), shape, f32) * scale
    unif = lambda shape, lo, hi: jax.random.uniform(next(ks), shape, f32, lo, hi)
    D = D_MODEL
    G, N, P = S5_GROUPS, S5_STATE, S5_GROUP
    ratio = jnp.arange(D, dtype=f32) / (D - 1)
    conv_base = jnp.zeros((CONV_W,), f32).at[-1].set(1.0)[None, :, None]
    return {
        'x_prompt': nrm((BATCH, SEQ, D), 1.0),
        'x_sample': nrm((DEC_BATCH, DEC_SEQ, D), 1.0),
        'state_s5_re': nrm((DEC_BATCH, G, N), 0.5),
        'state_s5_im': nrm((DEC_BATCH, G, N), 0.5),
        'state_rwkv_shift': nrm((DEC_BATCH, D), 1.0),
        'state_rwkv_wkv': nrm((DEC_BATCH, RW_HEADS, RW_HEAD, RW_HEAD), 0.3),
        'state_gla': nrm((DEC_BATCH, GLA_HEADS, GLA_DK, GLA_DV), 0.1),
        'state_hgrn': nrm((DEC_BATCH, HG_HEADS, HG_DK, HG_DV), 0.1),
        'state_ffn_conv': nrm((DEPTH, DEC_BATCH, CONV_W - 1, D_FF), 1.0),
        'norm_mix': 1.0 + nrm((DEPTH, D), 0.02),
        'norm_ffn': 1.0 + nrm((DEPTH, D), 0.02),
        'norm_final': 1.0 + nrm((D,), 0.02),
        's5_lambda_re': -0.5 + nrm((G, N), 0.01),
        's5_lambda_im': math.pi * jnp.arange(N, dtype=f32)[None, :] + nrm((G, N), 0.01),
        's5_log_dt': unif((G,), math.log(1e-3), math.log(1e-1)),
        's5_b_re': nrm((G, N, P), (2 * P) ** -0.5),
        's5_b_im': nrm((G, N, P), (2 * P) ** -0.5),
        's5_c_re': nrm((G, P, N), (2 * N) ** -0.5),
        's5_c_im': nrm((G, P, N), (2 * N) ** -0.5),
        's5_d': nrm((D,), 1.0),
        's5_w_glu': nrm((D, D), D ** -0.5),
        'rw_mix': unif((6, D), 0.0, 1.0),
        'rw_w_r': nrm((D, D), D ** -0.5),
        'rw_w_k': nrm((D, D), D ** -0.5),
        'rw_w_v': nrm((D, D), D ** -0.5),
        'rw_w_o': nrm((D, D), D ** -0.5),
        'rw_w0': -6.0 + 5.0 * ratio ** 0.9 + nrm((D,), 0.1),
        'rw_w1': nrm((D, RW_DECAY_LORA), D ** -0.5),
        'rw_w2': nrm((RW_DECAY_LORA, D), 0.1 * RW_DECAY_LORA ** -0.5),
        'rw_a0': nrm((D,), 0.1),
        'rw_a1': nrm((D, RW_A_LORA), D ** -0.5),
        'rw_a2': nrm((RW_A_LORA, D), 0.1 * RW_A_LORA ** -0.5),
        'rw_g1': nrm((D, RW_GATE_LORA), D ** -0.5),
        'rw_g2': nrm((RW_GATE_LORA, D), RW_GATE_LORA ** -0.5),
        'rw_k_k': 0.85 + nrm((D,), 0.02),
        'rw_k_a': 1.0 + nrm((D,), 0.02),
        'rw_r_k': nrm((RW_HEADS, RW_HEAD), 0.1),
        'rw_ln_w': 1.0 + nrm((D,), 0.02),
        'rw_ln_b': nrm((D,), 0.02),
        'gla_w_q': nrm((D, GLA_KEY), D ** -0.5),
        'gla_w_k': nrm((D, GLA_KEY), D ** -0.5),
        'gla_w_v': nrm((D, GLA_VAL), D ** -0.5),
        'gla_w_gk1': nrm((D, GLA_GATE_RANK), D ** -0.5),
        'gla_w_gk2': nrm((GLA_GATE_RANK, GLA_KEY), GLA_GATE_RANK ** -0.5),
        'gla_b_gk': nrm((GLA_KEY,), 0.1),
        'gla_w_g': nrm((D, GLA_VAL), D ** -0.5),
        'gla_norm': 1.0 + nrm((GLA_DV,), 0.02),
        'gla_w_o': nrm((GLA_VAL, D), GLA_VAL ** -0.5),
        'hg_w_q': nrm((D, D), D ** -0.5),
        'hg_w_f': nrm((D, D), D ** -0.5),
        'hg_w_i': nrm((D, D), D ** -0.5),
        'hg_w_g': nrm((D, D), D ** -0.5),
        'hg_lb': nrm((DEPTH, D), 0.1),
        'hg_norm': 1.0 + nrm((HG_DV,), 0.02),
        'hg_w_o': nrm((D, D), D ** -0.5),
        'ffn_w_up': nrm((DEPTH, D, D_FF), D ** -0.5),
        'ffn_w_gate': nrm((DEPTH, D, D_FF), D ** -0.5),
        'ffn_w_conv': conv_base + nrm((DEPTH, CONV_W, D_FF), 0.2),
        'ffn_b_conv': nrm((DEPTH, D_FF), 0.02),
        'ffn_w_down': nrm((DEPTH, D_FF, D), D_FF ** -0.5),
    }


def reference(x_prompt, x_sample, state_s5_re, state_s5_im, state_rwkv_shift, state_rwkv_wkv, state_gla, state_hgrn,
              state_ffn_conv, norm_mix, norm_ffn, norm_final, s5_lambda_re, s5_lambda_im, s5_log_dt, s5_b_re, s5_b_im,
              s5_c_re, s5_c_im, s5_d, s5_w_glu, rw_mix, rw_w_r, rw_w_k, rw_w_v, rw_w_o, rw_w0, rw_w1, rw_w2, rw_a0,
              rw_a1, rw_a2, rw_g1, rw_g2, rw_k_k, rw_k_a, rw_r_k, rw_ln_w, rw_ln_b, gla_w_q, gla_w_k, gla_w_v,
              gla_w_gk1, gla_w_gk2, gla_b_gk, gla_w_g, gla_norm, gla_w_o, hg_w_q, hg_w_f, hg_w_i, hg_w_g, hg_lb,
              hg_norm, hg_w_o, ffn_w_up, ffn_w_gate, ffn_w_conv, ffn_b_conv, ffn_w_down):
    p = {
        'norm_mix': norm_mix, 'norm_ffn': norm_ffn, 'norm_final': norm_final,
        's5_lambda_re': s5_lambda_re, 's5_lambda_im': s5_lambda_im, 's5_log_dt': s5_log_dt,
        's5_b_re': s5_b_re, 's5_b_im': s5_b_im, 's5_c_re': s5_c_re, 's5_c_im': s5_c_im, 's5_d': s5_d,
        's5_w_glu': s5_w_glu,
        'rw_mix': rw_mix, 'rw_w_r': rw_w_r, 'rw_w_k': rw_w_k, 'rw_w_v': rw_w_v, 'rw_w_o': rw_w_o,
        'rw_w0': rw_w0, 'rw_w1': rw_w1, 'rw_w2': rw_w2, 'rw_a0': rw_a0, 'rw_a1': rw_a1, 'rw_a2': rw_a2,
        'rw_g1': rw_g1, 'rw_g2': rw_g2, 'rw_k_k': rw_k_k, 'rw_k_a': rw_k_a, 'rw_r_k': rw_r_k,
        'rw_ln_w': rw_ln_w, 'rw_ln_b': rw_ln_b,
        'gla_w_q': gla_w_q, 'gla_w_k': gla_w_k, 'gla_w_v': gla_w_v, 'gla_w_gk1': gla_w_gk1,
        'gla_w_gk2': gla_w_gk2, 'gla_b_gk': gla_b_gk, 'gla_w_g': gla_w_g, 'gla_norm': gla_norm,
        'gla_w_o': gla_w_o,
        'hg_w_q': hg_w_q, 'hg_w_f': hg_w_f, 'hg_w_i': hg_w_i, 'hg_w_g': hg_w_g, 'hg_lb': hg_lb,
        'hg_norm': hg_norm, 'hg_w_o': hg_w_o,
        'ffn_w_up': ffn_w_up, 'ffn_w_gate': ffn_w_gate, 'ffn_w_conv': ffn_w_conv, 'ffn_b_conv': ffn_b_conv,
        'ffn_w_down': ffn_w_down,
    }
    f32 = jnp.float32
    nb = x_prompt.shape[0]
    z_s5 = jnp.zeros((nb, S5_GROUPS, S5_STATE), f32)
    (y_prompt, s5_re_p, s5_im_p, rw_shift_p, rw_wkv_p, gla_p, hgrn_p, ffn_conv_p) = _trunk(
        x_prompt, z_s5, z_s5, jnp.zeros((nb, D_MODEL), x_prompt.dtype),
        jnp.zeros((nb, RW_HEADS, RW_HEAD, RW_HEAD), f32), jnp.zeros((nb, GLA_HEADS, GLA_DK, GLA_DV), f32),
        jnp.zeros((nb, HG_HEADS, HG_DK, HG_DV), f32), jnp.zeros((DEPTH, nb, CONV_W - 1, D_FF), x_prompt.dtype), p)
    (y_sample, s5_re_s, s5_im_s, rw_shift_s, rw_wkv_s, gla_s, hgrn_s, ffn_conv_s) = _trunk(
        x_sample, state_s5_re, state_s5_im, state_rwkv_shift, state_rwkv_wkv, state_gla, state_hgrn,
        state_ffn_conv, p)
    return (y_prompt, y_sample, s5_re_p, s5_im_p, rw_shift_p, rw_wkv_p, gla_p, hgrn_p, ffn_conv_p,
            s5_re_s, s5_im_s, rw_shift_s, rw_wkv_s, gla_s, hgrn_s, ffn_conv_s)
```

```python
import functools
import math

import jax
import jax.numpy as jnp
from jax import lax
from jax.experimental import pallas as pl
from jax.experimental.pallas import tpu as pltpu

F32 = jnp.float32
BF16 = jnp.bfloat16

NORM_EPS = 1e-6
RW_LN_EPS = 64e-5
RW_HEAD = 64
S5_GROUP = 16
S5_STATE = 64
S5_CHUNK = 16
GLA_HEADS = 4
GLA_GATE_TEMP = 16.0
HG_EXPAND = 128
CONV_W = 3
LA_CHUNK = 64
RW_CHUNK = 64
STEP_PAD = 16
LANES = 128
SUBLANES = 8
VMEM_LIMIT = 56 * 1024 * 1024


def _cparams(sem):
    return pltpu.CompilerParams(dimension_semantics=sem, vmem_limit_bytes=VMEM_LIMIT)


def _bdot(a, b):
    return jnp.dot(a.astype(BF16), b.astype(BF16), preferred_element_type=F32)


def _bdot_nt(a, b):
    return lax.dot_general(a.astype(BF16), b.astype(BF16), (((1,), (1,)), ((), ())),
                           preferred_element_type=F32)


def _bdot_tn(a, b):
    return lax.dot_general(a.astype(BF16), b.astype(BF16), (((0,), (0,)), ((), ())),
                           preferred_element_type=F32)


def _split3(x):
    hi = x.astype(BF16)
    r = x - hi.astype(F32)
    mid = r.astype(BF16)
    lo = (r - mid.astype(F32)).astype(BF16)
    return hi, mid, lo


def _xdot_l(m, x):
    hi, mid, lo = _split3(x)
    d = lambda p: jnp.dot(m, p, preferred_element_type=F32)
    return d(hi) + d(mid) + d(lo)


def _xdot_r(x, m):
    hi, mid, lo = _split3(x)
    d = lambda p: jnp.dot(p, m, preferred_element_type=F32)
    return d(hi) + d(mid) + d(lo)


def _rms(x, g):
    y = x * lax.rsqrt(jnp.mean(x * x, axis=-1, keepdims=True) + NORM_EPS)
    return y * g


def _iota(shape, dim):
    return lax.broadcasted_iota(jnp.int32, shape, dim)


def _rmsnorm_kernel(x_ref, g_ref, o_ref):
    o_ref[...] = _rms(x_ref[...], g_ref[...])


def _rmsnorm_call(x, g):
    r, d = x.shape
    tr = 512 if r % 512 == 0 else r
    return pl.pallas_call(
        _rmsnorm_kernel,
        grid=(r // tr,),
        in_specs=[pl.BlockSpec((tr, d), lambda i: (i, 0)), pl.BlockSpec((1, d), lambda i: (0, 0))],
        out_specs=pl.BlockSpec((tr, d), lambda i: (i, 0)),
        out_shape=jax.ShapeDtypeStruct((r, d), F32),
        compiler_params=_cparams(("parallel",)),
        name="rmsnorm",
    )(x, g.reshape(1, d))


def _proj_call(body, *, m, n, tm, tn, row_ins, full_ins, col_ins, rc_ins, out_dtypes, scratch, name):
    args, specs = [], []
    for a in row_ins:
        if isinstance(a, tuple):
            args.append(a[0]); specs.append(a[1])
        else:
            args.append(a); specs.append(pl.BlockSpec((tm, a.shape[1]), lambda i, j: (i, 0)))
    for a in full_ins:
        args.append(a); specs.append(pl.BlockSpec(a.shape, lambda i, j, nd=a.ndim: (0,) * nd))
    for a in col_ins:
        args.append(a); specs.append(pl.BlockSpec((a.shape[0], tn), lambda i, j: (0, j)))
    for a in rc_ins:
        args.append(a); specs.append(pl.BlockSpec((tm, tn), lambda i, j: (i, j)))
    nr, nf, nc, nrc, no = len(row_ins), len(full_ins), len(col_ins), len(rc_ins), len(out_dtypes)

    def kernel(*refs):
        p = 0
        groups = []
        for cnt in (nr, nf, nc, nrc, no):
            groups.append(refs[p:p + cnt]); p += cnt
        body(*groups, refs[p:])

    outs = pl.pallas_call(
        kernel,
        grid=(m // tm, n // tn),
        in_specs=specs,
        out_specs=[pl.BlockSpec((tm, tn), lambda i, j: (i, j)) for _ in out_dtypes],
        out_shape=[jax.ShapeDtypeStruct((m, n), dt) for dt in out_dtypes],
        scratch_shapes=scratch,
        compiler_params=_cparams(("parallel", "arbitrary")),
        name=name,
    )(*args)
    return outs


def _row(v):
    return v.reshape(1, -1)


def _tiles(m, t):
    if t == 1:
        return 512 if m % 512 == 0 else m
    return min(512, t)


def _s5_params(lam_re, lam_im, log_dt, b_re, b_im, c_re, c_im, d):
    hp = lax.Precision.HIGHEST
    g, n = lam_re.shape
    p = b_re.shape[-1]
    cl = S5_CHUNK
    lr = jnp.minimum(lam_re.astype(F32), -1e-4)
    li = lam_im.astype(F32)
    dt = jnp.exp(log_dt.astype(F32))[:, None]
    mag = jnp.exp(lr * dt)
    ab_re = mag * jnp.cos(li * dt)
    ab_im = mag * jnp.sin(li * dt)
    den = lr * lr + li * li
    f_re = ((ab_re - 1.0) * lr + ab_im * li) / den
    f_im = (ab_im * lr - (ab_re - 1.0) * li) / den
    br, bi = b_re.astype(F32), b_im.astype(F32)
    bb_re = f_re[..., None] * br - f_im[..., None] * bi
    bb_im = f_re[..., None] * bi + f_im[..., None] * br
    cr, ci = c_re.astype(F32), c_im.astype(F32)

    pr, pi = [jnp.ones_like(ab_re)], [jnp.zeros_like(ab_im)]
    for _ in range(cl):
        pr.append(pr[-1] * ab_re - pi[-1] * ab_im)
        pi.append(pr[-2] * ab_im + pi[-1] * ab_re)
    pw_re = jnp.stack(pr)
    pw_im = jnp.stack(pi)
    ajb_re = pw_re[:cl, :, :, None] * bb_re - pw_im[:cl, :, :, None] * bb_im
    ajb_im = pw_re[:cl, :, :, None] * bb_im + pw_im[:cl, :, :, None] * bb_re
    kj = (jnp.einsum('gpn,jgnq->jgpq', cr, ajb_re, precision=hp)
          - jnp.einsum('gpn,jgnq->jgpq', ci, ajb_im, precision=hp))
    s_idx = jnp.arange(cl)[:, None]
    t_idx = jnp.arange(cl)[None, :]
    lag = t_idx - s_idx
    kst = kj[jnp.clip(lag, 0, cl - 1)]
    kst = jnp.where((lag >= 0)[:, :, None, None, None], kst, 0.0)
    mt = kst.transpose(2, 0, 4, 1, 3).reshape(g, cl * p, cl * p)
    w_re = ajb_re[::-1].transpose(1, 0, 3, 2).reshape(g, cl * p, n)
    w_im = ajb_im[::-1].transpose(1, 0, 3, 2).reshape(g, cl * p, n)
    wm = jnp.concatenate([w_re, w_im], axis=-1)
    ca_re = cr[None] * pw_re[1:, :, None, :] - ci[None] * pw_im[1:, :, None, :]
    ca_im = cr[None] * pw_im[1:, :, None, :] + ci[None] * pw_re[1:, :, None, :]
    v_re = ca_re.transpose(1, 3, 0, 2).reshape(g, n, cl * p)
    v_im = -ca_im.transpose(1, 3, 0, 2).reshape(g, n, cl * p)
    vm = jnp.concatenate([v_re, v_im], axis=1)
    ar, ai = pw_re[cl], pw_im[cl]
    pcs, qcs = [], []
    for _ in range(SUBLANES):
        pcs.append(jnp.concatenate([ar, ar], axis=-1))
        qcs.append(jnp.concatenate([-ai, ai], axis=-1))
        ar, ai = ar * ar - ai * ai, 2.0 * ar * ai
    pc = jnp.stack(pcs, axis=1)
    qc = jnp.stack(qcs, axis=1)
    dtile = jnp.tile(d.astype(F32).reshape(g, 1, p), (1, 1, cl))
    return dict(ab_re=ab_re, ab_im=ab_im, bb_re=bb_re, bb_im=bb_im, mt=mt.astype(BF16), wm=wm.astype(BF16),
                vm=vm.astype(BF16), pc=pc, qc=qc, dtile=dtile)


def _s5_seq_kernel(u_ref, mt_ref, w_ref, vm_ref, pc_ref, qc_ref, dt_ref, h0_ref, z_ref, hf_ref, *, gb, nc):
    half = S5_STATE
    rid = _iota((nc, 1), 0)
    nlev = int(math.log2(nc))
    for g in range(gb):
        u = u_ref[0, g]
        ub = u.astype(BF16)
        y_intra = jnp.dot(ub, mt_ref[g], preferred_element_type=F32)
        x = jnp.dot(ub, w_ref[g], preferred_element_type=F32)
        pc = pc_ref[g]
        qc = qc_ref[g]
        h0 = h0_ref[0, g]
        h0c = pc[0:1] * h0 + qc[0:1] * pltpu.roll(h0, half, 1)
        x = x + jnp.where(rid == 0, h0c, 0.0)
        for lv in range(nlev):
            sh = 1 << lv
            xs = jnp.where(rid >= sh, pltpu.roll(x, sh, 0), 0.0)
            x = x + pc[lv:lv + 1] * xs + qc[lv:lv + 1] * pltpu.roll(xs, half, 1)
        hprev = jnp.where(rid == 0, h0, pltpu.roll(x, 1, 0))
        y = y_intra + jnp.dot(hprev.astype(BF16), vm_ref[g], preferred_element_type=F32) + dt_ref[g] * u
        z_ref[0, g] = jax.nn.gelu(y)
        hf_ref[0, g] = x[nc - 1:nc, :]


def _s5_seq_call(u4, prm, h0):
    b, g, nc, w = u4.shape
    n2 = 2 * S5_STATE
    gb = 8
    kern = functools.partial(_s5_seq_kernel, gb=gb, nc=nc)
    gspec = lambda shape: pl.BlockSpec((gb,) + shape, lambda gi, bi: (gi, 0, 0))
    return pl.pallas_call(
        kern,
        grid=(g // gb, b),
        in_specs=[pl.BlockSpec((1, gb, nc, w), lambda gi, bi: (bi, gi, 0, 0)),
                  gspec((w, w)), gspec((w, n2)), gspec((n2, w)), gspec((SUBLANES, n2)), gspec((SUBLANES, n2)),
                  gspec((1, w)),
                  pl.BlockSpec((1, gb, 1, n2), lambda gi, bi: (bi, gi, 0, 0))],
        out_specs=[pl.BlockSpec((1, gb, nc, w), lambda gi, bi: (bi, gi, 0, 0)),
                   pl.BlockSpec((1, gb, 1, n2), lambda gi, bi: (bi, gi, 0, 0))],
        out_shape=[jax.ShapeDtypeStruct((b, g, nc, w), F32), jax.ShapeDtypeStruct((b, g, 1, n2), F32)],
        compiler_params=_cparams(("parallel", "arbitrary")),
        name="s5_seq",
    )(u4, prm['mt'], prm['wm'], prm['vm'], prm['pc'], prm['qc'], prm['dtile'], h0)


def _s5_step_kernel(u_ref, hr_ref, hi_ref, ar_ref, ai_ref, bdr_ref, bdi_ref, cdr_ref, cdi_ref, d_ref,
                    z_ref, hro_ref, hio_ref):
    u = u_ref[...]
    ar, ai = ar_ref[...], ai_ref[...]
    hr, hi = hr_ref[...], hi_ref[...]
    hr2 = ar * hr - ai * hi + _bdot(u, bdr_ref[0])
    hi2 = ar * hi + ai * hr + _bdot(u, bdi_ref[0])
    y = _bdot(hr2, cdr_ref[0]) - _bdot(hi2, cdi_ref[0]) + d_ref[...] * u
    z_ref[...] = jax.nn.gelu(y)
    hro_ref[...] = hr2
    hio_ref[...] = hi2


def _s5_step_call(u, h_re, h_im, prm, c_re, c_im, d):
    b, dm = u.shape
    g, n = prm['ab_re'].shape
    p = S5_GROUP
    gpb = LANES // p
    nb = g // gpb
    eye = jnp.eye(gpb, dtype=F32)
    bd = lambda bb: jnp.einsum('kgnp,gh->kgphn', bb.reshape(nb, gpb, n, p), eye).reshape(nb, gpb * p, gpb * n)
    cd = lambda cc: jnp.einsum('kgpn,gh->kgnhp', cc.astype(F32).reshape(nb, gpb, p, n), eye).reshape(
        nb, gpb * n, gpb * p)
    bdr, bdi = bd(prm['bb_re']).astype(BF16), bd(prm['bb_im']).astype(BF16)
    cdr, cdi = cd(c_re).astype(BF16), cd(c_im).astype(BF16)
    wn = gpb * n
    cspec = lambda wd: pl.BlockSpec((b, wd), lambda k: (0, k))
    rspec = lambda wd: pl.BlockSpec((1, wd), lambda k: (0, k))
    z, hr2, hi2 = pl.pallas_call(
        _s5_step_kernel,
        grid=(nb,),
        in_specs=[cspec(LANES), cspec(wn), cspec(wn), rspec(wn), rspec(wn),
                  pl.BlockSpec((1, LANES, wn), lambda k: (k, 0, 0)), pl.BlockSpec((1, LANES, wn), lambda k: (k, 0, 0)),
                  pl.BlockSpec((1, wn, LANES), lambda k: (k, 0, 0)), pl.BlockSpec((1, wn, LANES), lambda k: (k, 0, 0)),
                  rspec(LANES)],
        out_specs=[cspec(LANES), cspec(wn), cspec(wn)],
        out_shape=[jax.ShapeDtypeStruct((b, dm), F32), jax.ShapeDtypeStruct((b, g * n), F32),
                   jax.ShapeDtypeStruct((b, g * n), F32)],
        compiler_params=_cparams(("parallel",)),
        name="s5_step",
    )(u, h_re.reshape(b, g * n), h_im.reshape(b, g * n), prm['ab_re'].reshape(1, g * n),
      prm['ab_im'].reshape(1, g * n), bdr, bdi, cdr, cdi, _row(d.astype(F32)))
    return z, hr2.reshape(b, g, n), hi2.reshape(b, g, n)


def _glu_body(row, full, col, rc, out, scr):
    (z_ref,) = row
    (w_ref,) = col
    zt_ref, h_ref = rc
    (zb,) = scr

    @pl.when(pl.program_id(1) == 0)
    def _():
        zb[...] = z_ref[...].astype(BF16)

    zt = zt_ref[...]
    gate = jax.nn.sigmoid(jnp.dot(zb[...], w_ref[...], preferred_element_type=F32))
    out[0][...] = h_ref[...] + zt * gate


def _s5_layer(h, bsz, t, st_re, st_im, norm_w, prm, c_re, c_im, d, w_glu):
    m, dm = h.shape
    g, n, p = dm // S5_GROUP, S5_STATE, S5_GROUP
    xn = _rmsnorm_call(h, norm_w)
    if t == 1:
        z, s_re, s_im = _s5_step_call(xn, st_re, st_im, prm, c_re, c_im, d)
    else:
        cl = S5_CHUNK
        nc = t // cl
        u4 = xn.reshape(bsz, nc, cl, g, p).transpose(0, 3, 1, 2, 4).reshape(bsz, g, nc, cl * p)
        h0 = jnp.concatenate([st_re, st_im], axis=-1).reshape(bsz, g, 1, 2 * n)
        z4, hf = _s5_seq_call(u4, prm, h0)
        z = z4.reshape(bsz, g, nc, cl, p).transpose(0, 2, 3, 1, 4).reshape(m, dm)
        s_re, s_im = hf[:, :, 0, :n], hf[:, :, 0, n:]
    tm = _tiles(m, t)
    (h_new,) = _proj_call(_glu_body, m=m, n=dm, tm=tm, tn=512, row_ins=[z], full_ins=[], col_ins=[w_glu],
                          rc_ins=[z, h], out_dtypes=[F32], scratch=[pltpu.VMEM((tm, dm), BF16)], name="s5_glu")
    return h_new, s_re, s_im


def _rwkv_proj_body(row, full, col, rc, out, scr, *, tm, seq_tiles, step):
    h_ref, prev_ref = row
    nw_ref, mix_ref, w1_ref, a1_ref, g1_ref, bd_ref = full
    wr, wk, wv, w2, a2, g2, w0, a0, kk_ref, ka_ref = col
    r_o, lw_o, k_o, v_o, a_o, b_o, g_o = out
    xr_s, xk_s, xv_s, tw_s, ta_s, tg_s = scr
    i = pl.program_id(0)

    @pl.when(pl.program_id(1) == 0)
    def _():
        nw = nw_ref[...]
        xn = _rms(h_ref[...], nw)
        if step:
            xprev = prev_ref[...]
        else:
            last = _rms(prev_ref[...], nw)[SUBLANES - 1:SUBLANES, :]
            last = jnp.where(i % seq_tiles == 0, 0.0, last)
            xprev = jnp.where(_iota((tm, 1), 0) == 0, last, pltpu.roll(xn, 1, 0))
        xx = xprev - xn
        mix = mix_ref[...]
        xr_s[...] = (xn + xx * mix[0:1]).astype(BF16)
        xk_s[...] = (xn + xx * mix[2:3]).astype(BF16)
        xv_s[...] = (xn + xx * mix[3:4]).astype(BF16)
        tw_s[...] = jnp.tanh(_bdot(xn + xx * mix[1:2], w1_ref[...])).astype(BF16)
        ta_s[...] = _bdot(xn + xx * mix[4:5], a1_ref[...]).astype(BF16)
        tg_s[...] = jax.nn.sigmoid(_bdot(xn + xx * mix[5:6], g1_ref[...])).astype(BF16)

    dot = lambda a, b: jnp.dot(a[...], b[...], preferred_element_type=F32)
    r = dot(xr_s, wr)
    k = dot(xk_s, wk)
    v = dot(xv_s, wv)
    wraw = -jax.nn.softplus(-(w0[...] + dot(tw_s, w2))) - 0.5
    a = jax.nn.sigmoid(a0[...] + dot(ta_s, a2))
    kk = k * kk_ref[...]
    ss = _xdot_r(kk * kk, bd_ref[...])
    kk = kk / jnp.maximum(jnp.sqrt(ss), 1e-12)
    r_o[...] = r
    lw_o[...] = -jnp.exp(wraw)
    k_o[...] = k * (1.0 + (a - 1.0) * ka_ref[...])
    v_o[...] = v
    a_o[...] = -kk
    b_o[...] = kk * a
    g_o[...] = dot(tg_s, g2)


def _rwkv_chunk_kernel(r_ref, lw_ref, k_ref, v_ref, a_ref, b_ref, gate_ref, lnw_ref, lnb_ref, rk_ref, s0_ref,
                       o_ref, sout_ref, p_scr, *, npair, cl, nch):
    hd = RW_HEAD
    pw = 2 * hd
    lane = _iota((1, pw), 1)
    head0 = lane < hd
    bdmask = (_iota((pw, pw), 0) // hd) == (_iota((pw, pw), 1) // hd)
    e_dup = ((_iota((hd, pw), 1) % hd) == _iota((hd, pw), 0)).astype(BF16)
    e_fold = ((_iota((pw, hd), 0) % hd) == _iota((pw, hd), 1)).astype(BF16)
    ones_bd = bdmask.astype(BF16)
    avg = (bdmask.astype(F32) * (1.0 / hd)).astype(BF16)
    for p in range(npair):
        p_scr[p] = jnp.where(bdmask, _xdot_r(s0_ref[0, p], e_dup), 0.0)
    rl = _iota((cl, 2 * cl), 0)
    cm = _iota((cl, 2 * cl), 1) % cl
    strict2 = rl > cm
    incl2 = rl >= cm
    rsq = _iota((cl, cl), 0)
    csq = _iota((cl, cl), 1)
    tri = (rsq >= csq).astype(BF16)
    eye = (rsq == csq).astype(F32)
    zeros_l = jnp.zeros((cl, pw), F32)
    nround = int(math.log2(cl)) - 1

    def chunk(c, carry):
        r0 = pl.multiple_of(c * cl, cl)
        for p in range(npair):
            sl = slice(p * pw, (p + 1) * pw)
            r = r_ref[pl.ds(r0, cl), sl]
            lw = lw_ref[pl.ds(r0, cl), sl]
            k = k_ref[pl.ds(r0, cl), sl]
            v = v_ref[pl.ds(r0, cl), sl]
            a = a_ref[pl.ds(r0, cl), sl]
            b = b_ref[pl.ds(r0, cl), sl]
            cs = _xdot_l(tri, lw)
            gam = jnp.exp(cs)
            ginv = jnp.exp(-cs)
            x1 = jnp.concatenate([a * jnp.exp(cs - lw), r * gam], axis=0)
            x2 = jnp.concatenate([b * ginv, k * ginv], axis=0)
            pm = p_scr[p]
            y = _bdot_nt(x1, pm)
            a_s, r_s = y[:cl], y[cl:]
            zv = jnp.concatenate([zeros_l, v], axis=0)
            us, bots = [], []
            for hh in range(2):
                lm = head0 if hh == 0 else jnp.logical_not(head0)
                gmat = _bdot_nt(jnp.where(lm, x1, 0.0), x2)
                top = jnp.where(strict2, gmat[:cl], 0.0)
                bots.append(jnp.where(incl2, gmat[cl:], 0.0))
                nmat = top[:, :cl]
                tinv = eye + nmat
                pk = nmat
                for _ in range(nround):
                    pk = _bdot(pk, pk)
                    tinv = tinv + _bdot(tinv, pk)
                us.append(_bdot(tinv, a_s + _bdot(top, zv)))
            u = jnp.where(head0, us[0], us[1])
            uv = jnp.concatenate([u, v], axis=0)
            o = r_s + jnp.where(head0, _bdot(bots[0], uv), _bdot(bots[1], uv))
            p_scr[p] = (pm + jnp.where(bdmask, _bdot_tn(uv, x2), 0.0)) * gam[cl - 1:cl, :]
            mu = _xdot_r(o, avg)
            dlt = o - mu
            var = _xdot_r(dlt * dlt, avg)
            on = dlt * lax.rsqrt(var + RW_LN_EPS) * lnw_ref[:, sl] + lnb_ref[:, sl]
            bonus = _xdot_r(r * k * rk_ref[:, sl], ones_bd) * v
            o_ref[pl.ds(r0, cl), sl] = ((on + bonus) * gate_ref[pl.ds(r0, cl), sl]).astype(o_ref.dtype)
        return carry

    lax.fori_loop(0, nch, chunk, 0)
    for p in range(npair):
        sout_ref[0, p] = _xdot_r(jnp.where(bdmask, p_scr[p], 0.0), e_fold)


def _rwkv_chunk_call(r, lw, k, v, a, b, gate, ln_w, ln_b, r_k, s0, bsz, t, cl):
    m, dm = r.shape
    hd = RW_HEAD
    pw = 2 * hd
    npairs = dm // pw
    npair = 2
    s0p = s0.reshape(bsz, npairs, pw, hd)
    kern = functools.partial(_rwkv_chunk_kernel, npair=npair, cl=cl, nch=t // cl)
    tspec = pl.BlockSpec((t, npair * pw), lambda bi, pi: (bi, pi))
    vspec = pl.BlockSpec((1, npair * pw), lambda bi, pi: (0, pi))
    sspec = pl.BlockSpec((1, npair, pw, hd), lambda bi, pi: (bi, pi, 0, 0))
    o, s = pl.pallas_call(
        kern,
        grid=(bsz, npairs // npair),
        in_specs=[tspec] * 7 + [vspec] * 3 + [sspec],
        out_specs=[tspec, sspec],
        out_shape=[jax.ShapeDtypeStruct((m, dm), BF16), jax.ShapeDtypeStruct((bsz, npairs, pw, hd), F32)],
        scratch_shapes=[pltpu.VMEM((npair, pw, pw), F32)],
        compiler_params=_cparams(("parallel", "parallel")),
        name="rwkv_chunk",
    )(r, lw, k, v, a, b, gate, _row(ln_w), _row(ln_b), _row(r_k), s0p)
    return o, s.reshape(bsz, dm // hd, hd, hd)


def _outproj_body(row, full, col, rc, out, scr):
    out[0][...] = rc[0][...] + jnp.dot(row[0][...], col[0][...], preferred_element_type=F32)


def _outproj_call(a, w, h, t, name):
    m, kdim = a.shape
    n = w.shape[1]
    (o,) = _proj_call(_outproj_body, m=m, n=n, tm=_tiles(m, t), tn=512, row_ins=[a], full_ins=[], col_ins=[w],
                      rc_ins=[h], out_dtypes=[F32], scratch=[], name=name)
    return o


def _pad_steps(x, bsz):
    n = x.shape[-1]
    return jnp.pad(x.reshape(bsz, 1, n), ((0, 0), (0, STEP_PAD - 1), (0, 0))).reshape(bsz * STEP_PAD, n)


def _first_steps(x, bsz):
    return x.reshape(bsz, STEP_PAD, x.shape[-1])[:, 0]


def _rwkv_layer(h, bsz, t, shift, wkv, norm_w, wts):
    m, dm = h.shape
    tm = _tiles(m, t)
    step = t == 1
    if step:
        prev = (shift.astype(F32), pl.BlockSpec((tm, dm), lambda i, j: (i, 0)))
        seq_tiles = 1
    else:
        seq_tiles = t // tm
        prev = (h, pl.BlockSpec((SUBLANES, dm), lambda i, j: (jnp.maximum(i * (tm // SUBLANES) - 1, 0), 0)))
    tn = 256
    bd = (jnp.arange(tn)[:, None] // RW_HEAD == jnp.arange(tn)[None, :] // RW_HEAD).astype(BF16)
    body = functools.partial(_rwkv_proj_body, tm=tm, seq_tiles=seq_tiles, step=step)
    lora = wts['w1'].shape[1]
    glora = wts['g1'].shape[1]
    r, lw, k, v, a, b, g = _proj_call(
        body, m=m, n=dm, tm=tm, tn=tn, row_ins=[h, prev],
        full_ins=[_row(norm_w), wts['mix'], wts['w1'], wts['a1'], wts['g1'], bd],
        col_ins=[wts['w_r'], wts['w_k'], wts['w_v'], wts['w2'], wts['a2'], wts['g2'],
                 _row(wts['w0']), _row(wts['a0']), _row(wts['k_k']), _row(wts['k_a'])],
        rc_ins=[], out_dtypes=[F32] * 7,
        scratch=[pltpu.VMEM((tm, dm), BF16)] * 3 + [pltpu.VMEM((tm, lora), BF16)] * 2
        + [pltpu.VMEM((tm, glora), BF16)],
        name="rwkv_proj")
    if step:
        shift_new = _rmsnorm_call(h, norm_w)
        seqs = [_pad_steps(x, bsz) for x in (r, lw, k, v, a, b, g)]
        og, s_new = _rwkv_chunk_call(*seqs, wts['ln_w'], wts['ln_b'], wts['r_k'], wkv, bsz, STEP_PAD, STEP_PAD)
        og = _first_steps(og, bsz)
    else:
        shift_new = _rmsnorm_call(h.reshape(bsz, t, dm)[:, -1], norm_w)
        og, s_new = _rwkv_chunk_call(r, lw, k, v, a, b, g, wts['ln_w'], wts['ln_b'], wts['r_k'], wkv, bsz, t,
                                     RW_CHUNK)
    return _outproj_call(og, wts['w_o'], h, t, "rwkv_out"), shift_new, s_new


def _la_chunk_kernel(q_ref, k_ref, v_ref, g_ref, gate_ref, nw_ref, s0_ref, o_ref, sout_ref, s_scr,
                     *, hb, dk, dv, cl, nch):
    s_scr[...] = s0_ref[0]
    rowi = _iota((cl, cl), 0)
    coli = _iota((cl, cl), 1)
    tri = (rowi >= coli).astype(BF16)
    levels = [s for s in (32, 16, 8) if 2 * s <= cl]
    masks = {s: ((rowi // s) % 2 == 1) & ((coli // s) == (rowi // s) - 1) for s in levels}
    rid = _iota((cl, 1), 0)
    nw = nw_ref[...]

    def chunk(c, carry):
        r0 = pl.multiple_of(c * cl, cl)
        for h in range(hb):
            ks = slice(h * dk, (h + 1) * dk)
            vs = slice(h * dv, (h + 1) * dv)
            q = q_ref[pl.ds(r0, cl), ks]
            k = k_ref[pl.ds(r0, cl), ks]
            g = g_ref[pl.ds(r0, cl), ks]
            v = v_ref[pl.ds(r0, cl), vs]
            cum = _xdot_l(tri, g)
            st = s_scr[h]
            o = _bdot(q * jnp.exp(cum), st)
            if levels:
                att = jnp.zeros((cl, cl), F32)
                for s in levels:
                    ends = [jnp.broadcast_to(cum[(bk + 1) * s - 1:(bk + 1) * s, :], (s, dk)) for bk in range(cl // s)]
                    e_blk = jnp.concatenate(ends, axis=0)
                    p_blk = jnp.concatenate([jnp.zeros((s, dk), F32)] + ends[:-1], axis=0)
                    a_s = _bdot_nt(q * jnp.exp(cum - p_blk), k * jnp.exp(e_blk - cum))
                    att = att + jnp.where(masks[s], a_s, 0.0)
                o = o + _bdot(att, v)
            o = o + jnp.sum(q * k, axis=-1, keepdims=True) * v
            for d in range(1, SUBLANES):
                valid = (rid % SUBLANES) >= d
                e = jnp.exp(jnp.where(valid, cum - pltpu.roll(cum, d, 0), -1e30))
                w = jnp.sum(q * pltpu.roll(k, d, 0) * e, axis=-1, keepdims=True)
                o = o + w * pltpu.roll(v, d, 0)
            on = o * lax.rsqrt(jnp.mean(o * o, axis=-1, keepdims=True) + NORM_EPS) * nw
            o_ref[pl.ds(r0, cl), vs] = (on * gate_ref[pl.ds(r0, cl), vs]).astype(o_ref.dtype)
            last = cum[cl - 1:cl, :]
            upd = _bdot_tn(k * jnp.exp(last - cum), v)
            dec = jnp.exp(jnp.transpose(jnp.broadcast_to(last, (LANES, dk))))
            s_scr[h] = jnp.concatenate([st[:, j * LANES:(j + 1) * LANES] * dec for j in range(dv // LANES)],
                                       axis=1) + upd
        return carry

    lax.fori_loop(0, nch, chunk, 0)
    sout_ref[0] = s_scr[...]


def _la_chunk_call(q, k, v, g, gate, norm_w, s0, bsz, t, cl, hb, name):
    m = q.shape[0]
    _, nh, dk, dv = s0.shape
    kern = functools.partial(_la_chunk_kernel, hb=hb, dk=dk, dv=dv, cl=cl, nch=t // cl)
    kspec = pl.BlockSpec((t, hb * dk), lambda bi, hi: (bi, hi))
    vspec = pl.BlockSpec((t, hb * dv), lambda bi, hi: (bi, hi))
    sspec = pl.BlockSpec((1, hb, dk, dv), lambda bi, hi: (bi, hi, 0, 0))
    return pl.pallas_call(
        kern,
        grid=(bsz, nh // hb),
        in_specs=[kspec, kspec, vspec, kspec, vspec, pl.BlockSpec((1, dv), lambda bi, hi: (0, 0)), sspec],
        out_specs=[vspec, sspec],
        out_shape=[jax.ShapeDtypeStruct((m, nh * dv), BF16), jax.ShapeDtypeStruct(s0.shape, F32)],
        scratch_shapes=[pltpu.VMEM((hb, dk, dv), F32)],
        compiler_params=_cparams(("parallel", "parallel")),
        name=name,
    )(q, k, v, g, gate, _row(norm_w), s0)


def _la_run(q, k, v, g, gate, norm_w, s0, bsz, t, hb, name):
    if t == 1:
        seqs = [_pad_steps(x, bsz) for x in (q, k, v, g, gate)]
        og, s_new = _la_chunk_call(*seqs, norm_w, s0, bsz, STEP_PAD, STEP_PAD, hb, name)
        return _first_steps(og, bsz), s_new
    return _la_chunk_call(q, k, v, g, gate, norm_w, s0, bsz, t, LA_CHUNK, hb, name)


def _norm_prologue(h_ref, nw_ref, xn_s):
    @pl.when(pl.program_id(1) == 0)
    def _():
        xn_s[...] = _rms(h_ref[...], nw_ref[...]).astype(BF16)


def _gla_qkg_body(row, full, col, rc, out, scr, *, scale):
    nw_ref, gk1_ref = full
    wq, wk, gk2, bgk = col
    xn_s, t1_s = scr

    @pl.when(pl.program_id(1) == 0)
    def _():
        xn = _rms(row[0][...], nw_ref[...]).astype(BF16)
        xn_s[...] = xn
        t1_s[...] = jnp.dot(xn, gk1_ref[...], preferred_element_type=F32).astype(BF16)

    xn = xn_s[...]
    out[0][...] = jnp.dot(xn, wq[...], preferred_element_type=F32) * scale
    out[1][...] = jnp.dot(xn, wk[...], preferred_element_type=F32)
    lg = jnp.dot(t1_s[...], gk2[...], preferred_element_type=F32) + bgk[...]
    out[2][...] = jax.nn.log_sigmoid(lg) / GLA_GATE_TEMP


def _vg_body(row, full, col, rc, out, scr):
    _norm_prologue(row[0], full[0], scr[0])
    xn = scr[0][...]
    out[0][...] = jnp.dot(xn, col[0][...], preferred_element_type=F32)
    out[1][...] = jax.nn.silu(jnp.dot(xn, col[1][...], preferred_element_type=F32))


def _gla_layer(h, bsz, t, s0, norm_w, wts):
    m, dm = h.shape
    tm = _tiles(m, t)
    nh = GLA_HEADS
    dk = wts['w_q'].shape[1] // nh
    rank = wts['gk1'].shape[1]
    q, k, lg = _proj_call(functools.partial(_gla_qkg_body, scale=dk ** -0.5), m=m, n=nh * dk, tm=tm, tn=512,
                          row_ins=[h], full_ins=[_row(norm_w), wts['gk1']],
                          col_ins=[wts['w_q'], wts['w_k'], wts['gk2'], _row(wts['b_gk'])], rc_ins=[],
                          out_dtypes=[F32] * 3,
                          scratch=[pltpu.VMEM((tm, dm), BF16), pltpu.VMEM((tm, rank), BF16)], name="gla_qkg")
    v, gate = _proj_call(_vg_body, m=m, n=wts['w_v'].shape[1], tm=tm, tn=512, row_ins=[h],
                         full_ins=[_row(norm_w)], col_ins=[wts['w_v'], wts['w_g']], rc_ins=[],
                         out_dtypes=[F32] * 2, scratch=[pltpu.VMEM((tm, dm), BF16)], name="gla_vg")
    og, s_new = _la_run(q, k, v, lg, gate, wts['norm'], s0, bsz, t, 1, "gla_chunk")
    return _outproj_call(og, wts['w_o'], h, t, "gla_out"), s_new


def _hgrn_proj_body(row, full, col, rc, out, scr, *, scale, layer_idx):
    wq, wf, wi, wg, lbp = col
    _norm_prologue(row[0], full[0], scr[0])
    xn = scr[0][...]
    lbx = lbp[...]
    ex = jnp.exp(lbx - jnp.max(lbx, axis=0, keepdims=True))
    lb = jnp.sum(ex[1:layer_idx + 1], axis=0, keepdims=True) / jnp.sum(ex, axis=0, keepdims=True)
    z = jnp.dot(xn, wf[...], preferred_element_type=F32)
    l1m = jnp.log1p(-lb)
    out[0][...] = jnp.dot(xn, wq[...], preferred_element_type=F32) * scale
    out[1][...] = jnp.exp(l1m + jax.nn.log_sigmoid(-z))
    out[2][...] = jnp.logaddexp(jnp.log(lb), l1m + jax.nn.log_sigmoid(z))
    out[3][...] = jnp.dot(xn, wi[...], preferred_element_type=F32)
    out[4][...] = jax.nn.silu(jnp.dot(xn, wg[...], preferred_element_type=F32))


def _hgrn_layer(h, bsz, t, s0, norm_w, wts, layer_idx):
    m, dm = h.shape
    tm = _tiles(m, t)
    body = functools.partial(_hgrn_proj_body, scale=HG_EXPAND ** -0.5, layer_idx=layer_idx)
    q, k, lf, v, gate = _proj_call(body, m=m, n=dm, tm=tm, tn=512, row_ins=[h], full_ins=[_row(norm_w)],
                                   col_ins=[wts['w_q'], wts['w_f'], wts['w_i'], wts['w_g'], wts['lb']],
                                   rc_ins=[], out_dtypes=[F32] * 5, scratch=[pltpu.VMEM((tm, dm), BF16)],
                                   name="hgrn_proj")
    og, s_new = _la_run(q, k, v, lf, gate, wts['norm'], s0, bsz, t, 2, "hgrn_chunk")
    return _outproj_call(og, wts['w_o'], h, t, "hgrn_out"), s_new


def _ffn_kernel(*refs, step, tm, tf, nf, seq_tiles, final_norm):
    h_ref, nw_ref, wup_ref, wgate_ref, wc_ref, bc_ref, wdown_ref = refs[:7]
    p = 7
    if step:
        p0_ref, p1_ref = refs[p:p + 2]; p += 2
    if final_norm:
        fnw_ref = refs[p]; p += 1
    out_ref, tail_ref = refs[p:p + 2]
    xn_s, carry_s = refs[p + 2:]
    i = pl.program_id(0)
    f = pl.program_id(1)

    @pl.when(f == 0)
    def _():
        hv = h_ref[...]
        xn_s[...] = _rms(hv, nw_ref[...]).astype(BF16)
        out_ref[...] = hv

    xn = xn_s[...]
    u = jnp.dot(xn, wup_ref[...], preferred_element_type=F32)
    gate = jnp.dot(xn, wgate_ref[...], preferred_element_type=F32)
    wc = wc_ref[...]
    if step:
        u2, u1 = p0_ref[...], p1_ref[...]
        tail_ref[...] = u
    else:
        @pl.when(i % seq_tiles == 0)
        def _():
            carry_s[f] = jnp.zeros((SUBLANES, tf), F32)

        prev = carry_s[f]
        rid = _iota((tm, 1), 0)
        u1 = jnp.where(rid == 0, prev[SUBLANES - 1:SUBLANES], pltpu.roll(u, 1, 0))
        u2 = jnp.where(rid == 0, prev[SUBLANES - 2:SUBLANES - 1],
                       jnp.where(rid == 1, prev[SUBLANES - 1:SUBLANES], pltpu.roll(u, 2, 0)))
        tail = u[tm - SUBLANES:tm]
        carry_s[f] = tail
        tail_ref[0] = tail
    c = u2 * wc[0:1] + u1 * wc[1:2] + u * wc[2:3] + bc_ref[...]
    act = (jax.nn.gelu(c) * gate).astype(BF16)
    out_ref[...] += jnp.dot(act, wdown_ref[...], preferred_element_type=F32)
    if final_norm:
        @pl.when(f == nf - 1)
        def _():
            out_ref[...] = _rms(out_ref[...], fnw_ref[...])


def _ffn_call(h, bsz, t, conv_prev, norm_w, w_up, w_gate, w_conv, b_conv, w_down, final_norm_w):
    m, dm = h.shape
    dff = w_up.shape[1]
    step = t == 1
    tm = _tiles(m, t)
    tf = 512
    nf = dff // tf
    seq_tiles = 1 if step else t // tm
    final_norm = final_norm_w is not None
    args = [h, _row(norm_w), w_up, w_gate, w_conv.astype(F32), _row(b_conv.astype(F32)), w_down]
    specs = [pl.BlockSpec((tm, dm), lambda i, f: (i, 0)), pl.BlockSpec((1, dm), lambda i, f: (0, 0)),
             pl.BlockSpec((dm, tf), lambda i, f: (0, f)), pl.BlockSpec((dm, tf), lambda i, f: (0, f)),
             pl.BlockSpec((CONV_W, tf), lambda i, f: (0, f)), pl.BlockSpec((1, tf), lambda i, f: (0, f)),
             pl.BlockSpec((tf, dm), lambda i, f: (f, 0))]
    if step:
        args += [conv_prev[:, 0].astype(F32), conv_prev[:, 1].astype(F32)]
        specs += [pl.BlockSpec((tm, tf), lambda i, f: (i, f))] * 2
        tail_shape = jax.ShapeDtypeStruct((m, dff), F32)
        tail_spec = pl.BlockSpec((tm, tf), lambda i, f: (i, f))
    else:
        tail_shape = jax.ShapeDtypeStruct((bsz, SUBLANES, dff), F32)
        tail_spec = pl.BlockSpec((1, SUBLANES, tf), lambda i, f: (i // seq_tiles, 0, f))
    if final_norm:
        args.append(_row(final_norm_w))
        specs.append(pl.BlockSpec((1, dm), lambda i, f: (0, 0)))
    kern = functools.partial(_ffn_kernel, step=step, tm=tm, tf=tf, nf=nf, seq_tiles=seq_tiles,
                             final_norm=final_norm)
    out, tail = pl.pallas_call(
        kern,
        grid=(m // tm, nf),
        in_specs=specs,
        out_specs=[pl.BlockSpec((tm, dm), lambda i, f: (i, 0)), tail_spec],
        out_shape=[jax.ShapeDtypeStruct((m, dm), F32), tail_shape],
        scratch_shapes=[pltpu.VMEM((tm, dm), BF16), pltpu.VMEM((nf, SUBLANES, tf), F32)],
        compiler_params=_cparams(("arbitrary", "arbitrary")),
        name="conv_ffn",
    )(*args)
    if step:
        conv_new = jnp.stack([conv_prev[:, 1].astype(F32), tail], axis=1)
    else:
        conv_new = tail[:, SUBLANES - (CONV_W - 1):]
    return out, conv_new


def _trunk(x, s5_re, s5_im, rw_shift, rw_wkv, gla_s, hg_s, ffn_conv, p):
    bsz, t, dm = x.shape
    h = x.reshape(bsz * t, dm).astype(F32)
    depth = p['norm_mix'].shape[0]
    conv_new = []
    for i in range(depth):
        kind = i % 4
        nw = p['norm_mix'][i]
        if kind == 0:
            h, s5_re, s5_im = _s5_layer(h, bsz, t, s5_re, s5_im, nw, p['s5'], p['s5_c_re'], p['s5_c_im'],
                                        p['s5_d'], p['s5_w_glu'])
        elif kind == 1:
            h, rw_shift, rw_wkv = _rwkv_layer(h, bsz, t, rw_shift, rw_wkv, nw, p['rw'])
        elif kind == 2:
            h, gla_s = _gla_layer(h, bsz, t, gla_s, nw, p['gla'])
        else:
            h, hg_s = _hgrn_layer(h, bsz, t, hg_s, nw, p['hg'], i)
        fin = p['norm_final'] if i == depth - 1 else None
        h, c = _ffn_call(h, bsz, t, ffn_conv[i], p['norm_ffn'][i], p['ffn_w_up'][i], p['ffn_w_gate'][i],
                         p['ffn_w_conv'][i], p['ffn_b_conv'][i], p['ffn_w_down'][i], fin)
        conv_new.append(c)
    return h.reshape(bsz, t, dm), s5_re, s5_im, rw_shift, rw_wkv, gla_s, hg_s, jnp.stack(conv_new)


def _pad_cols(w, n):
    return jnp.pad(w, ((0, 0), (0, n - w.shape[1])))


def _pad_rows(w, n):
    return jnp.pad(w, ((0, n - w.shape[0]), (0, 0)))


def kernel(x_prompt, x_sample, state_s5_re, state_s5_im, state_rwkv_shift, state_rwkv_wkv, state_gla, state_hgrn, state_ffn_conv, norm_mix, norm_ffn, norm_final, s5_lambda_re, s5_lambda_im, s5_log_dt, s5_b_re, s5_b_im, s5_c_re, s5_c_im, s5_d, s5_w_glu, rw_mix, rw_w_r, rw_w_k, rw_w_v, rw_w_o, rw_w0, rw_w1, rw_w2, rw_a0, rw_a1, rw_a2, rw_g1, rw_g2, rw_k_k, rw_k_a, rw_r_k, rw_ln_w, rw_ln_b, gla_w_q, gla_w_k, gla_w_v, gla_w_gk1, gla_w_gk2, gla_b_gk, gla_w_g, gla_norm, gla_w_o, hg_w_q, hg_w_f, hg_w_i, hg_w_g, hg_lb, hg_norm, hg_w_o, ffn_w_up, ffn_w_gate, ffn_w_conv, ffn_b_conv, ffn_w_down):
    bf = lambda w: w.astype(BF16)
    f32 = lambda w: w.astype(F32)
    lora = LANES * pl.cdiv(rw_w1.shape[1], LANES)
    alora = LANES * pl.cdiv(rw_a1.shape[1], LANES)
    grank = LANES * pl.cdiv(gla_w_gk1.shape[1], LANES)
    p = {
        'norm_mix': f32(norm_mix), 'norm_ffn': f32(norm_ffn), 'norm_final': f32(norm_final),
        's5': _s5_params(s5_lambda_re, s5_lambda_im, s5_log_dt, s5_b_re, s5_b_im, s5_c_re, s5_c_im, s5_d),
        's5_c_re': s5_c_re, 's5_c_im': s5_c_im, 's5_d': s5_d, 's5_w_glu': bf(s5_w_glu),
        'rw': dict(mix=_pad_rows(f32(rw_mix), SUBLANES), w_r=bf(rw_w_r), w_k=bf(rw_w_k), w_v=bf(rw_w_v),
                   w_o=bf(rw_w_o), w0=f32(rw_w0), w1=bf(_pad_cols(rw_w1, lora)), w2=bf(_pad_rows(rw_w2, lora)),
                   a0=f32(rw_a0), a1=bf(_pad_cols(rw_a1, alora)), a2=bf(_pad_rows(rw_a2, alora)),
                   g1=bf(rw_g1), g2=bf(rw_g2), k_k=f32(rw_k_k), k_a=f32(rw_k_a), r_k=f32(rw_r_k).reshape(-1),
                   ln_w=f32(rw_ln_w), ln_b=f32(rw_ln_b)),
        'gla': dict(w_q=bf(gla_w_q), w_k=bf(gla_w_k), w_v=bf(gla_w_v), gk1=bf(_pad_cols(gla_w_gk1, grank)),
                    gk2=bf(_pad_rows(gla_w_gk2, grank)), b_gk=f32(gla_b_gk), w_g=bf(gla_w_g), norm=f32(gla_norm),
                    w_o=bf(gla_w_o)),
        'hg': dict(w_q=bf(hg_w_q), w_f=bf(hg_w_f), w_i=bf(hg_w_i), w_g=bf(hg_w_g), lb=f32(hg_lb),
                   norm=f32(hg_norm), w_o=bf(hg_w_o)),
        'ffn_w_up': bf(ffn_w_up), 'ffn_w_gate': bf(ffn_w_gate), 'ffn_w_conv': ffn_w_conv,
        'ffn_b_conv': ffn_b_conv, 'ffn_w_down': bf(ffn_w_down),
    }
    nb, _, dm = x_prompt.shape
    depth = norm_mix.shape[0]
    dff = ffn_w_up.shape[-1]
    z_s5 = jnp.zeros((nb,) + state_s5_re.shape[1:], F32)
    outs_p = _trunk(x_prompt, z_s5, z_s5, jnp.zeros((nb, dm), F32),
                    jnp.zeros((nb,) + state_rwkv_wkv.shape[1:], F32), jnp.zeros((nb,) + state_gla.shape[1:], F32),
                    jnp.zeros((nb,) + state_hgrn.shape[1:], F32), jnp.zeros((depth, nb, CONV_W - 1, dff), F32), p)
    outs_s = _trunk(x_sample, f32(state_s5_re), f32(state_s5_im), state_rwkv_shift, f32(state_rwkv_wkv),
                    f32(state_gla), f32(state_hgrn), state_ffn_conv, p)
    return (outs_p[0], outs_s[0]) + tuple(outs_p[1:]) + tuple(outs_s[1:])
```

```python
import functools
import math

import jax
import jax.numpy as jnp
from jax import lax
from jax.experimental import pallas as pl
from jax.experimental.pallas import tpu as pltpu

F32 = jnp.float32
BF16 = jnp.bfloat16

NORM_EPS = 1e-6
RW_LN_EPS = 64e-5
RW_HEAD = 64
S5_GROUP = 16
S5_STATE = 64
S5_CHUNK = 16
GLA_HEADS = 4
GLA_GATE_TEMP = 16.0
HG_EXPAND = 128
CONV_W = 3
LA_CHUNK = 64
RW_CHUNK = 64
STEP_PAD = 16
LANES = 128
SUBLANES = 8
VMEM_LIMIT = 56 * 1024 * 1024


def _cparams(sem):
    return pltpu.CompilerParams(dimension_semantics=sem, vmem_limit_bytes=VMEM_LIMIT)


def _bdot(a, b):
    return jnp.dot(a.astype(BF16), b.astype(BF16), preferred_element_type=F32)


def _bdot_nt(a, b):
    return lax.dot_general(a.astype(BF16), b.astype(BF16), (((1,), (1,)), ((), ())),
                           preferred_element_type=F32)


def _bdot_tn(a, b):
    return lax.dot_general(a.astype(BF16), b.astype(BF16), (((0,), (0,)), ((), ())),
                           preferred_element_type=F32)


def _split3(x):
    hi = x.astype(BF16)
    r = x - hi.astype(F32)
    mid = r.astype(BF16)
    lo = (r - mid.astype(F32)).astype(BF16)
    return hi, mid, lo


def _xdot_l(m, x):
    hi, mid, lo = _split3(x)
    d = lambda p: jnp.dot(m, p, preferred_element_type=F32)
    return d(hi) + d(mid) + d(lo)


def _xdot_r(x, m):
    hi, mid, lo = _split3(x)
    d = lambda p: jnp.dot(p, m, preferred_element_type=F32)
    return d(hi) + d(mid) + d(lo)


def _rms(x, g):
    y = x * lax.rsqrt(jnp.mean(x * x, axis=-1, keepdims=True) + NORM_EPS)
    return y * g


def _iota(shape, dim):
    return lax.broadcasted_iota(jnp.int32, shape, dim)


def _rmsnorm_kernel(x_ref, g_ref, o_ref):
    o_ref[...] = _rms(x_ref[...], g_ref[...])


def _rmsnorm_call(x, g):
    r, d = x.shape
    tr = 512 if r % 512 == 0 else r
    return pl.pallas_call(
        _rmsnorm_kernel,
        grid=(r // tr,),
        in_specs=[pl.BlockSpec((tr, d), lambda i: (i, 0)), pl.BlockSpec((1, d), lambda i: (0, 0))],
        out_specs=pl.BlockSpec((tr, d), lambda i: (i, 0)),
        out_shape=jax.ShapeDtypeStruct((r, d), F32),
        compiler_params=_cparams(("parallel",)),
        name="rmsnorm",
    )(x, g.reshape(1, d))


def _proj_call(body, *, m, n, tm, tn, row_ins, full_ins, col_ins, rc_ins, out_dtypes, scratch, name):
    args, specs = [], []
    for a in row_ins:
        if isinstance(a, tuple):
            args.append(a[0]); specs.append(a[1])
        else:
            args.append(a); specs.append(pl.BlockSpec((tm, a.shape[1]), lambda i, j: (i, 0)))
    for a in full_ins:
        args.append(a); specs.append(pl.BlockSpec(a.shape, lambda i, j, nd=a.ndim: (0,) * nd))
    for a in col_ins:
        args.append(a); specs.append(pl.BlockSpec((a.shape[0], tn), lambda i, j: (0, j)))
    for a in rc_ins:
        args.append(a); specs.append(pl.BlockSpec((tm, tn), lambda i, j: (i, j)))
    nr, nf, nc, nrc, no = len(row_ins), len(full_ins), len(col_ins), len(rc_ins), len(out_dtypes)

    def kernel(*refs):
        p = 0
        groups = []
        for cnt in (nr, nf, nc, nrc, no):
            groups.append(refs[p:p + cnt]); p += cnt
        body(*groups, refs[p:])

    outs = pl.pallas_call(
        kernel,
        grid=(m // tm, n // tn),
        in_specs=specs,
        out_specs=[pl.BlockSpec((tm, tn), lambda i, j: (i, j)) for _ in out_dtypes],
        out_shape=[jax.ShapeDtypeStruct((m, n), dt) for dt in out_dtypes],
        scratch_shapes=scratch,
        compiler_params=_cparams(("parallel", "arbitrary")),
        name=name,
    )(*args)
    return outs


def _row(v):
    return v.reshape(1, -1)


def _tiles(m, t):
    if t == 1:
        return 512 if m % 512 == 0 else m
    return min(512, t)


def _s5_params(lam_re, lam_im, log_dt, b_re, b_im, c_re, c_im, d):
    hp = lax.Precision.HIGHEST
    g, n = lam_re.shape
    p = b_re.shape[-1]
    cl = S5_CHUNK
    lr = jnp.minimum(lam_re.astype(F32), -1e-4)
    li = lam_im.astype(F32)
    dt = jnp.exp(log_dt.astype(F32))[:, None]
    mag = jnp.exp(lr * dt)
    ab_re = mag * jnp.cos(li * dt)
    ab_im = mag * jnp.sin(li * dt)
    den = lr * lr + li * li
    f_re = ((ab_re - 1.0) * lr + ab_im * li) / den
    f_im = (ab_im * lr - (ab_re - 1.0) * li) / den
    br, bi = b_re.astype(F32), b_im.astype(F32)
    bb_re = f_re[..., None] * br - f_im[..., None] * bi
    bb_im = f_re[..., None] * bi + f_im[..., None] * br
    cr, ci = c_re.astype(F32), c_im.astype(F32)

    pr, pi = [jnp.ones_like(ab_re)], [jnp.zeros_like(ab_im)]
    for _ in range(cl):
        pr.append(pr[-1] * ab_re - pi[-1] * ab_im)
        pi.append(pr[-2] * ab_im + pi[-1] * ab_re)
    pw_re = jnp.stack(pr)
    pw_im = jnp.stack(pi)
    ajb_re = pw_re[:cl, :, :, None] * bb_re - pw_im[:cl, :, :, None] * bb_im
    ajb_im = pw_re[:cl, :, :, None] * bb_im + pw_im[:cl, :, :, None] * bb_re
    kj = (jnp.einsum('gpn,jgnq->jgpq', cr, ajb_re, precision=hp)
          - jnp.einsum('gpn,jgnq->jgpq', ci, ajb_im, precision=hp))
    s_idx = jnp.arange(cl)[:, None]
    t_idx = jnp.arange(cl)[None, :]
    lag = t_idx - s_idx
    kst = kj[jnp.clip(lag, 0, cl - 1)]
    kst = jnp.where((lag >= 0)[:, :, None, None, None], kst, 0.0)
    mt = kst.transpose(2, 0, 4, 1, 3).reshape(g, cl * p, cl * p)
    w_re = ajb_re[::-1].transpose(1, 0, 3, 2).reshape(g, cl * p, n)
    w_im = ajb_im[::-1].transpose(1, 0, 3, 2).reshape(g, cl * p, n)
    wm = jnp.concatenate([w_re, w_im], axis=-1)
    ca_re = cr[None] * pw_re[1:, :, None, :] - ci[None] * pw_im[1:, :, None, :]
    ca_im = cr[None] * pw_im[1:, :, None, :] + ci[None] * pw_re[1:, :, None, :]
    v_re = ca_re.transpose(1, 3, 0, 2).reshape(g, n, cl * p)
    v_im = -ca_im.transpose(1, 3, 0, 2).reshape(g, n, cl * p)
    vm = jnp.concatenate([v_re, v_im], axis=1)
    ar, ai = pw_re[cl], pw_im[cl]
    pcs, qcs = [], []
    for _ in range(SUBLANES):
        pcs.append(jnp.concatenate([ar, ar], axis=-1))
        qcs.append(jnp.concatenate([-ai, ai], axis=-1))
        ar, ai = ar * ar - ai * ai, 2.0 * ar * ai
    pc = jnp.stack(pcs, axis=1)
    qc = jnp.stack(qcs, axis=1)
    dtile = jnp.tile(d.astype(F32).reshape(g, 1, p), (1, 1, cl))
    return dict(ab_re=ab_re, ab_im=ab_im, bb_re=bb_re, bb_im=bb_im, mt=mt.astype(BF16), wm=wm.astype(BF16),
                vm=vm.astype(BF16), pc=pc, qc=qc, dtile=dtile)


def _s5_seq_kernel(u_ref, mt_ref, w_ref, vm_ref, pc_ref, qc_ref, dt_ref, h0_ref, z_ref, hf_ref, *, gb, nc):
    half = S5_STATE
    rid = _iota((nc, 1), 0)
    nlev = int(math.log2(nc))
    for g in range(gb):
        u = u_ref[0, g]
        ub = u.astype(BF16)
        y_intra = jnp.dot(ub, mt_ref[g], preferred_element_type=F32)
        x = jnp.dot(ub, w_ref[g], preferred_element_type=F32)
        pc = pc_ref[g]
        qc = qc_ref[g]
        h0 = h0_ref[0, g]
        h0c = pc[0:1] * h0 + qc[0:1] * pltpu.roll(h0, half, 1)
        x = x + jnp.where(rid == 0, h0c, 0.0)
        for lv in range(nlev):
            sh = 1 << lv
            xs = jnp.where(rid >= sh, pltpu.roll(x, sh, 0), 0.0)
            x = x + pc[lv:lv + 1] * xs + qc[lv:lv + 1] * pltpu.roll(xs, half, 1)
        hprev = jnp.where(rid == 0, h0, pltpu.roll(x, 1, 0))
        y = y_intra + jnp.dot(hprev.astype(BF16), vm_ref[g], preferred_element_type=F32) + dt_ref[g] * u
        z_ref[0, g] = jax.nn.gelu(y)
        hf_ref[0, g] = x[nc - 1:nc, :]


def _s5_seq_call(u4, prm, h0):
    b, g, nc, w = u4.shape
    n2 = 2 * S5_STATE
    gb = 8
    kern = functools.partial(_s5_seq_kernel, gb=gb, nc=nc)
    gspec = lambda shape: pl.BlockSpec((gb,) + shape, lambda gi, bi: (gi, 0, 0))
    return pl.pallas_call(
        kern,
        grid=(g // gb, b),
        in_specs=[pl.BlockSpec((1, gb, nc, w), lambda gi, bi: (bi, gi, 0, 0)),
                  gspec((w, w)), gspec((w, n2)), gspec((n2, w)), gspec((SUBLANES, n2)), gspec((SUBLANES, n2)),
                  gspec((1, w)),
                  pl.BlockSpec((1, gb, 1, n2), lambda gi, bi: (bi, gi, 0, 0))],
        out_specs=[pl.BlockSpec((1, gb, nc, w), lambda gi, bi: (bi, gi, 0, 0)),
                   pl.BlockSpec((1, gb, 1, n2), lambda gi, bi: (bi, gi, 0, 0))],
        out_shape=[jax.ShapeDtypeStruct((b, g, nc, w), F32), jax.ShapeDtypeStruct((b, g, 1, n2), F32)],
        compiler_params=_cparams(("parallel", "arbitrary")),
        name="s5_seq",
    )(u4, prm['mt'], prm['wm'], prm['vm'], prm['pc'], prm['qc'], prm['dtile'], h0)


def _s5_step_kernel(u_ref, hr_ref, hi_ref, ar_ref, ai_ref, bdr_ref, bdi_ref, cdr_ref, cdi_ref, d_ref,
                    z_ref, hro_ref, hio_ref):
    u = u_ref[...]
    ar, ai = ar_ref[...], ai_ref[...]
    hr, hi = hr_ref[...], hi_ref[...]
    hr2 = ar * hr - ai * hi + _bdot(u, bdr_ref[0])
    hi2 = ar * hi + ai * hr + _bdot(u, bdi_ref[0])
    y = _bdot(hr2, cdr_ref[0]) - _bdot(hi2, cdi_ref[0]) + d_ref[...] * u
    z_ref[...] = jax.nn.gelu(y)
    hro_ref[...] = hr2
    hio_ref[...] = hi2


def _s5_step_call(u, h_re, h_im, prm, c_re, c_im, d):
    b, dm = u.shape
    g, n = prm['ab_re'].shape
    p = S5_GROUP
    gpb = LANES // p
    nb = g // gpb
    eye = jnp.eye(gpb, dtype=F32)
    bd = lambda bb: jnp.einsum('kgnp,gh->kgphn', bb.reshape(nb, gpb, n, p), eye).reshape(nb, gpb * p, gpb * n)
    cd = lambda cc: jnp.einsum('kgpn,gh->kgnhp', cc.astype(F32).reshape(nb, gpb, p, n), eye).reshape(
        nb, gpb * n, gpb * p)
    bdr, bdi = bd(prm['bb_re']).astype(BF16), bd(prm['bb_im']).astype(BF16)
    cdr, cdi = cd(c_re).astype(BF16), cd(c_im).astype(BF16)
    wn = gpb * n
    cspec = lambda wd: pl.BlockSpec((b, wd), lambda k: (0, k))
    rspec = lambda wd: pl.BlockSpec((1, wd), lambda k: (0, k))
    z, hr2, hi2 = pl.pallas_call(
        _s5_step_kernel,
        grid=(nb,),
        in_specs=[cspec(LANES), cspec(wn), cspec(wn), rspec(wn), rspec(wn),
                  pl.BlockSpec((1, LANES, wn), lambda k: (k, 0, 0)), pl.BlockSpec((1, LANES, wn), lambda k: (k, 0, 0)),
                  pl.BlockSpec((1, wn, LANES), lambda k: (k, 0, 0)), pl.BlockSpec((1, wn, LANES), lambda k: (k, 0, 0)),
                  rspec(LANES)],
        out_specs=[cspec(LANES), cspec(wn), cspec(wn)],
        out_shape=[jax.ShapeDtypeStruct((b, dm), F32), jax.ShapeDtypeStruct((b, g * n), F32),
                   jax.ShapeDtypeStruct((b, g * n), F32)],
        compiler_params=_cparams(("parallel",)),
        name="s5_step",
    )(u, h_re.reshape(b, g * n), h_im.reshape(b, g * n), prm['ab_re'].reshape(1, g * n),
      prm['ab_im'].reshape(1, g * n), bdr, bdi, cdr, cdi, _row(d.astype(F32)))
    return z, hr2.reshape(b, g, n), hi2.reshape(b, g, n)


def _glu_body(row, full, col, rc, out, scr):
    (z_ref,) = row
    (w_ref,) = col
    zt_ref, h_ref = rc
    (zb,) = scr

    @pl.when(pl.program_id(1) == 0)
    def _():
        zb[...] = z_ref[...].astype(BF16)

    zt = zt_ref[...]
    gate = jax.nn.sigmoid(jnp.dot(zb[...], w_ref[...], preferred_element_type=F32))
    out[0][...] = h_ref[...] + zt * gate


def _s5_layer(h, bsz, t, st_re, st_im, norm_w, prm, c_re, c_im, d, w_glu):
    m, dm = h.shape
    g, n, p = dm // S5_GROUP, S5_STATE, S5_GROUP
    xn = _rmsnorm_call(h, norm_w)
    if t == 1:
        z, s_re, s_im = _s5_step_call(xn, st_re, st_im, prm, c_re, c_im, d)
    else:
        cl = S5_CHUNK
        nc = t // cl
        u4 = xn.reshape(bsz, nc, cl, g, p).transpose(0, 3, 1, 2, 4).reshape(bsz, g, nc, cl * p)
        h0 = jnp.concatenate([st_re, st_im], axis=-1).reshape(bsz, g, 1, 2 * n)
        z4, hf = _s5_seq_call(u4, prm, h0)
        z = z4.reshape(bsz, g, nc, cl, p).transpose(0, 2, 3, 1, 4).reshape(m, dm)
        s_re, s_im = hf[:, :, 0, :n], hf[:, :, 0, n:]
    tm = _tiles(m, t)
    (h_new,) = _proj_call(_glu_body, m=m, n=dm, tm=tm, tn=512, row_ins=[z], full_ins=[], col_ins=[w_glu],
                          rc_ins=[z, h], out_dtypes=[F32], scratch=[pltpu.VMEM((tm, dm), BF16)], name="s5_glu")
    return h_new, s_re, s_im


def _rwkv_proj_body(row, full, col, rc, out, scr, *, tm, seq_tiles, step):
    h_ref, prev_ref = row
    nw_ref, mix_ref, w1_ref, a1_ref, g1_ref, bd_ref = full
    wr, wk, wv, w2, a2, g2, w0, a0, kk_ref, ka_ref = col
    r_o, lw_o, k_o, v_o, a_o, b_o, g_o = out
    xr_s, xk_s, xv_s, tw_s, ta_s, tg_s = scr
    i = pl.program_id(0)

    @pl.when(pl.program_id(1) == 0)
    def _():
        nw = nw_ref[...]
        xn = _rms(h_ref[...], nw)
        if step:
            xprev = prev_ref[...]
        else:
            last = _rms(prev_ref[...], nw)[SUBLANES - 1:SUBLANES, :]
            last = jnp.where(i % seq_tiles == 0, 0.0, last)
            xprev = jnp.where(_iota((tm, 1), 0) == 0, last, pltpu.roll(xn, 1, 0))
        xx = xprev - xn
        mix = mix_ref[...]
        xr_s[...] = (xn + xx * mix[0:1]).astype(BF16)
        xk_s[...] = (xn + xx * mix[2:3]).astype(BF16)
        xv_s[...] = (xn + xx * mix[3:4]).astype(BF16)
        tw_s[...] = jnp.tanh(_bdot(xn + xx * mix[1:2], w1_ref[...])).astype(BF16)
        ta_s[...] = _bdot(xn + xx * mix[4:5], a1_ref[...]).astype(BF16)
        tg_s[...] = jax.nn.sigmoid(_bdot(xn + xx * mix[5:6], g1_ref[...])).astype(BF16)

    dot = lambda a, b: jnp.dot(a[...], b[...], preferred_element_type=F32)
    r = dot(xr_s, wr)
    k = dot(xk_s, wk)
    v = dot(xv_s, wv)
    wraw = -jax.nn.softplus(-(w0[...] + dot(tw_s, w2))) - 0.5
    a = jax.nn.sigmoid(a0[...] + dot(ta_s, a2))
    kk = k * kk_ref[...]
    ss = _xdot_r(kk * kk, bd_ref[...])
    kk = kk / jnp.maximum(jnp.sqrt(ss), 1e-12)
    r_o[...] = r
    lw_o[...] = -jnp.exp(wraw)
    k_o[...] = k * (1.0 + (a - 1.0) * ka_ref[...])
    v_o[...] = v
    a_o[...] = -kk
    b_o[...] = kk * a
    g_o[...] = dot(tg_s, g2)


def _rwkv_chunk_kernel(r_ref, lw_ref, k_ref, v_ref, a_ref, b_ref, gate_ref, lnw_ref, lnb_ref, rk_ref, s0_ref,
                       o_ref, sout_ref, p_scr, rp_scr, o0_scr, m_scr, d_scr, g_scr, *, npair, cl, nch):
    hd = RW_HEAD
    pw = 2 * hd
    lane = _iota((1, pw), 1)
    head0 = lane < hd
    bdmask = (_iota((pw, pw), 0) // hd) == (_iota((pw, pw), 1) // hd)
    e_dup = ((_iota((hd, pw), 1) % hd) == _iota((hd, pw), 0)).astype(BF16)
    e_fold = ((_iota((pw, hd), 0) % hd) == _iota((pw, hd), 1)).astype(BF16)
    ones_bd = bdmask.astype(BF16)
    avg = (bdmask.astype(F32) * (1.0 / hd)).astype(BF16)
    for p in range(npair):
        p_scr[p] = jnp.where(bdmask, _xdot_r(s0_ref[0, p], e_dup), 0.0)
    s2 = 2 * cl
    rr = _iota((s2, s2), 0)
    cc = _iota((s2, s2), 1)
    same = (rr // cl) == (cc // cl)
    strict = same & ((rr % cl) > (cc % cl))
    incl = same & ((rr % cl) >= (cc % cl))
    eye_s = (rr == cc).astype(F32)
    eye_p = (_iota((pw, pw), 0) == _iota((pw, pw), 1)).astype(F32)
    rid = _iota((cl, 1), 0)
    zeros_s = jnp.zeros((s2, pw), F32)
    nround = int(math.log2(cl)) - 1

    def stack(x):
        return jnp.concatenate([jnp.where(head0, x, 0.0), jnp.where(head0, 0.0, x)], axis=0)

    def phase_a(c, carry):
        r0 = pl.multiple_of(c * cl, cl)
        for p in range(npair):
            sl = slice(p * pw, (p + 1) * pw)
            r = r_ref[pl.ds(r0, cl), sl]
            lw = lw_ref[pl.ds(r0, cl), sl]
            k = k_ref[pl.ds(r0, cl), sl]
            v = v_ref[pl.ds(r0, cl), sl]
            a = a_ref[pl.ds(r0, cl), sl]
            b = b_ref[pl.ds(r0, cl), sl]
            cs = lw
            sh = 1
            while sh < cl:
                cs = cs + jnp.where(rid >= sh, pltpu.roll(cs, sh, 0), 0.0)
                sh *= 2
            gam = jnp.exp(cs)
            ginv = jnp.exp(-cs)
            ats = stack(a * jnp.exp(cs - lw))
            rts = stack(r * gam)
            x1s = jnp.concatenate([ats, rts], axis=0).astype(BF16)
            x2s = jnp.concatenate([stack(b * ginv), stack(k * ginv)], axis=0).astype(BF16)
            gs = _bdot_nt(x1s, x2s)
            nmat = jnp.where(strict, gs[:s2, :s2], 0.0)
            aak = jnp.where(strict, gs[:s2, s2:], 0.0)
            arb = jnp.where(incl, gs[s2:, :s2], 0.0)
            ark = jnp.where(incl, gs[s2:, s2:], 0.0)
            tinv = eye_s + nmat
            pk = nmat
            for _ in range(nround):
                pk = _bdot(pk, pk)
                tinv = tinv + _bdot(pk, tinv)
            vs = stack(v)
            tz = _bdot(tinv, jnp.concatenate([_bdot(aak, vs), ats], axis=1))
            u0s, aps = tz[:, :pw], tz[:, pw:]
            z = _bdot(arb, jnp.concatenate([aps, u0s], axis=1))
            rps = rts + z[:, :pw]
            o0s = z[:, pw:] + _bdot(ark, vs)
            rp_scr[c, p] = (rps[:cl] + rps[cl:]).astype(BF16)
            o0_scr[c, p] = o0s[:cl] + o0s[cl:]
            gl = gam[cl - 1:cl, :]
            lhs = jnp.concatenate([jnp.concatenate([aps, u0s], axis=1), jnp.concatenate([zeros_s, vs], axis=1)],
                                  axis=0)
            md = _bdot_tn(lhs, x2s)
            m_scr[c, p] = (md[:pw] * gl).astype(BF16)
            d_scr[c, p] = md[pw:] * gl
            g_scr[c, p] = gl
        return carry

    lax.fori_loop(0, nch, phase_a, 0, unroll=2 if nch % 2 == 0 else 1)

    def phase_b(c, carry):
        r0 = pl.multiple_of(c * cl, cl)
        for p in range(npair):
            sl = slice(p * pw, (p + 1) * pw)
            st = p_scr[p]
            o = _bdot_nt(rp_scr[c, p], st) + o0_scr[c, p]
            p_scr[p] = st * g_scr[c, p] + _bdot(st, m_scr[c, p]) + d_scr[c, p]
            r = r_ref[pl.ds(r0, cl), sl]
            k = k_ref[pl.ds(r0, cl), sl]
            v = v_ref[pl.ds(r0, cl), sl]
            mu = _bdot(o, avg)
            dlt = o - mu
            var = _bdot(dlt * dlt, avg)
            on = dlt * lax.rsqrt(var + RW_LN_EPS) * lnw_ref[:, sl] + lnb_ref[:, sl]
            bonus = _bdot(r * k * rk_ref[:, sl], ones_bd) * v
            o_ref[pl.ds(r0, cl), sl] = ((on + bonus) * gate_ref[pl.ds(r0, cl), sl]).astype(o_ref.dtype)
        return carry

    lax.fori_loop(0, nch, phase_b, 0, unroll=2 if nch % 2 == 0 else 1)
    for p in range(npair):
        sout_ref[0, p] = _xdot_r(jnp.where(bdmask, p_scr[p], 0.0), e_fold)


def _rwkv_chunk_call(r, lw, k, v, a, b, gate, ln_w, ln_b, r_k, s0, bsz, t, cl):
    m, dm = r.shape
    hd = RW_HEAD
    pw = 2 * hd
    npairs = dm // pw
    npair = 2
    nch = t // cl
    s0p = s0.reshape(bsz, npairs, pw, hd)
    kern = functools.partial(_rwkv_chunk_kernel, npair=npair, cl=cl, nch=nch)
    tspec = pl.BlockSpec((t, npair * pw), lambda bi, pi: (bi, pi))
    vspec = pl.BlockSpec((1, npair * pw), lambda bi, pi: (0, pi))
    sspec = pl.BlockSpec((1, npair, pw, hd), lambda bi, pi: (bi, pi, 0, 0))
    o, s = pl.pallas_call(
        kern,
        grid=(bsz, npairs // npair),
        in_specs=[tspec] * 7 + [vspec] * 3 + [sspec],
        out_specs=[tspec, sspec],
        out_shape=[jax.ShapeDtypeStruct((m, dm), BF16), jax.ShapeDtypeStruct((bsz, npairs, pw, hd), F32)],
        scratch_shapes=[pltpu.VMEM((npair, pw, pw), F32),
                        pltpu.VMEM((nch, npair, cl, pw), BF16), pltpu.VMEM((nch, npair, cl, pw), F32),
                        pltpu.VMEM((nch, npair, pw, pw), BF16), pltpu.VMEM((nch, npair, pw, pw), F32),
                        pltpu.VMEM((nch, npair, 1, pw), F32)],
        compiler_params=_cparams(("parallel", "parallel")),
        name="rwkv_chunk",
    )(r, lw, k, v, a, b, gate, _row(ln_w), _row(ln_b), _row(r_k), s0p)
    return o, s.reshape(bsz, dm // hd, hd, hd)


def _rwkv_step_kernel(r_ref, lw_ref, k_ref, v_ref, a_ref, b_ref, gate_ref, lnw_ref, lnb_ref, rk_ref, s0_ref,
                      o_ref, sout_ref, oacc, *, nh):
    first = _iota((STEP_PAD, 1), 0) == 0

    def pad(x):
        return jnp.where(first, x, 0.0)

    def head(h, carry):
        row = lambda ref: ref[0, pl.ds(h, 1), :]
        st = s0_ref[0, h]
        sa = jnp.sum(st * row(a_ref), axis=-1, keepdims=True)
        vh, vm, vl = _split3(pad(row(v_ref)))
        kh, km, kl = _split3(pad(row(k_ref)))
        tn = lambda x, y: lax.dot_general(x, y, (((0,), (0,)), ((), ())), preferred_element_type=F32)
        vk = tn(vh, kh) + (tn(vh, km) + tn(vm, kh)) + (tn(vh, kl) + tn(vm, km) + tn(vl, kh))
        sn = st * jnp.exp(row(lw_ref)) + sa * row(b_ref) + vk
        sout_ref[0, h] = sn
        oacc[pl.ds(h, 1), :] = _bdot_nt(pad(row(r_ref)), sn)[0:1]
        return carry

    lax.fori_loop(0, nh, head, 0, unroll=4)
    o = oacc[...]
    mu = jnp.mean(o, axis=-1, keepdims=True)
    dlt = o - mu
    var = jnp.mean(dlt * dlt, axis=-1, keepdims=True)
    on = dlt * lax.rsqrt(var + RW_LN_EPS) * lnw_ref[...] + lnb_ref[...]
    bonus = jnp.sum(r_ref[0] * k_ref[0] * rk_ref[...], axis=-1, keepdims=True) * v_ref[0]
    o_ref[0] = ((on + bonus) * gate_ref[0]).astype(o_ref.dtype)


def _rwkv_step_call(r, lw, k, v, a, b, gate, ln_w, ln_b, r_k, s0):
    bsz, dm = r.shape
    hd = RW_HEAD
    nh = dm // hd
    heads = lambda x: x.reshape(bsz, nh, hd)
    hspec = pl.BlockSpec((1, nh, hd), lambda bi: (bi, 0, 0))
    pspec = pl.BlockSpec((nh, hd), lambda bi: (0, 0))
    sspec = pl.BlockSpec((1, nh, hd, hd), lambda bi: (bi, 0, 0, 0))
    o, s = pl.pallas_call(
        functools.partial(_rwkv_step_kernel, nh=nh),
        grid=(bsz,),
        in_specs=[hspec] * 7 + [pspec] * 3 + [sspec],
        out_specs=[hspec, sspec],
        out_shape=[jax.ShapeDtypeStruct((bsz, nh, hd), BF16), jax.ShapeDtypeStruct(s0.shape, F32)],
        scratch_shapes=[pltpu.VMEM((nh, hd), F32)],
        compiler_params=_cparams(("parallel",)),
        name="rwkv_step",
    )(*[heads(x) for x in (r, lw, k, v, a, b, gate)], ln_w.reshape(nh, hd), ln_b.reshape(nh, hd),
      r_k.reshape(nh, hd), s0)
    return o.reshape(bsz, dm), s


def _outproj_body(row, full, col, rc, out, scr):
    out[0][...] = rc[0][...] + jnp.dot(row[0][...].astype(BF16), col[0][...], preferred_element_type=F32)


def _outproj_call(a, w, h, t, name):
    m, kdim = a.shape
    n = w.shape[1]
    (o,) = _proj_call(_outproj_body, m=m, n=n, tm=_tiles(m, t), tn=512, row_ins=[a], full_ins=[], col_ins=[w],
                      rc_ins=[h], out_dtypes=[F32], scratch=[], name=name)
    return o


def _rwkv_layer(h, bsz, t, shift, wkv, norm_w, wts):
    m, dm = h.shape
    tm = _tiles(m, t)
    step = t == 1
    if step:
        prev = (shift.astype(F32), pl.BlockSpec((tm, dm), lambda i, j: (i, 0)))
        seq_tiles = 1
    else:
        seq_tiles = t // tm
        prev = (h, pl.BlockSpec((SUBLANES, dm), lambda i, j: (jnp.maximum(i * (tm // SUBLANES) - 1, 0), 0)))
    tn = 256
    bd = (jnp.arange(tn)[:, None] // RW_HEAD == jnp.arange(tn)[None, :] // RW_HEAD).astype(BF16)
    body = functools.partial(_rwkv_proj_body, tm=tm, seq_tiles=seq_tiles, step=step)
    lora = wts['w1'].shape[1]
    glora = wts['g1'].shape[1]
    r, lw, k, v, a, b, g = _proj_call(
        body, m=m, n=dm, tm=tm, tn=tn, row_ins=[h, prev],
        full_ins=[_row(norm_w), wts['mix'], wts['w1'], wts['a1'], wts['g1'], bd],
        col_ins=[wts['w_r'], wts['w_k'], wts['w_v'], wts['w2'], wts['a2'], wts['g2'],
                 _row(wts['w0']), _row(wts['a0']), _row(wts['k_k']), _row(wts['k_a'])],
        rc_ins=[], out_dtypes=[F32] * 7,
        scratch=[pltpu.VMEM((tm, dm), BF16)] * 3 + [pltpu.VMEM((tm, lora), BF16)] * 2
        + [pltpu.VMEM((tm, glora), BF16)],
        name="rwkv_proj")
    if step:
        shift_new = _rmsnorm_call(h, norm_w)
        og, s_new = _rwkv_step_call(r, lw, k, v, a, b, g, wts['ln_w'], wts['ln_b'], wts['r_k'], wkv)
    else:
        shift_new = _rmsnorm_call(h.reshape(bsz, t, dm)[:, -1], norm_w)
        og, s_new = _rwkv_chunk_call(r, lw, k, v, a, b, g, wts['ln_w'], wts['ln_b'], wts['r_k'], wkv, bsz, t,
                                     RW_CHUNK)
    return _outproj_call(og, wts['w_o'], h, t, "rwkv_out"), shift_new, s_new


def _la_chunk_kernel(q_ref, k_ref, v_ref, g_ref, gate_ref, nw_ref, s0_ref, o_ref, sout_ref, s_scr,
                     *, hb, dk, dv, cl, nch):
    s_scr[...] = s0_ref[0]
    rowi = _iota((cl, cl), 0)
    coli = _iota((cl, cl), 1)
    tri = (rowi >= coli).astype(BF16)
    levels = [s for s in (32, 16, 8) if 2 * s <= cl]
    masks = {s: ((rowi // s) % 2 == 1) & ((coli // s) == (rowi // s) - 1) for s in levels}
    rid = _iota((cl, 1), 0)
    nw = nw_ref[...]

    def chunk(c, carry):
        r0 = pl.multiple_of(c * cl, cl)
        for h in range(hb):
            ks = slice(h * dk, (h + 1) * dk)
            vs = slice(h * dv, (h + 1) * dv)
            q = q_ref[pl.ds(r0, cl), ks]
            k = k_ref[pl.ds(r0, cl), ks]
            g = g_ref[pl.ds(r0, cl), ks]
            v = v_ref[pl.ds(r0, cl), vs]
            cum = _xdot_l(tri, g)
            st = s_scr[h]
            o = _bdot(q * jnp.exp(cum), st)
            if levels:
                att = jnp.zeros((cl, cl), F32)
                for s in levels:
                    ends = [jnp.broadcast_to(cum[(bk + 1) * s - 1:(bk + 1) * s, :], (s, dk)) for bk in range(cl // s)]
                    e_blk = jnp.concatenate(ends, axis=0)
                    p_blk = jnp.concatenate([jnp.zeros((s, dk), F32)] + ends[:-1], axis=0)
                    a_s = _bdot_nt(q * jnp.exp(cum - p_blk), k * jnp.exp(e_blk - cum))
                    att = att + jnp.where(masks[s], a_s, 0.0)
                o = o + _bdot(att, v)
            o = o + jnp.sum(q * k, axis=-1, keepdims=True) * v
            for d in range(1, SUBLANES):
                valid = (rid % SUBLANES) >= d
                e = jnp.exp(jnp.where(valid, cum - pltpu.roll(cum, d, 0), -1e30))
                w = jnp.sum(q * pltpu.roll(k, d, 0) * e, axis=-1, keepdims=True)
                o = o + w * pltpu.roll(v, d, 0)
            on = o * lax.rsqrt(jnp.mean(o * o, axis=-1, keepdims=True) + NORM_EPS) * nw
            o_ref[pl.ds(r0, cl), vs] = (on * gate_ref[pl.ds(r0, cl), vs]).astype(o_ref.dtype)
            last = cum[cl - 1:cl, :]
            upd = _bdot_tn(k * jnp.exp(last - cum), v)
            dec = jnp.exp(jnp.transpose(jnp.broadcast_to(last, (LANES, dk))))
            s_scr[h] = jnp.concatenate([st[:, j * LANES:(j + 1) * LANES] * dec for j in range(dv // LANES)],
                                       axis=1) + upd
        return carry

    lax.fori_loop(0, nch, chunk, 0)
    sout_ref[0] = s_scr[...]


def _la_chunk_call(q, k, v, g, gate, norm_w, s0, bsz, t, cl, hb, name):
    m = q.shape[0]
    _, nh, dk, dv = s0.shape
    kern = functools.partial(_la_chunk_kernel, hb=hb, dk=dk, dv=dv, cl=cl, nch=t // cl)
    kspec = pl.BlockSpec((t, hb * dk), lambda bi, hi: (bi, hi))
    vspec = pl.BlockSpec((t, hb * dv), lambda bi, hi: (bi, hi))
    sspec = pl.BlockSpec((1, hb, dk, dv), lambda bi, hi: (bi, hi, 0, 0))
    return pl.pallas_call(
        kern,
        grid=(bsz, nh // hb),
        in_specs=[kspec, kspec, vspec, kspec, vspec, pl.BlockSpec((1, dv), lambda bi, hi: (0, 0)), sspec],
        out_specs=[vspec, sspec],
        out_shape=[jax.ShapeDtypeStruct((m, nh * dv), BF16), jax.ShapeDtypeStruct(s0.shape, F32)],
        scratch_shapes=[pltpu.VMEM((hb, dk, dv), F32)],
        compiler_params=_cparams(("parallel", "parallel")),
        name=name,
    )(q, k, v, g, gate, _row(norm_w), s0)


def _la_step_kernel(q_ref, k_ref, v_ref, g_ref, gate_ref, nw_ref, s0_ref, o_ref, sout_ref, *, nh, dk, dv):
    first = _iota((STEP_PAD, 1), 0) == 0

    def pad(x):
        return jnp.where(first, x, 0.0)

    def head(h, carry):
        row = lambda ref: ref[0, pl.ds(h, 1), :]
        st = s0_ref[0, h]
        g = row(g_ref)
        dec = jnp.exp(jnp.transpose(jnp.broadcast_to(g, (LANES, dk))))
        upd = _bdot_tn(pad(row(k_ref)), pad(row(v_ref)))
        sn = jnp.concatenate([st[:, j * LANES:(j + 1) * LANES] * dec for j in range(dv // LANES)], axis=1) + upd
        sout_ref[0, h] = sn
        o = _bdot(pad(row(q_ref)), sn)[0:1]
        on = o * lax.rsqrt(jnp.mean(o * o, axis=-1, keepdims=True) + NORM_EPS) * nw_ref[...]
        o_ref[0, pl.ds(h, 1), :] = (on * row(gate_ref)).astype(o_ref.dtype)
        return carry

    lax.fori_loop(0, nh, head, 0, unroll=2)


def _la_step_call(q, k, v, g, gate, norm_w, s0, name):
    bsz, nh, dk, dv = s0.shape
    kh = lambda x: x.reshape(bsz, nh, dk)
    vh = lambda x: x.reshape(bsz, nh, dv)
    kspec = pl.BlockSpec((1, nh, dk), lambda bi: (bi, 0, 0))
    vspec = pl.BlockSpec((1, nh, dv), lambda bi: (bi, 0, 0))
    sspec = pl.BlockSpec((1, nh, dk, dv), lambda bi: (bi, 0, 0, 0))
    o, s = pl.pallas_call(
        functools.partial(_la_step_kernel, nh=nh, dk=dk, dv=dv),
        grid=(bsz,),
        in_specs=[kspec, kspec, vspec, kspec, vspec, pl.BlockSpec((1, dv), lambda bi: (0, 0)), sspec],
        out_specs=[vspec, sspec],
        out_shape=[jax.ShapeDtypeStruct((bsz, nh, dv), F32), jax.ShapeDtypeStruct(s0.shape, F32)],
        compiler_params=_cparams(("parallel",)),
        name=name,
    )(kh(q), kh(k), vh(v), kh(g), vh(gate), _row(norm_w), s0)
    return o.reshape(bsz, nh * dv), s


def _la_run(q, k, v, g, gate, norm_w, s0, bsz, t, hb, name):
    if t == 1:
        return _la_step_call(q, k, v, g, gate, norm_w, s0, name + "_step")
    return _la_chunk_call(q, k, v, g, gate, norm_w, s0, bsz, t, LA_CHUNK, hb, name)


def _norm_prologue(h_ref, nw_ref, xn_s):
    @pl.when(pl.program_id(1) == 0)
    def _():
        xn_s[...] = _rms(h_ref[...], nw_ref[...]).astype(BF16)


def _gla_qkg_body(row, full, col, rc, out, scr, *, scale):
    nw_ref, gk1_ref = full
    wq, wk, gk2, bgk = col
    xn_s, t1_s = scr

    @pl.when(pl.program_id(1) == 0)
    def _():
        xn = _rms(row[0][...], nw_ref[...]).astype(BF16)
        xn_s[...] = xn
        t1_s[...] = jnp.dot(xn, gk1_ref[...], preferred_element_type=F32).astype(BF16)

    xn = xn_s[...]
    out[0][...] = jnp.dot(xn, wq[...], preferred_element_type=F32) * scale
    out[1][...] = jnp.dot(xn, wk[...], preferred_element_type=F32)
    lg = jnp.dot(t1_s[...], gk2[...], preferred_element_type=F32) + bgk[...]
    out[2][...] = jax.nn.log_sigmoid(lg) / GLA_GATE_TEMP


def _vg_body(row, full, col, rc, out, scr):
    _norm_prologue(row[0], full[0], scr[0])
    xn = scr[0][...]
    out[0][...] = jnp.dot(xn, col[0][...], preferred_element_type=F32)
    out[1][...] = jax.nn.silu(jnp.dot(xn, col[1][...], preferred_element_type=F32))


def _gla_layer(h, bsz, t, s0, norm_w, wts):
    m, dm = h.shape
    tm = _tiles(m, t)
    nh = GLA_HEADS
    dk = wts['w_q'].shape[1] // nh
    rank = wts['gk1'].shape[1]
    q, k, lg = _proj_call(functools.partial(_gla_qkg_body, scale=dk ** -0.5), m=m, n=nh * dk, tm=tm, tn=512,
                          row_ins=[h], full_ins=[_row(norm_w), wts['gk1']],
                          col_ins=[wts['w_q'], wts['w_k'], wts['gk2'], _row(wts['b_gk'])], rc_ins=[],
                          out_dtypes=[F32] * 3,
                          scratch=[pltpu.VMEM((tm, dm), BF16), pltpu.VMEM((tm, rank), BF16)], name="gla_qkg")
    v, gate = _proj_call(_vg_body, m=m, n=wts['w_v'].shape[1], tm=tm, tn=512, row_ins=[h],
                         full_ins=[_row(norm_w)], col_ins=[wts['w_v'], wts['w_g']], rc_ins=[],
                         out_dtypes=[F32] * 2, scratch=[pltpu.VMEM((tm, dm), BF16)], name="gla_vg")
    og, s_new = _la_run(q, k, v, lg, gate, wts['norm'], s0, bsz, t, 1, "gla_chunk")
    return _outproj_call(og, wts['w_o'], h, t, "gla_out"), s_new


def _hgrn_proj_body(row, full, col, rc, out, scr, *, scale, layer_idx):
    wq, wf, wi, wg, lbp = col
    _norm_prologue(row[0], full[0], scr[0])
    xn = scr[0][...]
    lbx = lbp[...]
    ex = jnp.exp(lbx - jnp.max(lbx, axis=0, keepdims=True))
    lb = jnp.sum(ex[1:layer_idx + 1], axis=0, keepdims=True) / jnp.sum(ex, axis=0, keepdims=True)
    z = jnp.dot(xn, wf[...], preferred_element_type=F32)
    l1m = jnp.log1p(-lb)
    out[0][...] = jnp.dot(xn, wq[...], preferred_element_type=F32) * scale
    out[1][...] = jnp.exp(l1m + jax.nn.log_sigmoid(-z))
    out[2][...] = jnp.logaddexp(jnp.log(lb), l1m + jax.nn.log_sigmoid(z))
    out[3][...] = jnp.dot(xn, wi[...], preferred_element_type=F32)
    out[4][...] = jax.nn.silu(jnp.dot(xn, wg[...], preferred_element_type=F32))


def _hgrn_layer(h, bsz, t, s0, norm_w, wts, layer_idx):
    m, dm = h.shape
    tm = _tiles(m, t)
    body = functools.partial(_hgrn_proj_body, scale=HG_EXPAND ** -0.5, layer_idx=layer_idx)
    q, k, lf, v, gate = _proj_call(body, m=m, n=dm, tm=tm, tn=512, row_ins=[h], full_ins=[_row(norm_w)],
                                   col_ins=[wts['w_q'], wts['w_f'], wts['w_i'], wts['w_g'], wts['lb']],
                                   rc_ins=[], out_dtypes=[F32] * 5, scratch=[pltpu.VMEM((tm, dm), BF16)],
                                   name="hgrn_proj")
    og, s_new = _la_run(q, k, v, lf, gate, wts['norm'], s0, bsz, t, 2, "hgrn_chunk")
    return _outproj_call(og, wts['w_o'], h, t, "hgrn_out"), s_new


def _ffn_kernel(*refs, step, tm, tf, nf, seq_tiles, final_norm):
    h_ref, nw_ref, wup_ref, wgate_ref, wc_ref, bc_ref, wdown_ref = refs[:7]
    p = 7
    if step:
        p0_ref, p1_ref = refs[p:p + 2]; p += 2
    if final_norm:
        fnw_ref = refs[p]; p += 1
    out_ref, tail_ref = refs[p:p + 2]
    xn_s, carry_s = refs[p + 2:]
    i = pl.program_id(0)
    f = pl.program_id(1)

    @pl.when(f == 0)
    def _():
        hv = h_ref[...]
        xn_s[...] = _rms(hv, nw_ref[...]).astype(BF16)
        out_ref[...] = hv

    xn = xn_s[...]
    u = jnp.dot(xn, wup_ref[...], preferred_element_type=F32)
    gate = jnp.dot(xn, wgate_ref[...], preferred_element_type=F32)
    wc = wc_ref[...]
    if step:
        u2, u1 = p0_ref[...], p1_ref[...]
        tail_ref[...] = u
    else:
        @pl.when(i % seq_tiles == 0)
        def _():
            carry_s[f] = jnp.zeros((SUBLANES, tf), F32)

        prev = carry_s[f]
        rid = _iota((tm, 1), 0)
        u1 = jnp.where(rid == 0, prev[SUBLANES - 1:SUBLANES], pltpu.roll(u, 1, 0))
        u2 = jnp.where(rid == 0, prev[SUBLANES - 2:SUBLANES - 1],
                       jnp.where(rid == 1, prev[SUBLANES - 1:SUBLANES], pltpu.roll(u, 2, 0)))
        tail = u[tm - SUBLANES:tm]
        carry_s[f] = tail
        tail_ref[0] = tail
    c = u2 * wc[0:1] + u1 * wc[1:2] + u * wc[2:3] + bc_ref[...]
    act = (jax.nn.gelu(c) * gate).astype(BF16)
    out_ref[...] += jnp.dot(act, wdown_ref[...], preferred_element_type=F32)
    if final_norm:
        @pl.when(f == nf - 1)
        def _():
            out_ref[...] = _rms(out_ref[...], fnw_ref[...])


def _ffn_call(h, bsz, t, conv_prev, norm_w, w_up, w_gate, w_conv, b_conv, w_down, final_norm_w):
    m, dm = h.shape
    dff = w_up.shape[1]
    step = t == 1
    tm = _tiles(m, t)
    tf = 512
    nf = dff // tf
    seq_tiles = 1 if step else t // tm
    final_norm = final_norm_w is not None
    args = [h, _row(norm_w), w_up, w_gate, w_conv.astype(F32), _row(b_conv.astype(F32)), w_down]
    specs = [pl.BlockSpec((tm, dm), lambda i, f: (i, 0)), pl.BlockSpec((1, dm), lambda i, f: (0, 0)),
             pl.BlockSpec((dm, tf), lambda i, f: (0, f)), pl.BlockSpec((dm, tf), lambda i, f: (0, f)),
             pl.BlockSpec((CONV_W, tf), lambda i, f: (0, f)), pl.BlockSpec((1, tf), lambda i, f: (0, f)),
             pl.BlockSpec((tf, dm), lambda i, f: (f, 0))]
    if step:
        args += [conv_prev[:, 0].astype(F32), conv_prev[:, 1].astype(F32)]
        specs += [pl.BlockSpec((tm, tf), lambda i, f: (i, f))] * 2
        tail_shape = jax.ShapeDtypeStruct((m, dff), F32)
        tail_spec = pl.BlockSpec((tm, tf), lambda i, f: (i, f))
    else:
        tail_shape = jax.ShapeDtypeStruct((m // tm, SUBLANES, dff), F32)
        tail_spec = pl.BlockSpec((1, SUBLANES, tf), lambda i, f: (i, 0, f))
    if final_norm:
        args.append(_row(final_norm_w))
        specs.append(pl.BlockSpec((1, dm), lambda i, f: (0, 0)))
    kern = functools.partial(_ffn_kernel, step=step, tm=tm, tf=tf, nf=nf, seq_tiles=seq_tiles,
                             final_norm=final_norm)
    out, tail = pl.pallas_call(
        kern,
        grid=(m // tm, nf),
        in_specs=specs,
        out_specs=[pl.BlockSpec((tm, dm), lambda i, f: (i, 0)), tail_spec],
        out_shape=[jax.ShapeDtypeStruct((m, dm), F32), tail_shape],
        scratch_shapes=[pltpu.VMEM((tm, dm), BF16), pltpu.VMEM((nf, SUBLANES, tf), F32)],
        compiler_params=_cparams(("arbitrary", "arbitrary")),
        name="conv_ffn",
    )(*args)
    if step:
        conv_new = jnp.stack([conv_prev[:, 1].astype(F32), tail], axis=1)
    else:
        conv_new = tail.reshape(bsz, seq_tiles, SUBLANES, dff)[:, -1, SUBLANES - (CONV_W - 1):]
    return out, conv_new


def _trunk(x, s5_re, s5_im, rw_shift, rw_wkv, gla_s, hg_s, ffn_conv, p):
    bsz, t, dm = x.shape
    h = x.reshape(bsz * t, dm).astype(F32)
    depth = p['norm_mix'].shape[0]
    conv_new = []
    for i in range(depth):
        kind = i % 4
        nw = p['norm_mix'][i]
        if kind == 0:
            h, s5_re, s5_im = _s5_layer(h, bsz, t, s5_re, s5_im, nw, p['s5'], p['s5_c_re'], p['s5_c_im'],
                                        p['s5_d'], p['s5_w_glu'])
        elif kind == 1:
            h, rw_shift, rw_wkv = _rwkv_layer(h, bsz, t, rw_shift, rw_wkv, nw, p['rw'])
        elif kind == 2:
            h, gla_s = _gla_layer(h, bsz, t, gla_s, nw, p['gla'])
        else:
            h, hg_s = _hgrn_layer(h, bsz, t, hg_s, nw, p['hg'], i)
        fin = p['norm_final'] if i == depth - 1 else None
        h, c = _ffn_call(h, bsz, t, ffn_conv[i], p['norm_ffn'][i], p['ffn_w_up'][i], p['ffn_w_gate'][i],
                         p['ffn_w_conv'][i], p['ffn_b_conv'][i], p['ffn_w_down'][i], fin)
        conv_new.append(c)
    return h.reshape(bsz, t, dm), s5_re, s5_im, rw_shift, rw_wkv, gla_s, hg_s, jnp.stack(conv_new)


def _pad_cols(w, n):
    return jnp.pad(w, ((0, 0), (0, n - w.shape[1])))


def _pad_rows(w, n):
    return jnp.pad(w, ((0, n - w.shape[0]), (0, 0)))


def kernel(x_prompt, x_sample, state_s5_re, state_s5_im, state_rwkv_shift, state_rwkv_wkv, state_gla, state_hgrn, state_ffn_conv, norm_mix, norm_ffn, norm_final, s5_lambda_re, s5_lambda_im, s5_log_dt, s5_b_re, s5_b_im, s5_c_re, s5_c_im, s5_d, s5_w_glu, rw_mix, rw_w_r, rw_w_k, rw_w_v, rw_w_o, rw_w0, rw_w1, rw_w2, rw_a0, rw_a1, rw_a2, rw_g1, rw_g2, rw_k_k, rw_k_a, rw_r_k, rw_ln_w, rw_ln_b, gla_w_q, gla_w_k, gla_w_v, gla_w_gk1, gla_w_gk2, gla_b_gk, gla_w_g, gla_norm, gla_w_o, hg_w_q, hg_w_f, hg_w_i, hg_w_g, hg_lb, hg_norm, hg_w_o, ffn_w_up, ffn_w_gate, ffn_w_conv, ffn_b_conv, ffn_w_down):
    bf = lambda w: w.astype(BF16)
    f32 = lambda w: w.astype(F32)
    lora = LANES * pl.cdiv(rw_w1.shape[1], LANES)
    alora = LANES * pl.cdiv(rw_a1.shape[1], LANES)
    grank = LANES * pl.cdiv(gla_w_gk1.shape[1], LANES)
    p = {
        'norm_mix': f32(norm_mix), 'norm_ffn': f32(norm_ffn), 'norm_final': f32(norm_final),
        's5': _s5_params(s5_lambda_re, s5_lambda_im, s5_log_dt, s5_b_re, s5_b_im, s5_c_re, s5_c_im, s5_d),
        's5_c_re': s5_c_re, 's5_c_im': s5_c_im, 's5_d': s5_d, 's5_w_glu': bf(s5_w_glu),
        'rw': dict(mix=_pad_rows(f32(rw_mix), SUBLANES), w_r=bf(rw_w_r), w_k=bf(rw_w_k), w_v=bf(rw_w_v),
                   w_o=bf(rw_w_o), w0=f32(rw_w0), w1=bf(_pad_cols(rw_w1, lora)), w2=bf(_pad_rows(rw_w2, lora)),
                   a0=f32(rw_a0), a1=bf(_pad_cols(rw_a1, alora)), a2=bf(_pad_rows(rw_a2, alora)),
                   g1=bf(rw_g1), g2=bf(rw_g2), k_k=f32(rw_k_k), k_a=f32(rw_k_a), r_k=f32(rw_r_k).reshape(-1),
                   ln_w=f32(rw_ln_w), ln_b=f32(rw_ln_b)),
        'gla': dict(w_q=bf(gla_w_q), w_k=bf(gla_w_k), w_v=bf(gla_w_v), gk1=bf(_pad_cols(gla_w_gk1, grank)),
                    gk2=bf(_pad_rows(gla_w_gk2, grank)), b_gk=f32(gla_b_gk), w_g=bf(gla_w_g), norm=f32(gla_norm),
                    w_o=bf(gla_w_o)),
        'hg': dict(w_q=bf(hg_w_q), w_f=bf(hg_w_f), w_i=bf(hg_w_i), w_g=bf(hg_w_g), lb=f32(hg_lb),
                   norm=f32(hg_norm), w_o=bf(hg_w_o)),
        'ffn_w_up': bf(ffn_w_up), 'ffn_w_gate': bf(ffn_w_gate), 'ffn_w_conv': ffn_w_conv,
        'ffn_b_conv': ffn_b_conv, 'ffn_w_down': bf(ffn_w_down),
    }
    nb, _, dm = x_prompt.shape
    depth = norm_mix.shape[0]
    dff = ffn_w_up.shape[-1]
    z_s5 = jnp.zeros((nb,) + state_s5_re.shape[1:], F32)
    outs_p = _trunk(x_prompt, z_s5, z_s5, jnp.zeros((nb, dm), F32),
                    jnp.zeros((nb,) + state_rwkv_wkv.shape[1:], F32), jnp.zeros((nb,) + state_gla.shape[1:], F32),
                    jnp.zeros((nb,) + state_hgrn.shape[1:], F32), jnp.zeros((depth, nb, CONV_W - 1, dff), F32), p)
    outs_s = _trunk(x_sample, f32(state_s5_re), f32(state_s5_im), state_rwkv_shift, f32(state_rwkv_wkv),
                    f32(state_gla), f32(state_hgrn), state_ffn_conv, p)
    return (outs_p[0], outs_s[0]) + tuple(outs_p[1:]) + tuple(outs_s[1:])
```

```python
import functools
import math

import jax
import jax.numpy as jnp
from jax import lax
from jax.experimental import pallas as pl
from jax.experimental.pallas import tpu as pltpu

F32 = jnp.float32
BF16 = jnp.bfloat16

NORM_EPS = 1e-6
RW_LN_EPS = 64e-5
RW_HEAD = 64
S5_GROUP = 16
S5_STATE = 64
S5_CHUNK = 16
GLA_HEADS = 4
GLA_GATE_TEMP = 16.0
HG_EXPAND = 128
CONV_W = 3
LA_CHUNK = 64
RW_CHUNK = 64
STEP_PAD = 16
LANES = 128
SUBLANES = 8
VMEM_LIMIT = 56 * 1024 * 1024


def _cparams(sem):
    return pltpu.CompilerParams(dimension_semantics=sem, vmem_limit_bytes=VMEM_LIMIT)


def _bdot(a, b):
    return jnp.dot(a.astype(BF16), b.astype(BF16), preferred_element_type=F32)


def _bdot_nt(a, b):
    return lax.dot_general(a.astype(BF16), b.astype(BF16), (((1,), (1,)), ((), ())),
                           preferred_element_type=F32)


def _bdot_tn(a, b):
    return lax.dot_general(a.astype(BF16), b.astype(BF16), (((0,), (0,)), ((), ())),
                           preferred_element_type=F32)


def _split3(x):
    hi = x.astype(BF16)
    r = x - hi.astype(F32)
    mid = r.astype(BF16)
    lo = (r - mid.astype(F32)).astype(BF16)
    return hi, mid, lo


def _xdot_l(m, x):
    hi, mid, lo = _split3(x)
    d = lambda p: jnp.dot(m, p, preferred_element_type=F32)
    return d(hi) + d(mid) + d(lo)


def _xdot_r(x, m):
    hi, mid, lo = _split3(x)
    d = lambda p: jnp.dot(p, m, preferred_element_type=F32)
    return d(hi) + d(mid) + d(lo)


def _rms(x, g):
    y = x * lax.rsqrt(jnp.mean(x * x, axis=-1, keepdims=True) + NORM_EPS)
    return y * g


def _iota(shape, dim):
    return lax.broadcasted_iota(jnp.int32, shape, dim)


def _rmsnorm_kernel(x_ref, g_ref, o_ref):
    o_ref[...] = _rms(x_ref[...], g_ref[...])


def _rmsnorm_call(x, g):
    r, d = x.shape
    tr = 512 if r % 512 == 0 else r
    return pl.pallas_call(
        _rmsnorm_kernel,
        grid=(r // tr,),
        in_specs=[pl.BlockSpec((tr, d), lambda i: (i, 0)), pl.BlockSpec((1, d), lambda i: (0, 0))],
        out_specs=pl.BlockSpec((tr, d), lambda i: (i, 0)),
        out_shape=jax.ShapeDtypeStruct((r, d), F32),
        compiler_params=_cparams(("parallel",)),
        name="rmsnorm",
    )(x, g.reshape(1, d))


def _proj_call(body, *, m, n, tm, tn, row_ins, full_ins, col_ins, rc_ins, out_dtypes, scratch, name):
    args, specs = [], []
    for a in row_ins:
        if isinstance(a, tuple):
            args.append(a[0]); specs.append(a[1])
        else:
            args.append(a); specs.append(pl.BlockSpec((tm, a.shape[1]), lambda i, j: (i, 0)))
    for a in full_ins:
        args.append(a); specs.append(pl.BlockSpec(a.shape, lambda i, j, nd=a.ndim: (0,) * nd))
    for a in col_ins:
        args.append(a); specs.append(pl.BlockSpec((a.shape[0], tn), lambda i, j: (0, j)))
    for a in rc_ins:
        args.append(a); specs.append(pl.BlockSpec((tm, tn), lambda i, j: (i, j)))
    nr, nf, nc, nrc, no = len(row_ins), len(full_ins), len(col_ins), len(rc_ins), len(out_dtypes)

    def kernel(*refs):
        p = 0
        groups = []
        for cnt in (nr, nf, nc, nrc, no):
            groups.append(refs[p:p + cnt]); p += cnt
        body(*groups, refs[p:])

    outs = pl.pallas_call(
        kernel,
        grid=(m // tm, n // tn),
        in_specs=specs,
        out_specs=[pl.BlockSpec((tm, tn), lambda i, j: (i, j)) for _ in out_dtypes],
        out_shape=[jax.ShapeDtypeStruct((m, n), dt) for dt in out_dtypes],
        scratch_shapes=scratch,
        compiler_params=_cparams(("parallel", "arbitrary")),
        name=name,
    )(*args)
    return outs


def _row(v):
    return v.reshape(1, -1)


def _tiles(m, t):
    if t == 1:
        return 512 if m % 512 == 0 else m
    return min(512, t)


def _s5_params(lam_re, lam_im, log_dt, b_re, b_im, c_re, c_im, d):
    hp = lax.Precision.HIGHEST
    g, n = lam_re.shape
    p = b_re.shape[-1]
    cl = S5_CHUNK
    lr = jnp.minimum(lam_re.astype(F32), -1e-4)
    li = lam_im.astype(F32)
    dt = jnp.exp(log_dt.astype(F32))[:, None]
    mag = jnp.exp(lr * dt)
    ab_re = mag * jnp.cos(li * dt)
    ab_im = mag * jnp.sin(li * dt)
    den = lr * lr + li * li
    f_re = ((ab_re - 1.0) * lr + ab_im * li) / den
    f_im = (ab_im * lr - (ab_re - 1.0) * li) / den
    br, bi = b_re.astype(F32), b_im.astype(F32)
    bb_re = f_re[..., None] * br - f_im[..., None] * bi
    bb_im = f_re[..., None] * bi + f_im[..., None] * br
    cr, ci = c_re.astype(F32), c_im.astype(F32)

    pr, pi = [jnp.ones_like(ab_re)], [jnp.zeros_like(ab_im)]
    for _ in range(cl):
        pr.append(pr[-1] * ab_re - pi[-1] * ab_im)
        pi.append(pr[-2] * ab_im + pi[-1] * ab_re)
    pw_re = jnp.stack(pr)
    pw_im = jnp.stack(pi)
    ajb_re = pw_re[:cl, :, :, None] * bb_re - pw_im[:cl, :, :, None] * bb_im
    ajb_im = pw_re[:cl, :, :, None] * bb_im + pw_im[:cl, :, :, None] * bb_re
    kj = (jnp.einsum('gpn,jgnq->jgpq', cr, ajb_re, precision=hp)
          - jnp.einsum('gpn,jgnq->jgpq', ci, ajb_im, precision=hp))
    s_idx = jnp.arange(cl)[:, None]
    t_idx = jnp.arange(cl)[None, :]
    lag = t_idx - s_idx
    kst = kj[jnp.clip(lag, 0, cl - 1)]
    kst = jnp.where((lag >= 0)[:, :, None, None, None], kst, 0.0)
    mt = kst.transpose(2, 0, 4, 1, 3).reshape(g, cl * p, cl * p)
    w_re = ajb_re[::-1].transpose(1, 0, 3, 2).reshape(g, cl * p, n)
    w_im = ajb_im[::-1].transpose(1, 0, 3, 2).reshape(g, cl * p, n)
    wm = jnp.concatenate([w_re, w_im], axis=-1)
    ca_re = cr[None] * pw_re[1:, :, None, :] - ci[None] * pw_im[1:, :, None, :]
    ca_im = cr[None] * pw_im[1:, :, None, :] + ci[None] * pw_re[1:, :, None, :]
    v_re = ca_re.transpose(1, 3, 0, 2).reshape(g, n, cl * p)
    v_im = -ca_im.transpose(1, 3, 0, 2).reshape(g, n, cl * p)
    vm = jnp.concatenate([v_re, v_im], axis=1)
    ar, ai = pw_re[cl], pw_im[cl]
    pcs, qcs = [], []
    for _ in range(SUBLANES):
        pcs.append(jnp.concatenate([ar, ar], axis=-1))
        qcs.append(jnp.concatenate([-ai, ai], axis=-1))
        ar, ai = ar * ar - ai * ai, 2.0 * ar * ai
    pc = jnp.stack(pcs, axis=1)
    qc = jnp.stack(qcs, axis=1)
    dtile = jnp.tile(d.astype(F32).reshape(g, 1, p), (1, 1, cl))
    return dict(ab_re=ab_re, ab_im=ab_im, bb_re=bb_re, bb_im=bb_im, mt=mt.astype(BF16), wm=wm.astype(BF16),
                vm=vm.astype(BF16), pc=pc, qc=qc, dtile=dtile)


def _s5_seq_kernel(u_ref, mt_ref, w_ref, vm_ref, pc_ref, qc_ref, dt_ref, h0_ref, z_ref, hf_ref, *, gb, nc):
    half = S5_STATE
    rid = _iota((nc, 1), 0)
    nlev = int(math.log2(nc))
    for g in range(gb):
        u = u_ref[0, g]
        ub = u.astype(BF16)
        y_intra = jnp.dot(ub, mt_ref[g], preferred_element_type=F32)
        x = jnp.dot(ub, w_ref[g], preferred_element_type=F32)
        pc = pc_ref[g]
        qc = qc_ref[g]
        h0 = h0_ref[0, g]
        h0c = pc[0:1] * h0 + qc[0:1] * pltpu.roll(h0, half, 1)
        x = x + jnp.where(rid == 0, h0c, 0.0)
        for lv in range(nlev):
            sh = 1 << lv
            xs = jnp.where(rid >= sh, pltpu.roll(x, sh, 0), 0.0)
            x = x + pc[lv:lv + 1] * xs + qc[lv:lv + 1] * pltpu.roll(xs, half, 1)
        hprev = jnp.where(rid == 0, h0, pltpu.roll(x, 1, 0))
        y = y_intra + jnp.dot(hprev.astype(BF16), vm_ref[g], preferred_element_type=F32) + dt_ref[g] * u
        z_ref[0, g] = jax.nn.gelu(y)
        hf_ref[0, g] = x[nc - 1:nc, :]


def _s5_seq_call(u4, prm, h0):
    b, g, nc, w = u4.shape
    n2 = 2 * S5_STATE
    gb = 8
    kern = functools.partial(_s5_seq_kernel, gb=gb, nc=nc)
    gspec = lambda shape: pl.BlockSpec((gb,) + shape, lambda gi, bi: (gi, 0, 0))
    return pl.pallas_call(
        kern,
        grid=(g // gb, b),
        in_specs=[pl.BlockSpec((1, gb, nc, w), lambda gi, bi: (bi, gi, 0, 0)),
                  gspec((w, w)), gspec((w, n2)), gspec((n2, w)), gspec((SUBLANES, n2)), gspec((SUBLANES, n2)),
                  gspec((1, w)),
                  pl.BlockSpec((1, gb, 1, n2), lambda gi, bi: (bi, gi, 0, 0))],
        out_specs=[pl.BlockSpec((1, gb, nc, w), lambda gi, bi: (bi, gi, 0, 0)),
                   pl.BlockSpec((1, gb, 1, n2), lambda gi, bi: (bi, gi, 0, 0))],
        out_shape=[jax.ShapeDtypeStruct((b, g, nc, w), F32), jax.ShapeDtypeStruct((b, g, 1, n2), F32)],
        compiler_params=_cparams(("parallel", "arbitrary")),
        name="s5_seq",
    )(u4, prm['mt'], prm['wm'], prm['vm'], prm['pc'], prm['qc'], prm['dtile'], h0)


def _s5_step_kernel(u_ref, hr_ref, hi_ref, ar_ref, ai_ref, bdr_ref, bdi_ref, cdr_ref, cdi_ref, d_ref,
                    z_ref, hro_ref, hio_ref):
    u = u_ref[...]
    ar, ai = ar_ref[...], ai_ref[...]
    hr, hi = hr_ref[...], hi_ref[...]
    hr2 = ar * hr - ai * hi + _bdot(u, bdr_ref[0])
    hi2 = ar * hi + ai * hr + _bdot(u, bdi_ref[0])
    y = _bdot(hr2, cdr_ref[0]) - _bdot(hi2, cdi_ref[0]) + d_ref[...] * u
    z_ref[...] = jax.nn.gelu(y)
    hro_ref[...] = hr2
    hio_ref[...] = hi2


def _s5_step_call(u, h_re, h_im, prm, c_re, c_im, d):
    b, dm = u.shape
    g, n = prm['ab_re'].shape
    p = S5_GROUP
    gpb = LANES // p
    nb = g // gpb
    eye = jnp.eye(gpb, dtype=F32)
    bd = lambda bb: jnp.einsum('kgnp,gh->kgphn', bb.reshape(nb, gpb, n, p), eye).reshape(nb, gpb * p, gpb * n)
    cd = lambda cc: jnp.einsum('kgpn,gh->kgnhp', cc.astype(F32).reshape(nb, gpb, p, n), eye).reshape(
        nb, gpb * n, gpb * p)
    bdr, bdi = bd(prm['bb_re']).astype(BF16), bd(prm['bb_im']).astype(BF16)
    cdr, cdi = cd(c_re).astype(BF16), cd(c_im).astype(BF16)
    wn = gpb * n
    cspec = lambda wd: pl.BlockSpec((b, wd), lambda k: (0, k))
    rspec = lambda wd: pl.BlockSpec((1, wd), lambda k: (0, k))
    z, hr2, hi2 = pl.pallas_call(
        _s5_step_kernel,
        grid=(nb,),
        in_specs=[cspec(LANES), cspec(wn), cspec(wn), rspec(wn), rspec(wn),
                  pl.BlockSpec((1, LANES, wn), lambda k: (k, 0, 0)), pl.BlockSpec((1, LANES, wn), lambda k: (k, 0, 0)),
                  pl.BlockSpec((1, wn, LANES), lambda k: (k, 0, 0)), pl.BlockSpec((1, wn, LANES), lambda k: (k, 0, 0)),
                  rspec(LANES)],
        out_specs=[cspec(LANES), cspec(wn), cspec(wn)],
        out_shape=[jax.ShapeDtypeStruct((b, dm), F32), jax.ShapeDtypeStruct((b, g * n), F32),
                   jax.ShapeDtypeStruct((b, g * n), F32)],
        compiler_params=_cparams(("parallel",)),
        name="s5_step",
    )(u, h_re.reshape(b, g * n), h_im.reshape(b, g * n), prm['ab_re'].reshape(1, g * n),
      prm['ab_im'].reshape(1, g * n), bdr, bdi, cdr, cdi, _row(d.astype(F32)))
    return z, hr2.reshape(b, g, n), hi2.reshape(b, g, n)


def _glu_body(row, full, col, rc, out, scr):
    (z_ref,) = row
    (w_ref,) = col
    zt_ref, h_ref = rc
    (zb,) = scr

    @pl.when(pl.program_id(1) == 0)
    def _():
        zb[...] = z_ref[...].astype(BF16)

    zt = zt_ref[...]
    gate = jax.nn.sigmoid(jnp.dot(zb[...], w_ref[...], preferred_element_type=F32))
    out[0][...] = h_ref[...] + zt * gate


def _s5_layer(h, bsz, t, st_re, st_im, norm_w, prm, c_re, c_im, d, w_glu):
    m, dm = h.shape
    g, n, p = dm // S5_GROUP, S5_STATE, S5_GROUP
    xn = _rmsnorm_call(h, norm_w)
    if t == 1:
        z, s_re, s_im = _s5_step_call(xn, st_re, st_im, prm, c_re, c_im, d)
    else:
        cl = S5_CHUNK
        nc = t // cl
        u4 = xn.reshape(bsz, nc, cl, g, p).transpose(0, 3, 1, 2, 4).reshape(bsz, g, nc, cl * p)
        h0 = jnp.concatenate([st_re, st_im], axis=-1).reshape(bsz, g, 1, 2 * n)
        z4, hf = _s5_seq_call(u4, prm, h0)
        z = z4.reshape(bsz, g, nc, cl, p).transpose(0, 2, 3, 1, 4).reshape(m, dm)
        s_re, s_im = hf[:, :, 0, :n], hf[:, :, 0, n:]
    tm = _tiles(m, t)
    (h_new,) = _proj_call(_glu_body, m=m, n=dm, tm=tm, tn=512, row_ins=[z], full_ins=[], col_ins=[w_glu],
                          rc_ins=[z, h], out_dtypes=[F32], scratch=[pltpu.VMEM((tm, dm), BF16)], name="s5_glu")
    return h_new, s_re, s_im


def _rwkv_proj_body(row, full, col, rc, out, scr, *, tm, seq_tiles, step):
    h_ref, prev_ref = row
    nw_ref, mix_ref, w1_ref, a1_ref, g1_ref, bd_ref = full
    wr, wk, wv, w2, a2, g2, w0, a0, kk_ref, ka_ref = col
    r_o, lw_o, k_o, v_o, a_o, b_o, g_o = out
    xr_s, xk_s, xv_s, tw_s, ta_s, tg_s = scr
    i = pl.program_id(0)

    @pl.when(pl.program_id(1) == 0)
    def _():
        nw = nw_ref[...]
        xn = _rms(h_ref[...], nw)
        if step:
            xprev = prev_ref[...]
        else:
            last = _rms(prev_ref[...], nw)[SUBLANES - 1:SUBLANES, :]
            last = jnp.where(i % seq_tiles == 0, 0.0, last)
            xprev = jnp.where(_iota((tm, 1), 0) == 0, last, pltpu.roll(xn, 1, 0))
        xx = xprev - xn
        mix = mix_ref[...]
        xr_s[...] = (xn + xx * mix[0:1]).astype(BF16)
        xk_s[...] = (xn + xx * mix[2:3]).astype(BF16)
        xv_s[...] = (xn + xx * mix[3:4]).astype(BF16)
        tw_s[...] = jnp.tanh(_bdot(xn + xx * mix[1:2], w1_ref[...])).astype(BF16)
        ta_s[...] = _bdot(xn + xx * mix[4:5], a1_ref[...]).astype(BF16)
        tg_s[...] = jax.nn.sigmoid(_bdot(xn + xx * mix[5:6], g1_ref[...])).astype(BF16)

    dot = lambda a, b: jnp.dot(a[...], b[...], preferred_element_type=F32)
    r = dot(xr_s, wr)
    k = dot(xk_s, wk)
    v = dot(xv_s, wv)
    wraw = -jax.nn.softplus(-(w0[...] + dot(tw_s, w2))) - 0.5
    a = jax.nn.sigmoid(a0[...] + dot(ta_s, a2))
    kk = k * kk_ref[...]
    ss = _xdot_r(kk * kk, bd_ref[...])
    kk = kk / jnp.maximum(jnp.sqrt(ss), 1e-12)
    r_o[...] = r
    lw_o[...] = -jnp.exp(wraw)
    k_o[...] = k * (1.0 + (a - 1.0) * ka_ref[...])
    v_o[...] = v
    a_o[...] = -kk
    b_o[...] = kk * a
    g_o[...] = dot(tg_s, g2)


def _rwkv_chunk_kernel(r_ref, lw_ref, k_ref, v_ref, a_ref, b_ref, gate_ref, lnw_ref, lnb_ref, rk_ref, s0_ref,
                       o_ref, sout_ref, p_scr, rp_scr, o0_scr, m_scr, d_scr, g_scr, *, npair, cl, nch):
    hd = RW_HEAD
    pw = 2 * hd
    lane = _iota((1, pw), 1)
    head0 = lane < hd
    bdmask = (_iota((pw, pw), 0) // hd) == (_iota((pw, pw), 1) // hd)
    e_dup = ((_iota((hd, pw), 1) % hd) == _iota((hd, pw), 0)).astype(BF16)
    e_fold = ((_iota((pw, hd), 0) % hd) == _iota((pw, hd), 1)).astype(BF16)
    ones_bd = bdmask.astype(BF16)
    avg = (bdmask.astype(F32) * (1.0 / hd)).astype(BF16)
    for p in range(npair):
        p_scr[p] = jnp.where(bdmask, _xdot_r(s0_ref[0, p], e_dup), 0.0)
    s2 = 2 * cl
    rr = _iota((s2, s2), 0)
    cc = _iota((s2, s2), 1)
    same = (rr // cl) == (cc // cl)
    strict = same & ((rr % cl) > (cc % cl))
    incl = same & ((rr % cl) >= (cc % cl))
    eye_s = (rr == cc).astype(F32)
    rid = _iota((cl, 1), 0)
    zeros_s = jnp.zeros((s2, pw), F32)
    nround = int(math.log2(cl)) - 1

    def stack(x):
        return jnp.concatenate([jnp.where(head0, x, 0.0), jnp.where(head0, 0.0, x)], axis=0)

    ug = 4 if nch % 4 == 0 else 1
    cat = jnp.concatenate

    def tile(ref, c, p):
        return ref[pl.ds(pl.multiple_of(c * cl, cl), cl), p * pw:(p + 1) * pw]

    def phase_a(it, carry):
        ch = [(it * ug + j, p) for j in range(ug) for p in range(npair)]
        ld = lambda ref: [tile(ref, c, p) for c, p in ch]
        lw = ld(lw_ref)
        cs = lw
        sh = 1
        while sh < cl:
            cs = [x + jnp.where(rid >= sh, pltpu.roll(x, sh, 0), 0.0) for x in cs]
            sh *= 2
        gam = [jnp.exp(x) for x in cs]
        ginv = [jnp.exp(-x) for x in cs]
        ats = [stack(a * jnp.exp(x - l)) for a, x, l in zip(ld(a_ref), cs, lw)]
        rts = [stack(r * g) for r, g in zip(ld(r_ref), gam)]
        x1s = [cat([a, r], axis=0).astype(BF16) for a, r in zip(ats, rts)]
        x2s = [cat([stack(b * g), stack(k * g)], axis=0).astype(BF16)
               for b, k, g in zip(ld(b_ref), ld(k_ref), ginv)]
        gs = [_bdot_nt(x1, x2) for x1, x2 in zip(x1s, x2s)]
        nmat = [jnp.where(strict, g[:s2, :s2], 0.0) for g in gs]
        aak = [jnp.where(strict, g[:s2, s2:], 0.0) for g in gs]
        arb = [jnp.where(incl, g[s2:, :s2], 0.0) for g in gs]
        ark = [jnp.where(incl, g[s2:, s2:], 0.0) for g in gs]
        vs = [stack(v) for v in ld(v_ref)]
        akv = [_bdot(x, v) for x, v in zip(aak, vs)]
        rkv = [_bdot(x, v) for x, v in zip(ark, vs)]
        tinv = [eye_s + n for n in nmat]
        pk = nmat
        for _ in range(nround):
            pk = [_bdot(x, x) for x in pk]
            tinv = [t + _bdot(x, t) for x, t in zip(pk, tinv)]
        tz = [_bdot(t, cat([x, a], axis=1)) for t, x, a in zip(tinv, akv, ats)]
        u0s = [x[:, :pw] for x in tz]
        aps = [x[:, pw:] for x in tz]
        z = [_bdot(x, cat([a, u], axis=1)) for x, a, u in zip(arb, aps, u0s)]
        md = [_bdot_tn(cat([cat([a, u], axis=1), cat([zeros_s, v], axis=1)], axis=0), x2)
              for a, u, v, x2 in zip(aps, u0s, vs, x2s)]
        for i, (c, p) in enumerate(ch):
            rps = rts[i] + z[i][:, :pw]
            o0s = z[i][:, pw:] + rkv[i]
            rp_scr[c, p] = (rps[:cl] + rps[cl:]).astype(BF16)
            o0_scr[c, p] = o0s[:cl] + o0s[cl:]
            gl = gam[i][cl - 1:cl, :]
            m_scr[c, p] = (md[i][:pw] * gl).astype(BF16)
            d_scr[c, p] = md[i][pw:] * gl
            g_scr[c, p] = gl
        return carry

    lax.fori_loop(0, nch // ug, phase_a, 0)

    def phase_b(it, carry):
        cs = [it * ug + j for j in range(ug)]
        prs = range(npair)
        bonus = [[_bdot(tile(r_ref, c, p) * tile(k_ref, c, p) * rk_ref[:, p * pw:(p + 1) * pw], ones_bd)
                  * tile(v_ref, c, p) for p in prs] for c in cs]
        st = [p_scr[p] for p in prs]
        outs, mus, dlts, vrs = [], [], [], []
        for j in range(ug + 2):
            if j < ug:
                c = cs[j]
                new = [st[p] * g_scr[c, p] + _bdot(st[p], m_scr[c, p]) + d_scr[c, p] for p in prs]
                outs.append([_bdot_nt(rp_scr[c, p], st[p]) + o0_scr[c, p] for p in prs])
                st = new
            if 1 <= j <= ug:
                mus.append([_bdot(o, avg) for o in outs[j - 1]])
            if 2 <= j:
                dl = [o - m for o, m in zip(outs[j - 2], mus[j - 2])]
                dlts.append(dl)
                vrs.append([_bdot(d * d, avg) for d in dl])
        for p in prs:
            p_scr[p] = st[p]
        for j, c in enumerate(cs):
            for p in prs:
                sl = slice(p * pw, (p + 1) * pw)
                on = dlts[j][p] * lax.rsqrt(vrs[j][p] + RW_LN_EPS) * lnw_ref[:, sl] + lnb_ref[:, sl]
                o_ref[pl.ds(pl.multiple_of(c * cl, cl), cl), sl] = (
                    (on + bonus[j][p]) * tile(gate_ref, c, p)).astype(o_ref.dtype)
        return carry

    lax.fori_loop(0, nch // ug, phase_b, 0)
    for p in range(npair):
        sout_ref[0, p] = _xdot_r(jnp.where(bdmask, p_scr[p], 0.0), e_fold)


def _rwkv_chunk_call(r, lw, k, v, a, b, gate, ln_w, ln_b, r_k, s0, bsz, t, cl):
    m, dm = r.shape
    hd = RW_HEAD
    pw = 2 * hd
    npairs = dm // pw
    npair = 2
    nch = t // cl
    s0p = s0.reshape(bsz, npairs, pw, hd)
    kern = functools.partial(_rwkv_chunk_kernel, npair=npair, cl=cl, nch=nch)
    tspec = pl.BlockSpec((t, npair * pw), lambda bi, pi: (bi, pi))
    vspec = pl.BlockSpec((1, npair * pw), lambda bi, pi: (0, pi))
    sspec = pl.BlockSpec((1, npair, pw, hd), lambda bi, pi: (bi, pi, 0, 0))
    o, s = pl.pallas_call(
        kern,
        grid=(bsz, npairs // npair),
        in_specs=[tspec] * 7 + [vspec] * 3 + [sspec],
        out_specs=[tspec, sspec],
        out_shape=[jax.ShapeDtypeStruct((m, dm), BF16), jax.ShapeDtypeStruct((bsz, npairs, pw, hd), F32)],
        scratch_shapes=[pltpu.VMEM((npair, pw, pw), F32),
                        pltpu.VMEM((nch, npair, cl, pw), BF16), pltpu.VMEM((nch, npair, cl, pw), F32),
                        pltpu.VMEM((nch, npair, pw, pw), BF16), pltpu.VMEM((nch, npair, pw, pw), F32),
                        pltpu.VMEM((nch, npair, 1, pw), F32)],
        compiler_params=_cparams(("parallel", "parallel")),
        name="rwkv_chunk",
    )(r, lw, k, v, a, b, gate, _row(ln_w), _row(ln_b), _row(r_k), s0p)
    return o, s.reshape(bsz, dm // hd, hd, hd)


def _rwkv_step_kernel(r_ref, lw_ref, k_ref, v_ref, a_ref, b_ref, gate_ref, lnw_ref, lnb_ref, rk_ref, s0_ref,
                      o_ref, sout_ref, oacc, *, nh):
    rid = _iota((STEP_PAD, 1), 0)
    hg = 8

    def rows6(hi, mid, lo, order):
        terms = (hi.astype(F32), mid.astype(F32), lo.astype(F32))
        out = jnp.zeros((STEP_PAD, hi.shape[-1]), F32)
        for i, t in enumerate(order):
            out = jnp.where(rid == i, terms[t], out)
        return out.astype(BF16)

    def group(gi, carry):
        hs = [gi * hg + j for j in range(hg)]
        rows = lambda ref: [ref[0, pl.ds(h, 1), :] for h in hs]
        st = [s0_ref[0, h] for h in hs]
        sa = [jnp.sum(s * a, axis=-1, keepdims=True) for s, a in zip(st, rows(a_ref))]
        vx = [rows6(*_split3(v), (0, 0, 1, 0, 1, 2)) for v in rows(v_ref)]
        kx = [rows6(*_split3(k), (0, 1, 0, 2, 1, 0)) for k in rows(k_ref)]
        vk = [lax.dot_general(x, y, (((0,), (0,)), ((), ())), preferred_element_type=F32) for x, y in zip(vx, kx)]
        sn = [s * jnp.exp(w) + c * b + o for s, w, c, b, o in zip(st, rows(lw_ref), sa, rows(b_ref), vk)]
        for h, s in zip(hs, sn):
            sout_ref[0, h] = s
        os_ = [_bdot_nt(jnp.where(rid == 0, r, 0.0), s) for r, s in zip(rows(r_ref), sn)]
        for h, o in zip(hs, os_):
            oacc[pl.ds(h, 1), :] = o[0:1]
        return carry

    lax.fori_loop(0, nh // hg, group, 0)
    o = oacc[...]
    mu = jnp.mean(o, axis=-1, keepdims=True)
    dlt = o - mu
    var = jnp.mean(dlt * dlt, axis=-1, keepdims=True)
    on = dlt * lax.rsqrt(var + RW_LN_EPS) * lnw_ref[...] + lnb_ref[...]
    bonus = jnp.sum(r_ref[0] * k_ref[0] * rk_ref[...], axis=-1, keepdims=True) * v_ref[0]
    o_ref[0] = ((on + bonus) * gate_ref[0]).astype(o_ref.dtype)


def _rwkv_step_call(r, lw, k, v, a, b, gate, ln_w, ln_b, r_k, s0):
    bsz, dm = r.shape
    hd = RW_HEAD
    nh = dm // hd
    heads = lambda x: x.reshape(bsz, nh, hd)
    hspec = pl.BlockSpec((1, nh, hd), lambda bi: (bi, 0, 0))
    pspec = pl.BlockSpec((nh, hd), lambda bi: (0, 0))
    sspec = pl.BlockSpec((1, nh, hd, hd), lambda bi: (bi, 0, 0, 0))
    o, s = pl.pallas_call(
        functools.partial(_rwkv_step_kernel, nh=nh),
        grid=(bsz,),
        in_specs=[hspec] * 7 + [pspec] * 3 + [sspec],
        out_specs=[hspec, sspec],
        out_shape=[jax.ShapeDtypeStruct((bsz, nh, hd), BF16), jax.ShapeDtypeStruct(s0.shape, F32)],
        scratch_shapes=[pltpu.VMEM((nh, hd), F32)],
        compiler_params=_cparams(("parallel",)),
        name="rwkv_step",
    )(*[heads(x) for x in (r, lw, k, v, a, b, gate)], ln_w.reshape(nh, hd), ln_b.reshape(nh, hd),
      r_k.reshape(nh, hd), s0)
    return o.reshape(bsz, dm), s


def _outproj_body(row, full, col, rc, out, scr):
    out[0][...] = rc[0][...] + jnp.dot(row[0][...].astype(BF16), col[0][...], preferred_element_type=F32)


def _outproj_call(a, w, h, t, name):
    m, kdim = a.shape
    n = w.shape[1]
    (o,) = _proj_call(_outproj_body, m=m, n=n, tm=_tiles(m, t), tn=512, row_ins=[a], full_ins=[], col_ins=[w],
                      rc_ins=[h], out_dtypes=[F32], scratch=[], name=name)
    return o


def _rwkv_layer(h, bsz, t, shift, wkv, norm_w, wts):
    m, dm = h.shape
    tm = _tiles(m, t)
    step = t == 1
    if step:
        prev = (shift.astype(F32), pl.BlockSpec((tm, dm), lambda i, j: (i, 0)))
        seq_tiles = 1
    else:
        seq_tiles = t // tm
        prev = (h, pl.BlockSpec((SUBLANES, dm), lambda i, j: (jnp.maximum(i * (tm // SUBLANES) - 1, 0), 0)))
    tn = 256
    bd = (jnp.arange(tn)[:, None] // RW_HEAD == jnp.arange(tn)[None, :] // RW_HEAD).astype(BF16)
    body = functools.partial(_rwkv_proj_body, tm=tm, seq_tiles=seq_tiles, step=step)
    lora = wts['w1'].shape[1]
    glora = wts['g1'].shape[1]
    r, lw, k, v, a, b, g = _proj_call(
        body, m=m, n=dm, tm=tm, tn=tn, row_ins=[h, prev],
        full_ins=[_row(norm_w), wts['mix'], wts['w1'], wts['a1'], wts['g1'], bd],
        col_ins=[wts['w_r'], wts['w_k'], wts['w_v'], wts['w2'], wts['a2'], wts['g2'],
                 _row(wts['w0']), _row(wts['a0']), _row(wts['k_k']), _row(wts['k_a'])],
        rc_ins=[], out_dtypes=[F32] * 7,
        scratch=[pltpu.VMEM((tm, dm), BF16)] * 3 + [pltpu.VMEM((tm, lora), BF16)] * 2
        + [pltpu.VMEM((tm, glora), BF16)],
        name="rwkv_proj")
    if step:
        shift_new = _rmsnorm_call(h, norm_w)
        og, s_new = _rwkv_step_call(r, lw, k, v, a, b, g, wts['ln_w'], wts['ln_b'], wts['r_k'], wkv)
    else:
        shift_new = _rmsnorm_call(h.reshape(bsz, t, dm)[:, -1], norm_w)
        og, s_new = _rwkv_chunk_call(r, lw, k, v, a, b, g, wts['ln_w'], wts['ln_b'], wts['r_k'], wkv, bsz, t,
                                     RW_CHUNK)
    return _outproj_call(og, wts['w_o'], h, t, "rwkv_out"), shift_new, s_new


def _la_chunk_kernel(q_ref, k_ref, v_ref, g_ref, gate_ref, nw_ref, s0_ref, o_ref, sout_ref, s_scr,
                     *, hb, dk, dv, cl, nch):
    s_scr[...] = s0_ref[0]
    rowi = _iota((cl, cl), 0)
    coli = _iota((cl, cl), 1)
    tri = (rowi >= coli).astype(BF16)
    levels = [s for s in (32, 16, 8) if 2 * s <= cl]
    masks = {s: ((rowi // s) % 2 == 1) & ((coli // s) == (rowi // s) - 1) for s in levels}
    rid = _iota((cl, 1), 0)
    nw = nw_ref[...]

    def chunk(c, carry):
        r0 = pl.multiple_of(c * cl, cl)
        for h in range(hb):
            ks = slice(h * dk, (h + 1) * dk)
            vs = slice(h * dv, (h + 1) * dv)
            q = q_ref[pl.ds(r0, cl), ks]
            k = k_ref[pl.ds(r0, cl), ks]
            g = g_ref[pl.ds(r0, cl), ks]
            v = v_ref[pl.ds(r0, cl), vs]
            cum = _xdot_l(tri, g)
            st = s_scr[h]
            o = _bdot(q * jnp.exp(cum), st)
            if levels:
                att = jnp.zeros((cl, cl), F32)
                for s in levels:
                    ends = [jnp.broadcast_to(cum[(bk + 1) * s - 1:(bk + 1) * s, :], (s, dk)) for bk in range(cl // s)]
                    e_blk = jnp.concatenate(ends, axis=0)
                    p_blk = jnp.concatenate([jnp.zeros((s, dk), F32)] + ends[:-1], axis=0)
                    a_s = _bdot_nt(q * jnp.exp(cum - p_blk), k * jnp.exp(e_blk - cum))
                    att = att + jnp.where(masks[s], a_s, 0.0)
                o = o + _bdot(att, v)
            o = o + jnp.sum(q * k, axis=-1, keepdims=True) * v
            for d in range(1, SUBLANES):
                valid = (rid % SUBLANES) >= d
                e = jnp.exp(jnp.where(valid, cum - pltpu.roll(cum, d, 0), -1e30))
                w = jnp.sum(q * pltpu.roll(k, d, 0) * e, axis=-1, keepdims=True)
                o = o + w * pltpu.roll(v, d, 0)
            on = o * lax.rsqrt(jnp.mean(o * o, axis=-1, keepdims=True) + NORM_EPS) * nw
            o_ref[pl.ds(r0, cl), vs] = (on * gate_ref[pl.ds(r0, cl), vs]).astype(o_ref.dtype)
            last = cum[cl - 1:cl, :]
            upd = _bdot_tn(k * jnp.exp(last - cum), v)
            dec = jnp.exp(jnp.transpose(jnp.broadcast_to(last, (LANES, dk))))
            s_scr[h] = jnp.concatenate([st[:, j * LANES:(j + 1) * LANES] * dec for j in range(dv // LANES)],
                                       axis=1) + upd
        return carry

    lax.fori_loop(0, nch, chunk, 0)
    sout_ref[0] = s_scr[...]


def _la_chunk_call(q, k, v, g, gate, norm_w, s0, bsz, t, cl, hb, name):
    m = q.shape[0]
    _, nh, dk, dv = s0.shape
    kern = functools.partial(_la_chunk_kernel, hb=hb, dk=dk, dv=dv, cl=cl, nch=t // cl)
    kspec = pl.BlockSpec((t, hb * dk), lambda bi, hi: (bi, hi))
    vspec = pl.BlockSpec((t, hb * dv), lambda bi, hi: (bi, hi))
    sspec = pl.BlockSpec((1, hb, dk, dv), lambda bi, hi: (bi, hi, 0, 0))
    return pl.pallas_call(
        kern,
        grid=(bsz, nh // hb),
        in_specs=[kspec, kspec, vspec, kspec, vspec, pl.BlockSpec((1, dv), lambda bi, hi: (0, 0)), sspec],
        out_specs=[vspec, sspec],
        out_shape=[jax.ShapeDtypeStruct((m, nh * dv), BF16), jax.ShapeDtypeStruct(s0.shape, F32)],
        scratch_shapes=[pltpu.VMEM((hb, dk, dv), F32)],
        compiler_params=_cparams(("parallel", "parallel")),
        name=name,
    )(q, k, v, g, gate, _row(norm_w), s0)


def _la_step_kernel(q_ref, k_ref, v_ref, g_ref, gate_ref, nw_ref, s0_ref, o_ref, sout_ref, *, nh, dk, dv):
    first = _iota((STEP_PAD, 1), 0) == 0

    def pad(x):
        return jnp.where(first, x, 0.0)

    hg = min(nh, 4)

    def group(gi, carry):
        hs = [gi * hg + j for j in range(hg)]
        rows = lambda ref: [ref[0, pl.ds(h, 1), :] for h in hs]
        dec = [jnp.exp(jnp.transpose(jnp.broadcast_to(g, (LANES, dk)))) for g in rows(g_ref)]
        upd = [_bdot_tn(pad(k), pad(v)) for k, v in zip(rows(k_ref), rows(v_ref))]
        sn = [jnp.concatenate([s0_ref[0, h][:, j * LANES:(j + 1) * LANES] * d for j in range(dv // LANES)], axis=1)
              + u for h, d, u in zip(hs, dec, upd)]
        for h, s in zip(hs, sn):
            sout_ref[0, h] = s
        os_ = [_bdot(pad(q), s)[0:1] for q, s in zip(rows(q_ref), sn)]
        for h, o, gt in zip(hs, os_, rows(gate_ref)):
            on = o * lax.rsqrt(jnp.mean(o * o, axis=-1, keepdims=True) + NORM_EPS) * nw_ref[...]
            o_ref[0, pl.ds(h, 1), :] = (on * gt).astype(o_ref.dtype)
        return carry

    lax.fori_loop(0, nh // hg, group, 0)


def _la_step_call(q, k, v, g, gate, norm_w, s0, name):
    bsz, nh, dk, dv = s0.shape
    kh = lambda x: x.reshape(bsz, nh, dk)
    vh = lambda x: x.reshape(bsz, nh, dv)
    kspec = pl.BlockSpec((1, nh, dk), lambda bi: (bi, 0, 0))
    vspec = pl.BlockSpec((1, nh, dv), lambda bi: (bi, 0, 0))
    sspec = pl.BlockSpec((1, nh, dk, dv), lambda bi: (bi, 0, 0, 0))
    o, s = pl.pallas_call(
        functools.partial(_la_step_kernel, nh=nh, dk=dk, dv=dv),
        grid=(bsz,),
        in_specs=[kspec, kspec, vspec, kspec, vspec, pl.BlockSpec((1, dv), lambda bi: (0, 0)), sspec],
        out_specs=[vspec, sspec],
        out_shape=[jax.ShapeDtypeStruct((bsz, nh, dv), F32), jax.ShapeDtypeStruct(s0.shape, F32)],
        compiler_params=_cparams(("parallel",)),
        name=name,
    )(kh(q), kh(k), vh(v), kh(g), vh(gate), _row(norm_w), s0)
    return o.reshape(bsz, nh * dv), s


def _la_run(q, k, v, g, gate, norm_w, s0, bsz, t, hb, name):
    if t == 1:
        return _la_step_call(q, k, v, g, gate, norm_w, s0, name + "_step")
    return _la_chunk_call(q, k, v, g, gate, norm_w, s0, bsz, t, LA_CHUNK, hb, name)


def _norm_prologue(h_ref, nw_ref, xn_s):
    @pl.when(pl.program_id(1) == 0)
    def _():
        xn_s[...] = _rms(h_ref[...], nw_ref[...]).astype(BF16)


def _gla_qkg_body(row, full, col, rc, out, scr, *, scale):
    nw_ref, gk1_ref = full
    wq, wk, gk2, bgk = col
    xn_s, t1_s = scr

    @pl.when(pl.program_id(1) == 0)
    def _():
        xn = _rms(row[0][...], nw_ref[...]).astype(BF16)
        xn_s[...] = xn
        t1_s[...] = jnp.dot(xn, gk1_ref[...], preferred_element_type=F32).astype(BF16)

    xn = xn_s[...]
    out[0][...] = jnp.dot(xn, wq[...], preferred_element_type=F32) * scale
    out[1][...] = jnp.dot(xn, wk[...], preferred_element_type=F32)
    lg = jnp.dot(t1_s[...], gk2[...], preferred_element_type=F32) + bgk[...]
    out[2][...] = jax.nn.log_sigmoid(lg) / GLA_GATE_TEMP


def _vg_body(row, full, col, rc, out, scr):
    _norm_prologue(row[0], full[0], scr[0])
    xn = scr[0][...]
    out[0][...] = jnp.dot(xn, col[0][...], preferred_element_type=F32)
    out[1][...] = jax.nn.silu(jnp.dot(xn, col[1][...], preferred_element_type=F32))


def _gla_layer(h, bsz, t, s0, norm_w, wts):
    m, dm = h.shape
    tm = _tiles(m, t)
    nh = GLA_HEADS
    dk = wts['w_q'].shape[1] // nh
    rank = wts['gk1'].shape[1]
    q, k, lg = _proj_call(functools.partial(_gla_qkg_body, scale=dk ** -0.5), m=m, n=nh * dk, tm=tm, tn=512,
                          row_ins=[h], full_ins=[_row(norm_w), wts['gk1']],
                          col_ins=[wts['w_q'], wts['w_k'], wts['gk2'], _row(wts['b_gk'])], rc_ins=[],
                          out_dtypes=[F32] * 3,
                          scratch=[pltpu.VMEM((tm, dm), BF16), pltpu.VMEM((tm, rank), BF16)], name="gla_qkg")
    v, gate = _proj_call(_vg_body, m=m, n=wts['w_v'].shape[1], tm=tm, tn=512, row_ins=[h],
                         full_ins=[_row(norm_w)], col_ins=[wts['w_v'], wts['w_g']], rc_ins=[],
                         out_dtypes=[F32] * 2, scratch=[pltpu.VMEM((tm, dm), BF16)], name="gla_vg")
    og, s_new = _la_run(q, k, v, lg, gate, wts['norm'], s0, bsz, t, 1, "gla_chunk")
    return _outproj_call(og, wts['w_o'], h, t, "gla_out"), s_new


def _hgrn_proj_body(row, full, col, rc, out, scr, *, scale, layer_idx):
    wq, wf, wi, wg, lbp = col
    _norm_prologue(row[0], full[0], scr[0])
    xn = scr[0][...]
    lbx = lbp[...]
    ex = jnp.exp(lbx - jnp.max(lbx, axis=0, keepdims=True))
    lb = jnp.sum(ex[1:layer_idx + 1], axis=0, keepdims=True) / jnp.sum(ex, axis=0, keepdims=True)
    z = jnp.dot(xn, wf[...], preferred_element_type=F32)
    l1m = jnp.log1p(-lb)
    out[0][...] = jnp.dot(xn, wq[...], preferred_element_type=F32) * scale
    out[1][...] = jnp.exp(l1m + jax.nn.log_sigmoid(-z))
    out[2][...] = jnp.logaddexp(jnp.log(lb), l1m + jax.nn.log_sigmoid(z))
    out[3][...] = jnp.dot(xn, wi[...], preferred_element_type=F32)
    out[4][...] = jax.nn.silu(jnp.dot(xn, wg[...], preferred_element_type=F32))


def _hgrn_layer(h, bsz, t, s0, norm_w, wts, layer_idx):
    m, dm = h.shape
    tm = _tiles(m, t)
    body = functools.partial(_hgrn_proj_body, scale=HG_EXPAND ** -0.5, layer_idx=layer_idx)
    q, k, lf, v, gate = _proj_call(body, m=m, n=dm, tm=tm, tn=512, row_ins=[h], full_ins=[_row(norm_w)],
                                   col_ins=[wts['w_q'], wts['w_f'], wts['w_i'], wts['w_g'], wts['lb']],
                                   rc_ins=[], out_dtypes=[F32] * 5, scratch=[pltpu.VMEM((tm, dm), BF16)],
                                   name="hgrn_proj")
    og, s_new = _la_run(q, k, v, lf, gate, wts['norm'], s0, bsz, t, 2, "hgrn_chunk")
    return _outproj_call(og, wts['w_o'], h, t, "hgrn_out"), s_new


def _ffn_kernel(*refs, step, tm, tf, nf, seq_tiles, final_norm):
    h_ref, nw_ref, wup_ref, wgate_ref, wc_ref, bc_ref, wdown_ref = refs[:7]
    p = 7
    if step:
        p0_ref, p1_ref = refs[p:p + 2]; p += 2
    if final_norm:
        fnw_ref = refs[p]; p += 1
    out_ref, tail_ref = refs[p:p + 2]
    xn_s, carry_s = refs[p + 2:]
    i = pl.program_id(0)
    f = pl.program_id(1)

    @pl.when(f == 0)
    def _():
        hv = h_ref[...]
        xn_s[...] = _rms(hv, nw_ref[...]).astype(BF16)
        out_ref[...] = hv

    xn = xn_s[...]
    u = jnp.dot(xn, wup_ref[...], preferred_element_type=F32)
    gate = jnp.dot(xn, wgate_ref[...], preferred_element_type=F32)
    wc = wc_ref[...]
    if step:
        u2, u1 = p0_ref[...], p1_ref[...]
        tail_ref[...] = u
    else:
        @pl.when(i % seq_tiles == 0)
        def _():
            carry_s[f] = jnp.zeros((SUBLANES, tf), F32)

        prev = carry_s[f]
        rid = _iota((tm, 1), 0)
        u1 = jnp.where(rid == 0, prev[SUBLANES - 1:SUBLANES], pltpu.roll(u, 1, 0))
        u2 = jnp.where(rid == 0, prev[SUBLANES - 2:SUBLANES - 1],
                       jnp.where(rid == 1, prev[SUBLANES - 1:SUBLANES], pltpu.roll(u, 2, 0)))
        tail = u[tm - SUBLANES:tm]
        carry_s[f] = tail
        tail_ref[0] = tail
    c = u2 * wc[0:1] + u1 * wc[1:2] + u * wc[2:3] + bc_ref[...]
    act = (jax.nn.gelu(c) * gate).astype(BF16)
    out_ref[...] += jnp.dot(act, wdown_ref[...], preferred_element_type=F32)
    if final_norm:
        @pl.when(f == nf - 1)
        def _():
            out_ref[...] = _rms(out_ref[...], fnw_ref[...])


def _ffn_call(h, bsz, t, conv_prev, norm_w, w_up, w_gate, w_conv, b_conv, w_down, final_norm_w):
    m, dm = h.shape
    dff = w_up.shape[1]
    step = t == 1
    tm = _tiles(m, t)
    tf = 512
    nf = dff // tf
    seq_tiles = 1 if step else t // tm
    final_norm = final_norm_w is not None
    args = [h, _row(norm_w), w_up, w_gate, w_conv.astype(F32), _row(b_conv.astype(F32)), w_down]
    specs = [pl.BlockSpec((tm, dm), lambda i, f: (i, 0)), pl.BlockSpec((1, dm), lambda i, f: (0, 0)),
             pl.BlockSpec((dm, tf), lambda i, f: (0, f)), pl.BlockSpec((dm, tf), lambda i, f: (0, f)),
             pl.BlockSpec((CONV_W, tf), lambda i, f: (0, f)), pl.BlockSpec((1, tf), lambda i, f: (0, f)),
             pl.BlockSpec((tf, dm), lambda i, f: (f, 0))]
    if step:
        args += [conv_prev[:, 0].astype(F32), conv_prev[:, 1].astype(F32)]
        specs += [pl.BlockSpec((tm, tf), lambda i, f: (i, f))] * 2
        tail_shape = jax.ShapeDtypeStruct((m, dff), F32)
        tail_spec = pl.BlockSpec((tm, tf), lambda i, f: (i, f))
    else:
        tail_shape = jax.ShapeDtypeStruct((m // tm, SUBLANES, dff), F32)
        tail_spec = pl.BlockSpec((1, SUBLANES, tf), lambda i, f: (i, 0, f))
    if final_norm:
        args.append(_row(final_norm_w))
        specs.append(pl.BlockSpec((1, dm), lambda i, f: (0, 0)))
    kern = functools.partial(_ffn_kernel, step=step, tm=tm, tf=tf, nf=nf, seq_tiles=seq_tiles,
                             final_norm=final_norm)
    out, tail = pl.pallas_call(
        kern,
        grid=(m // tm, nf),
        in_specs=specs,
        out_specs=[pl.BlockSpec((tm, dm), lambda i, f: (i, 0)), tail_spec],
        out_shape=[jax.ShapeDtypeStruct((m, dm), F32), tail_shape],
        scratch_shapes=[pltpu.VMEM((tm, dm), BF16), pltpu.VMEM((nf, SUBLANES, tf), F32)],
        compiler_params=_cparams(("arbitrary", "arbitrary")),
        name="conv_ffn",
    )(*args)
    if step:
        conv_new = jnp.stack([conv_prev[:, 1].astype(F32), tail], axis=1)
    else:
        conv_new = tail.reshape(bsz, seq_tiles, SUBLANES, dff)[:, -1, SUBLANES - (CONV_W - 1):]
    return out, conv_new


def _trunk(x, s5_re, s5_im, rw_shift, rw_wkv, gla_s, hg_s, ffn_conv, p):
    bsz, t, dm = x.shape
    h = x.reshape(bsz * t, dm).astype(F32)
    depth = p['norm_mix'].shape[0]
    conv_new = []
    for i in range(depth):
        kind = i % 4
        nw = p['norm_mix'][i]
        if kind == 0:
            h, s5_re, s5_im = _s5_layer(h, bsz, t, s5_re, s5_im, nw, p['s5'], p['s5_c_re'], p['s5_c_im'],
                                        p['s5_d'], p['s5_w_glu'])
        elif kind == 1:
            h, rw_shift, rw_wkv = _rwkv_layer(h, bsz, t, rw_shift, rw_wkv, nw, p['rw'])
        elif kind == 2:
            h, gla_s = _gla_layer(h, bsz, t, gla_s, nw, p['gla'])
        else:
            h, hg_s = _hgrn_layer(h, bsz, t, hg_s, nw, p['hg'], i)
        fin = p['norm_final'] if i == depth - 1 else None
        h, c = _ffn_call(h, bsz, t, ffn_conv[i], p['norm_ffn'][i], p['ffn_w_up'][i], p['ffn_w_gate'][i],
                         p['ffn_w_conv'][i], p['ffn_b_conv'][i], p['ffn_w_down'][i], fin)
        conv_new.append(c)
    return h.reshape(bsz, t, dm), s5_re, s5_im, rw_shift, rw_wkv, gla_s, hg_s, jnp.stack(conv_new)


def _pad_cols(w, n):
    return jnp.pad(w, ((0, 0), (0, n - w.shape[1])))


def _pad_rows(w, n):
    return jnp.pad(w, ((0, n - w.shape[0]), (0, 0)))


def kernel(x_prompt, x_sample, state_s5_re, state_s5_im, state_rwkv_shift, state_rwkv_wkv, state_gla, state_hgrn, state_ffn_conv, norm_mix, norm_ffn, norm_final, s5_lambda_re, s5_lambda_im, s5_log_dt, s5_b_re, s5_b_im, s5_c_re, s5_c_im, s5_d, s5_w_glu, rw_mix, rw_w_r, rw_w_k, rw_w_v, rw_w_o, rw_w0, rw_w1, rw_w2, rw_a0, rw_a1, rw_a2, rw_g1, rw_g2, rw_k_k, rw_k_a, rw_r_k, rw_ln_w, rw_ln_b, gla_w_q, gla_w_k, gla_w_v, gla_w_gk1, gla_w_gk2, gla_b_gk, gla_w_g, gla_norm, gla_w_o, hg_w_q, hg_w_f, hg_w_i, hg_w_g, hg_lb, hg_norm, hg_w_o, ffn_w_up, ffn_w_gate, ffn_w_conv, ffn_b_conv, ffn_w_down):
    bf = lambda w: w.astype(BF16)
    f32 = lambda w: w.astype(F32)
    lora = LANES * pl.cdiv(rw_w1.shape[1], LANES)
    alora = LANES * pl.cdiv(rw_a1.shape[1], LANES)
    grank = LANES * pl.cdiv(gla_w_gk1.shape[1], LANES)
    p = {
        'norm_mix': f32(norm_mix), 'norm_ffn': f32(norm_ffn), 'norm_final': f32(norm_final),
        's5': _s5_params(s5_lambda_re, s5_lambda_im, s5_log_dt, s5_b_re, s5_b_im, s5_c_re, s5_c_im, s5_d),
        's5_c_re': s5_c_re, 's5_c_im': s5_c_im, 's5_d': s5_d, 's5_w_glu': bf(s5_w_glu),
        'rw': dict(mix=_pad_rows(f32(rw_mix), SUBLANES), w_r=bf(rw_w_r), w_k=bf(rw_w_k), w_v=bf(rw_w_v),
                   w_o=bf(rw_w_o), w0=f32(rw_w0), w1=bf(_pad_cols(rw_w1, lora)), w2=bf(_pad_rows(rw_w2, lora)),
                   a0=f32(rw_a0), a1=bf(_pad_cols(rw_a1, alora)), a2=bf(_pad_rows(rw_a2, alora)),
                   g1=bf(rw_g1), g2=bf(rw_g2), k_k=f32(rw_k_k), k_a=f32(rw_k_a), r_k=f32(rw_r_k).reshape(-1),
                   ln_w=f32(rw_ln_w), ln_b=f32(rw_ln_b)),
        'gla': dict(w_q=bf(gla_w_q), w_k=bf(gla_w_k), w_v=bf(gla_w_v), gk1=bf(_pad_cols(gla_w_gk1, grank)),
                    gk2=bf(_pad_rows(gla_w_gk2, grank)), b_gk=f32(gla_b_gk), w_g=bf(gla_w_g), norm=f32(gla_norm),
                    w_o=bf(gla_w_o)),
        'hg': dict(w_q=bf(hg_w_q), w_f=bf(hg_w_f), w_i=bf(hg_w_i), w_g=bf(hg_w_g), lb=f32(hg_lb),
                   norm=f32(hg_norm), w_o=bf(hg_w_o)),
        'ffn_w_up': bf(ffn_w_up), 'ffn_w_gate': bf(ffn_w_gate), 'ffn_w_conv': ffn_w_conv,
        'ffn_b_conv': ffn_b_conv, 'ffn_w_down': bf(ffn_w_down),
    }
    nb, _, dm = x_prompt.shape
    depth = norm_mix.shape[0]
    dff = ffn_w_up.shape[-1]
    z_s5 = jnp.zeros((nb,) + state_s5_re.shape[1:], F32)
    outs_p = _trunk(x_prompt, z_s5, z_s5, jnp.zeros((nb, dm), F32),
                    jnp.zeros((nb,) + state_rwkv_wkv.shape[1:], F32), jnp.zeros((nb,) + state_gla.shape[1:], F32),
                    jnp.zeros((nb,) + state_hgrn.shape[1:], F32), jnp.zeros((depth, nb, CONV_W - 1, dff), F32), p)
    outs_s = _trunk(x_sample, f32(state_s5_re), f32(state_s5_im), state_rwkv_shift, f32(state_rwkv_wkv),
                    f32(state_gla), f32(state_hgrn), state_ffn_conv, p)
    return (outs_p[0], outs_s[0]) + tuple(outs_p[1:]) + tuple(outs_s[1:])
```

```python
import functools
import math

import jax
import jax.numpy as jnp
from jax import lax
from jax.experimental import pallas as pl
from jax.experimental.pallas import tpu as pltpu

F32 = jnp.float32
BF16 = jnp.bfloat16

NORM_EPS = 1e-6
RW_LN_EPS = 64e-5
RW_HEAD = 64
S5_GROUP = 16
S5_STATE = 64
S5_CHUNK = 16
GLA_HEADS = 4
GLA_GATE_TEMP = 16.0
HG_EXPAND = 128
CONV_W = 3
LA_CHUNK = 64
RW_CHUNK = 64
STEP_PAD = 16
LANES = 128
SUBLANES = 8
VMEM_LIMIT = 56 * 1024 * 1024


def _cparams(sem):
    return pltpu.CompilerParams(dimension_semantics=sem, vmem_limit_bytes=VMEM_LIMIT)


def _bdot(a, b):
    return jnp.dot(a.astype(BF16), b.astype(BF16), preferred_element_type=F32)


def _bdot_nt(a, b):
    return lax.dot_general(a.astype(BF16), b.astype(BF16), (((1,), (1,)), ((), ())),
                           preferred_element_type=F32)


def _bdot_tn(a, b):
    return lax.dot_general(a.astype(BF16), b.astype(BF16), (((0,), (0,)), ((), ())),
                           preferred_element_type=F32)


def _split3(x):
    hi = x.astype(BF16)
    r = x - hi.astype(F32)
    mid = r.astype(BF16)
    lo = (r - mid.astype(F32)).astype(BF16)
    return hi, mid, lo


def _xdot_l(m, x):
    hi, mid, lo = _split3(x)
    d = lambda p: jnp.dot(m, p, preferred_element_type=F32)
    return d(hi) + d(mid) + d(lo)


def _xdot_r(x, m):
    hi, mid, lo = _split3(x)
    d = lambda p: jnp.dot(p, m, preferred_element_type=F32)
    return d(hi) + d(mid) + d(lo)


def _rms(x, g):
    y = x * lax.rsqrt(jnp.mean(x * x, axis=-1, keepdims=True) + NORM_EPS)
    return y * g


def _iota(shape, dim):
    return lax.broadcasted_iota(jnp.int32, shape, dim)


def _rmsnorm_kernel(x_ref, g_ref, o_ref):
    o_ref[...] = _rms(x_ref[...], g_ref[...])


def _rmsnorm_call(x, g):
    r, d = x.shape
    tr = 512 if r % 512 == 0 else r
    return pl.pallas_call(
        _rmsnorm_kernel,
        grid=(r // tr,),
        in_specs=[pl.BlockSpec((tr, d), lambda i: (i, 0)), pl.BlockSpec((1, d), lambda i: (0, 0))],
        out_specs=pl.BlockSpec((tr, d), lambda i: (i, 0)),
        out_shape=jax.ShapeDtypeStruct((r, d), F32),
        compiler_params=_cparams(("parallel",)),
        name="rmsnorm",
    )(x, g.reshape(1, d))


def _proj_call(body, *, m, n, tm, tn, row_ins, full_ins, col_ins, rc_ins, out_dtypes, scratch, name):
    args, specs = [], []
    for a in row_ins:
        if isinstance(a, tuple):
            args.append(a[0]); specs.append(a[1])
        else:
            args.append(a); specs.append(pl.BlockSpec((tm, a.shape[1]), lambda i, j: (i, 0)))
    for a in full_ins:
        args.append(a); specs.append(pl.BlockSpec(a.shape, lambda i, j, nd=a.ndim: (0,) * nd))
    for a in col_ins:
        args.append(a); specs.append(pl.BlockSpec((a.shape[0], tn), lambda i, j: (0, j)))
    for a in rc_ins:
        args.append(a); specs.append(pl.BlockSpec((tm, tn), lambda i, j: (i, j)))
    nr, nf, nc, nrc, no = len(row_ins), len(full_ins), len(col_ins), len(rc_ins), len(out_dtypes)

    def kernel(*refs):
        p = 0
        groups = []
        for cnt in (nr, nf, nc, nrc, no):
            groups.append(refs[p:p + cnt]); p += cnt
        body(*groups, refs[p:])

    outs = pl.pallas_call(
        kernel,
        grid=(m // tm, n // tn),
        in_specs=specs,
        out_specs=[pl.BlockSpec((tm, tn), lambda i, j: (i, j)) for _ in out_dtypes],
        out_shape=[jax.ShapeDtypeStruct((m, n), dt) for dt in out_dtypes],
        scratch_shapes=scratch,
        compiler_params=_cparams(("parallel", "arbitrary")),
        name=name,
    )(*args)
    return outs


def _row(v):
    return v.reshape(1, -1)


def _tiles(m, t):
    if t == 1:
        return 512 if m % 512 == 0 else m
    return min(512, t)


def _s5_params(lam_re, lam_im, log_dt, b_re, b_im, c_re, c_im, d):
    hp = lax.Precision.HIGHEST
    g, n = lam_re.shape
    p = b_re.shape[-1]
    cl = S5_CHUNK
    lr = jnp.minimum(lam_re.astype(F32), -1e-4)
    li = lam_im.astype(F32)
    dt = jnp.exp(log_dt.astype(F32))[:, None]
    mag = jnp.exp(lr * dt)
    ab_re = mag * jnp.cos(li * dt)
    ab_im = mag * jnp.sin(li * dt)
    den = lr * lr + li * li
    f_re = ((ab_re - 1.0) * lr + ab_im * li) / den
    f_im = (ab_im * lr - (ab_re - 1.0) * li) / den
    br, bi = b_re.astype(F32), b_im.astype(F32)
    bb_re = f_re[..., None] * br - f_im[..., None] * bi
    bb_im = f_re[..., None] * bi + f_im[..., None] * br
    cr, ci = c_re.astype(F32), c_im.astype(F32)

    pr, pi = [jnp.ones_like(ab_re)], [jnp.zeros_like(ab_im)]
    for _ in range(cl):
        pr.append(pr[-1] * ab_re - pi[-1] * ab_im)
        pi.append(pr[-2] * ab_im + pi[-1] * ab_re)
    pw_re = jnp.stack(pr)
    pw_im = jnp.stack(pi)
    ajb_re = pw_re[:cl, :, :, None] * bb_re - pw_im[:cl, :, :, None] * bb_im
    ajb_im = pw_re[:cl, :, :, None] * bb_im + pw_im[:cl, :, :, None] * bb_re
    kj = (jnp.einsum('gpn,jgnq->jgpq', cr, ajb_re, precision=hp)
          - jnp.einsum('gpn,jgnq->jgpq', ci, ajb_im, precision=hp))
    s_idx = jnp.arange(cl)[:, None]
    t_idx = jnp.arange(cl)[None, :]
    lag = t_idx - s_idx
    kst = kj[jnp.clip(lag, 0, cl - 1)]
    kst = jnp.where((lag >= 0)[:, :, None, None, None], kst, 0.0)
    mt = kst.transpose(2, 0, 4, 1, 3).reshape(g, cl * p, cl * p)
    w_re = ajb_re[::-1].transpose(1, 0, 3, 2).reshape(g, cl * p, n)
    w_im = ajb_im[::-1].transpose(1, 0, 3, 2).reshape(g, cl * p, n)
    wm = jnp.concatenate([w_re, w_im], axis=-1)
    ca_re = cr[None] * pw_re[1:, :, None, :] - ci[None] * pw_im[1:, :, None, :]
    ca_im = cr[None] * pw_im[1:, :, None, :] + ci[None] * pw_re[1:, :, None, :]
    v_re = ca_re.transpose(1, 3, 0, 2).reshape(g, n, cl * p)
    v_im = -ca_im.transpose(1, 3, 0, 2).reshape(g, n, cl * p)
    vm = jnp.concatenate([v_re, v_im], axis=1)
    ar, ai = pw_re[cl], pw_im[cl]
    pcs, qcs = [], []
    for _ in range(SUBLANES):
        pcs.append(jnp.concatenate([ar, ar], axis=-1))
        qcs.append(jnp.concatenate([-ai, ai], axis=-1))
        ar, ai = ar * ar - ai * ai, 2.0 * ar * ai
    pc = jnp.stack(pcs, axis=1)
    qc = jnp.stack(qcs, axis=1)
    dtile = jnp.tile(d.astype(F32).reshape(g, 1, p), (1, 1, cl))
    return dict(ab_re=ab_re, ab_im=ab_im, bb_re=bb_re, bb_im=bb_im, mt=mt.astype(BF16), wm=wm.astype(BF16),
                vm=vm.astype(BF16), pc=pc, qc=qc, dtile=dtile)


def _s5_seq_kernel(u_ref, mt_ref, w_ref, vm_ref, pc_ref, qc_ref, dt_ref, h0_ref, z_ref, hf_ref, *, gb, nc):
    half = S5_STATE
    rid = _iota((nc, 1), 0)
    nlev = int(math.log2(nc))
    for g in range(gb):
        u = u_ref[0, g]
        ub = u.astype(BF16)
        y_intra = jnp.dot(ub, mt_ref[g], preferred_element_type=F32)
        x = jnp.dot(ub, w_ref[g], preferred_element_type=F32)
        pc = pc_ref[g]
        qc = qc_ref[g]
        h0 = h0_ref[0, g]
        h0c = pc[0:1] * h0 + qc[0:1] * pltpu.roll(h0, half, 1)
        x = x + jnp.where(rid == 0, h0c, 0.0)
        for lv in range(nlev):
            sh = 1 << lv
            xs = jnp.where(rid >= sh, pltpu.roll(x, sh, 0), 0.0)
            x = x + pc[lv:lv + 1] * xs + qc[lv:lv + 1] * pltpu.roll(xs, half, 1)
        hprev = jnp.where(rid == 0, h0, pltpu.roll(x, 1, 0))
        y = y_intra + jnp.dot(hprev.astype(BF16), vm_ref[g], preferred_element_type=F32) + dt_ref[g] * u
        z_ref[0, g] = jax.nn.gelu(y)
        hf_ref[0, g] = x[nc - 1:nc, :]


def _s5_seq_call(u4, prm, h0):
    b, g, nc, w = u4.shape
    n2 = 2 * S5_STATE
    gb = 8
    kern = functools.partial(_s5_seq_kernel, gb=gb, nc=nc)
    gspec = lambda shape: pl.BlockSpec((gb,) + shape, lambda gi, bi: (gi, 0, 0))
    return pl.pallas_call(
        kern,
        grid=(g // gb, b),
        in_specs=[pl.BlockSpec((1, gb, nc, w), lambda gi, bi: (bi, gi, 0, 0)),
                  gspec((w, w)), gspec((w, n2)), gspec((n2, w)), gspec((SUBLANES, n2)), gspec((SUBLANES, n2)),
                  gspec((1, w)),
                  pl.BlockSpec((1, gb, 1, n2), lambda gi, bi: (bi, gi, 0, 0))],
        out_specs=[pl.BlockSpec((1, gb, nc, w), lambda gi, bi: (bi, gi, 0, 0)),
                   pl.BlockSpec((1, gb, 1, n2), lambda gi, bi: (bi, gi, 0, 0))],
        out_shape=[jax.ShapeDtypeStruct((b, g, nc, w), F32), jax.ShapeDtypeStruct((b, g, 1, n2), F32)],
        compiler_params=_cparams(("parallel", "arbitrary")),
        name="s5_seq",
    )(u4, prm['mt'], prm['wm'], prm['vm'], prm['pc'], prm['qc'], prm['dtile'], h0)


def _s5_step_kernel(u_ref, hr_ref, hi_ref, ar_ref, ai_ref, bdr_ref, bdi_ref, cdr_ref, cdi_ref, d_ref,
                    z_ref, hro_ref, hio_ref):
    u = u_ref[...]
    ar, ai = ar_ref[...], ai_ref[...]
    hr, hi = hr_ref[...], hi_ref[...]
    hr2 = ar * hr - ai * hi + _bdot(u, bdr_ref[0])
    hi2 = ar * hi + ai * hr + _bdot(u, bdi_ref[0])
    y = _bdot(hr2, cdr_ref[0]) - _bdot(hi2, cdi_ref[0]) + d_ref[...] * u
    z_ref[...] = jax.nn.gelu(y)
    hro_ref[...] = hr2
    hio_ref[...] = hi2


def _s5_step_call(u, h_re, h_im, prm, c_re, c_im, d):
    b, dm = u.shape
    g, n = prm['ab_re'].shape
    p = S5_GROUP
    gpb = LANES // p
    nb = g // gpb
    eye = jnp.eye(gpb, dtype=F32)
    bd = lambda bb: jnp.einsum('kgnp,gh->kgphn', bb.reshape(nb, gpb, n, p), eye).reshape(nb, gpb * p, gpb * n)
    cd = lambda cc: jnp.einsum('kgpn,gh->kgnhp', cc.astype(F32).reshape(nb, gpb, p, n), eye).reshape(
        nb, gpb * n, gpb * p)
    bdr, bdi = bd(prm['bb_re']).astype(BF16), bd(prm['bb_im']).astype(BF16)
    cdr, cdi = cd(c_re).astype(BF16), cd(c_im).astype(BF16)
    wn = gpb * n
    cspec = lambda wd: pl.BlockSpec((b, wd), lambda k: (0, k))
    rspec = lambda wd: pl.BlockSpec((1, wd), lambda k: (0, k))
    z, hr2, hi2 = pl.pallas_call(
        _s5_step_kernel,
        grid=(nb,),
        in_specs=[cspec(LANES), cspec(wn), cspec(wn), rspec(wn), rspec(wn),
                  pl.BlockSpec((1, LANES, wn), lambda k: (k, 0, 0)), pl.BlockSpec((1, LANES, wn), lambda k: (k, 0, 0)),
                  pl.BlockSpec((1, wn, LANES), lambda k: (k, 0, 0)), pl.BlockSpec((1, wn, LANES), lambda k: (k, 0, 0)),
                  rspec(LANES)],
        out_specs=[cspec(LANES), cspec(wn), cspec(wn)],
        out_shape=[jax.ShapeDtypeStruct((b, dm), F32), jax.ShapeDtypeStruct((b, g * n), F32),
                   jax.ShapeDtypeStruct((b, g * n), F32)],
        compiler_params=_cparams(("parallel",)),
        name="s5_step",
    )(u, h_re.reshape(b, g * n), h_im.reshape(b, g * n), prm['ab_re'].reshape(1, g * n),
      prm['ab_im'].reshape(1, g * n), bdr, bdi, cdr, cdi, _row(d.astype(F32)))
    return z, hr2.reshape(b, g, n), hi2.reshape(b, g, n)


def _glu_body(row, full, col, rc, out, scr):
    (z_ref,) = row
    (w_ref,) = col
    zt_ref, h_ref = rc
    (zb,) = scr

    @pl.when(pl.program_id(1) == 0)
    def _():
        zb[...] = z_ref[...].astype(BF16)

    zt = zt_ref[...]
    gate = jax.nn.sigmoid(jnp.dot(zb[...], w_ref[...], preferred_element_type=F32))
    out[0][...] = h_ref[...] + zt * gate


def _s5_layer(h, bsz, t, st_re, st_im, norm_w, prm, c_re, c_im, d, w_glu):
    m, dm = h.shape
    g, n, p = dm // S5_GROUP, S5_STATE, S5_GROUP
    xn = _rmsnorm_call(h, norm_w)
    if t == 1:
        z, s_re, s_im = _s5_step_call(xn, st_re, st_im, prm, c_re, c_im, d)
    else:
        cl = S5_CHUNK
        nc = t // cl
        u4 = xn.reshape(bsz, nc, cl, g, p).transpose(0, 3, 1, 2, 4).reshape(bsz, g, nc, cl * p)
        h0 = jnp.concatenate([st_re, st_im], axis=-1).reshape(bsz, g, 1, 2 * n)
        z4, hf = _s5_seq_call(u4, prm, h0)
        z = z4.reshape(bsz, g, nc, cl, p).transpose(0, 2, 3, 1, 4).reshape(m, dm)
        s_re, s_im = hf[:, :, 0, :n], hf[:, :, 0, n:]
    tm = _tiles(m, t)
    (h_new,) = _proj_call(_glu_body, m=m, n=dm, tm=tm, tn=512, row_ins=[z], full_ins=[], col_ins=[w_glu],
                          rc_ins=[z, h], out_dtypes=[F32], scratch=[pltpu.VMEM((tm, dm), BF16)], name="s5_glu")
    return h_new, s_re, s_im


def _rwkv_proj_body(row, full, col, rc, out, scr, *, tm, seq_tiles, step):
    h_ref, prev_ref = row
    nw_ref, mix_ref, w1_ref, a1_ref, g1_ref, bd_ref = full
    wr, wk, wv, w2, a2, g2, w0, a0, kk_ref, ka_ref = col
    r_o, lw_o, k_o, v_o, a_o, b_o, g_o = out
    xr_s, xk_s, xv_s, tw_s, ta_s, tg_s = scr
    i = pl.program_id(0)

    @pl.when(pl.program_id(1) == 0)
    def _():
        nw = nw_ref[...]
        xn = _rms(h_ref[...], nw)
        if step:
            xprev = prev_ref[...]
        else:
            last = _rms(prev_ref[...], nw)[SUBLANES - 1:SUBLANES, :]
            last = jnp.where(i % seq_tiles == 0, 0.0, last)
            xprev = jnp.where(_iota((tm, 1), 0) == 0, last, pltpu.roll(xn, 1, 0))
        xx = xprev - xn
        mix = mix_ref[...]
        xr_s[...] = (xn + xx * mix[0:1]).astype(BF16)
        xk_s[...] = (xn + xx * mix[2:3]).astype(BF16)
        xv_s[...] = (xn + xx * mix[3:4]).astype(BF16)
        tw_s[...] = jnp.tanh(_bdot(xn + xx * mix[1:2], w1_ref[...])).astype(BF16)
        ta_s[...] = _bdot(xn + xx * mix[4:5], a1_ref[...]).astype(BF16)
        tg_s[...] = jax.nn.sigmoid(_bdot(xn + xx * mix[5:6], g1_ref[...])).astype(BF16)

    dot = lambda a, b: jnp.dot(a[...], b[...], preferred_element_type=F32)
    r = dot(xr_s, wr)
    k = dot(xk_s, wk)
    v = dot(xv_s, wv)
    wraw = -jax.nn.softplus(-(w0[...] + dot(tw_s, w2))) - 0.5
    a = jax.nn.sigmoid(a0[...] + dot(ta_s, a2))
    kk = k * kk_ref[...]
    ss = _bdot(kk * kk, bd_ref[...])
    kk = kk / jnp.maximum(jnp.sqrt(ss), 1e-12)
    r_o[...] = r
    lw_o[...] = -jnp.exp(wraw)
    k_o[...] = k * (1.0 + (a - 1.0) * ka_ref[...])
    v_o[...] = v
    a_o[...] = -kk
    b_o[...] = kk * a
    g_o[...] = dot(tg_s, g2)


def _rwkv_chunk_kernel(r_ref, lw_ref, k_ref, v_ref, a_ref, b_ref, gate_ref, lnw_ref, lnb_ref, rk_ref, s0_ref,
                       o_ref, sout_ref, p_scr, rp_scr, o0_scr, m_scr, d_scr, g_scr, *, npair, cl, nch):
    hd = RW_HEAD
    pw = 2 * hd
    lane = _iota((1, pw), 1)
    head0 = lane < hd
    bdmask = (_iota((pw, pw), 0) // hd) == (_iota((pw, pw), 1) // hd)
    e_dup = ((_iota((hd, pw), 1) % hd) == _iota((hd, pw), 0)).astype(BF16)
    e_fold = ((_iota((pw, hd), 0) % hd) == _iota((pw, hd), 1)).astype(BF16)
    ones_bd = bdmask.astype(BF16)
    avg = (bdmask.astype(F32) * (1.0 / hd)).astype(BF16)
    for p in range(npair):
        p_scr[p] = jnp.where(bdmask, _xdot_r(s0_ref[0, p], e_dup), 0.0)
    s2 = 2 * cl
    rr = _iota((s2, s2), 0)
    cc = _iota((s2, s2), 1)
    same = (rr // cl) == (cc // cl)
    strict = same & ((rr % cl) > (cc % cl))
    incl = same & ((rr % cl) >= (cc % cl))
    eye_s = (rr == cc).astype(F32)
    rid = _iota((cl, 1), 0)
    zeros_s = jnp.zeros((s2, pw), F32)
    nround = int(math.log2(cl)) - 1

    def stack(x):
        return jnp.concatenate([jnp.where(head0, x, 0.0), jnp.where(head0, 0.0, x)], axis=0)

    ug = 4 if nch % 4 == 0 else 1
    cat = jnp.concatenate

    def tile(ref, c, p):
        return ref[pl.ds(pl.multiple_of(c * cl, cl), cl), p * pw:(p + 1) * pw]

    def phase_a(it, carry):
        ch = [(it * ug + j, p) for j in range(ug) for p in range(npair)]
        ld = lambda ref: [tile(ref, c, p) for c, p in ch]
        lw = ld(lw_ref)
        cs = lw
        sh = 1
        while sh < cl:
            cs = [x + jnp.where(rid >= sh, pltpu.roll(x, sh, 0), 0.0) for x in cs]
            sh *= 2
        gam = [jnp.exp(x) for x in cs]
        ginv = [jnp.exp(-x) for x in cs]
        ats = [stack(a * jnp.exp(x - l)) for a, x, l in zip(ld(a_ref), cs, lw)]
        rts = [stack(r * g) for r, g in zip(ld(r_ref), gam)]
        x1s = [cat([a, r], axis=0).astype(BF16) for a, r in zip(ats, rts)]
        x2s = [cat([stack(b * g), stack(k * g)], axis=0).astype(BF16)
               for b, k, g in zip(ld(b_ref), ld(k_ref), ginv)]
        gs = [_bdot_nt(x1, x2) for x1, x2 in zip(x1s, x2s)]
        nmat = [jnp.where(strict, g[:s2, :s2], 0.0) for g in gs]
        aak = [jnp.where(strict, g[:s2, s2:], 0.0) for g in gs]
        arb = [jnp.where(incl, g[s2:, :s2], 0.0) for g in gs]
        ark = [jnp.where(incl, g[s2:, s2:], 0.0) for g in gs]
        vs = [stack(v) for v in ld(v_ref)]
        akv = [_bdot(x, v) for x, v in zip(aak, vs)]
        rkv = [_bdot(x, v) for x, v in zip(ark, vs)]
        tinv = [eye_s + n for n in nmat]
        pk = nmat
        for _ in range(nround):
            pk = [_bdot(x, x) for x in pk]
            tinv = [t + _bdot(x, t) for x, t in zip(pk, tinv)]
        tz = [_bdot(t, cat([x, a], axis=1)) for t, x, a in zip(tinv, akv, ats)]
        u0s = [x[:, :pw] for x in tz]
        aps = [x[:, pw:] for x in tz]
        z = [_bdot(x, cat([a, u], axis=1)) for x, a, u in zip(arb, aps, u0s)]
        md = [_bdot_tn(cat([cat([a, u], axis=1), cat([zeros_s, v], axis=1)], axis=0), x2)
              for a, u, v, x2 in zip(aps, u0s, vs, x2s)]
        for i, (c, p) in enumerate(ch):
            rps = rts[i] + z[i][:, :pw]
            o0s = z[i][:, pw:] + rkv[i]
            rp_scr[c, p] = (rps[:cl] + rps[cl:]).astype(BF16)
            o0_scr[c, p] = o0s[:cl] + o0s[cl:]
            gl = gam[i][cl - 1:cl, :]
            m_scr[c, p] = (md[i][:pw] * gl).astype(BF16)
            d_scr[c, p] = md[i][pw:] * gl
            g_scr[c, p] = gl
        return carry

    lax.fori_loop(0, nch // ug, phase_a, 0)

    def phase_b(it, carry):
        cs = [it * ug + j for j in range(ug)]
        prs = range(npair)
        bonus = [[_bdot(tile(r_ref, c, p) * tile(k_ref, c, p) * rk_ref[:, p * pw:(p + 1) * pw], ones_bd)
                  * tile(v_ref, c, p) for p in prs] for c in cs]
        st = [p_scr[p] for p in prs]
        outs, mus, dlts, vrs = [], [], [], []
        for j in range(ug + 2):
            if j < ug:
                c = cs[j]
                new = [st[p] * g_scr[c, p] + _bdot(st[p], m_scr[c, p]) + d_scr[c, p] for p in prs]
                outs.append([_bdot_nt(rp_scr[c, p], st[p]) + o0_scr[c, p] for p in prs])
                st = new
            if 1 <= j <= ug:
                mus.append([_bdot(o, avg) for o in outs[j - 1]])
            if 2 <= j:
                dl = [o - m for o, m in zip(outs[j - 2], mus[j - 2])]
                dlts.append(dl)
                vrs.append([_bdot(d * d, avg) for d in dl])
        for p in prs:
            p_scr[p] = st[p]
        for j, c in enumerate(cs):
            for p in prs:
                sl = slice(p * pw, (p + 1) * pw)
                on = dlts[j][p] * lax.rsqrt(vrs[j][p] + RW_LN_EPS) * lnw_ref[:, sl] + lnb_ref[:, sl]
                o_ref[pl.ds(pl.multiple_of(c * cl, cl), cl), sl] = (
                    (on + bonus[j][p]) * tile(gate_ref, c, p)).astype(o_ref.dtype)
        return carry

    lax.fori_loop(0, nch // ug, phase_b, 0)
    for p in range(npair):
        sout_ref[0, p] = _xdot_r(jnp.where(bdmask, p_scr[p], 0.0), e_fold)


def _rwkv_chunk_call(r, lw, k, v, a, b, gate, ln_w, ln_b, r_k, s0, bsz, t, cl):
    m, dm = r.shape
    hd = RW_HEAD
    pw = 2 * hd
    npairs = dm // pw
    npair = 2
    nch = t // cl
    s0p = s0.reshape(bsz, npairs, pw, hd)
    kern = functools.partial(_rwkv_chunk_kernel, npair=npair, cl=cl, nch=nch)
    tspec = pl.BlockSpec((t, npair * pw), lambda bi, pi: (bi, pi))
    vspec = pl.BlockSpec((1, npair * pw), lambda bi, pi: (0, pi))
    sspec = pl.BlockSpec((1, npair, pw, hd), lambda bi, pi: (bi, pi, 0, 0))
    o, s = pl.pallas_call(
        kern,
        grid=(bsz, npairs // npair),
        in_specs=[tspec] * 7 + [vspec] * 3 + [sspec],
        out_specs=[tspec, sspec],
        out_shape=[jax.ShapeDtypeStruct((m, dm), BF16), jax.ShapeDtypeStruct((bsz, npairs, pw, hd), F32)],
        scratch_shapes=[pltpu.VMEM((npair, pw, pw), F32),
                        pltpu.VMEM((nch, npair, cl, pw), BF16), pltpu.VMEM((nch, npair, cl, pw), F32),
                        pltpu.VMEM((nch, npair, pw, pw), BF16), pltpu.VMEM((nch, npair, pw, pw), F32),
                        pltpu.VMEM((nch, npair, 1, pw), F32)],
        compiler_params=_cparams(("parallel", "parallel")),
        name="rwkv_chunk",
    )(r, lw, k, v, a, b, gate, _row(ln_w), _row(ln_b), _row(r_k), s0p)
    return o, s.reshape(bsz, dm // hd, hd, hd)


def _rwkv_step_kernel(r_ref, lw_ref, k_ref, v_ref, a_ref, b_ref, gate_ref, lnw_ref, lnb_ref, rk_ref, s0_ref,
                      o_ref, sout_ref, oacc, *, nh):
    rid = _iota((STEP_PAD, 1), 0)
    hg = 8

    def rows6(hi, mid, lo, order):
        terms = (hi.astype(F32), mid.astype(F32), lo.astype(F32))
        out = jnp.zeros((STEP_PAD, hi.shape[-1]), F32)
        for i, t in enumerate(order):
            out = jnp.where(rid == i, terms[t], out)
        return out.astype(BF16)

    def group(gi, carry):
        hs = [gi * hg + j for j in range(hg)]
        rows = lambda ref: [ref[0, pl.ds(h, 1), :] for h in hs]
        st = [s0_ref[0, h] for h in hs]
        sa = [jnp.sum(s * a, axis=-1, keepdims=True) for s, a in zip(st, rows(a_ref))]
        vx = [rows6(*_split3(v), (0, 0, 1, 0, 1, 2)) for v in rows(v_ref)]
        kx = [rows6(*_split3(k), (0, 1, 0, 2, 1, 0)) for k in rows(k_ref)]
        vk = [lax.dot_general(x, y, (((0,), (0,)), ((), ())), preferred_element_type=F32) for x, y in zip(vx, kx)]
        sn = [s * jnp.exp(w) + c * b + o for s, w, c, b, o in zip(st, rows(lw_ref), sa, rows(b_ref), vk)]
        for h, s in zip(hs, sn):
            sout_ref[0, h] = s
        os_ = [_bdot_nt(jnp.where(rid == 0, r, 0.0), s) for r, s in zip(rows(r_ref), sn)]
        for h, o in zip(hs, os_):
            oacc[pl.ds(h, 1), :] = o[0:1]
        return carry

    lax.fori_loop(0, nh // hg, group, 0)
    o = oacc[...]
    mu = jnp.mean(o, axis=-1, keepdims=True)
    dlt = o - mu
    var = jnp.mean(dlt * dlt, axis=-1, keepdims=True)
    on = dlt * lax.rsqrt(var + RW_LN_EPS) * lnw_ref[...] + lnb_ref[...]
    bonus = jnp.sum(r_ref[0] * k_ref[0] * rk_ref[...], axis=-1, keepdims=True) * v_ref[0]
    o_ref[0] = ((on + bonus) * gate_ref[0]).astype(o_ref.dtype)


def _rwkv_step_call(r, lw, k, v, a, b, gate, ln_w, ln_b, r_k, s0):
    bsz, dm = r.shape
    hd = RW_HEAD
    nh = dm // hd
    heads = lambda x: x.reshape(bsz, nh, hd)
    hspec = pl.BlockSpec((1, nh, hd), lambda bi: (bi, 0, 0))
    pspec = pl.BlockSpec((nh, hd), lambda bi: (0, 0))
    sspec = pl.BlockSpec((1, nh, hd, hd), lambda bi: (bi, 0, 0, 0))
    o, s = pl.pallas_call(
        functools.partial(_rwkv_step_kernel, nh=nh),
        grid=(bsz,),
        in_specs=[hspec] * 7 + [pspec] * 3 + [sspec],
        out_specs=[hspec, sspec],
        out_shape=[jax.ShapeDtypeStruct((bsz, nh, hd), BF16), jax.ShapeDtypeStruct(s0.shape, F32)],
        scratch_shapes=[pltpu.VMEM((nh, hd), F32)],
        compiler_params=_cparams(("parallel",)),
        name="rwkv_step",
    )(*[heads(x) for x in (r, lw, k, v, a, b, gate)], ln_w.reshape(nh, hd), ln_b.reshape(nh, hd),
      r_k.reshape(nh, hd), s0)
    return o.reshape(bsz, dm), s


def _outproj_body(row, full, col, rc, out, scr):
    out[0][...] = rc[0][...] + jnp.dot(row[0][...].astype(BF16), col[0][...], preferred_element_type=F32)


def _outproj_call(a, w, h, t, name):
    m, kdim = a.shape
    n = w.shape[1]
    (o,) = _proj_call(_outproj_body, m=m, n=n, tm=_tiles(m, t), tn=512, row_ins=[a], full_ins=[], col_ins=[w],
                      rc_ins=[h], out_dtypes=[F32], scratch=[], name=name)
    return o


def _rwkv_layer(h, bsz, t, shift, wkv, norm_w, wts):
    m, dm = h.shape
    tm = _tiles(m, t)
    step = t == 1
    if step:
        prev = (shift.astype(F32), pl.BlockSpec((tm, dm), lambda i, j: (i, 0)))
        seq_tiles = 1
    else:
        seq_tiles = t // tm
        prev = (h, pl.BlockSpec((SUBLANES, dm), lambda i, j: (jnp.maximum(i * (tm // SUBLANES) - 1, 0), 0)))
    tn = 256
    bd = (jnp.arange(tn)[:, None] // RW_HEAD == jnp.arange(tn)[None, :] // RW_HEAD).astype(BF16)
    body = functools.partial(_rwkv_proj_body, tm=tm, seq_tiles=seq_tiles, step=step)
    lora = wts['w1'].shape[1]
    glora = wts['g1'].shape[1]
    r, lw, k, v, a, b, g = _proj_call(
        body, m=m, n=dm, tm=tm, tn=tn, row_ins=[h, prev],
        full_ins=[_row(norm_w), wts['mix'], wts['w1'], wts['a1'], wts['g1'], bd],
        col_ins=[wts['w_r'], wts['w_k'], wts['w_v'], wts['w2'], wts['a2'], wts['g2'],
                 _row(wts['w0']), _row(wts['a0']), _row(wts['k_k']), _row(wts['k_a'])],
        rc_ins=[], out_dtypes=[F32] * 7,
        scratch=[pltpu.VMEM((tm, dm), BF16)] * 3 + [pltpu.VMEM((tm, lora), BF16)] * 2
        + [pltpu.VMEM((tm, glora), BF16)],
        name="rwkv_proj")
    if step:
        shift_new = _rmsnorm_call(h, norm_w)
        og, s_new = _rwkv_step_call(r, lw, k, v, a, b, g, wts['ln_w'], wts['ln_b'], wts['r_k'], wkv)
    else:
        shift_new = _rmsnorm_call(h.reshape(bsz, t, dm)[:, -1], norm_w)
        og, s_new = _rwkv_chunk_call(r, lw, k, v, a, b, g, wts['ln_w'], wts['ln_b'], wts['r_k'], wkv, bsz, t,
                                     RW_CHUNK)
    return _outproj_call(og, wts['w_o'], h, t, "rwkv_out"), shift_new, s_new


def _la_chunk_kernel(q_ref, k_ref, v_ref, g_ref, gate_ref, nw_ref, s0_ref, o_ref, sout_ref, s_scr,
                     *, hb, dk, dv, cl, nch):
    s_scr[...] = s0_ref[0]
    rowi = _iota((cl, cl), 0)
    coli = _iota((cl, cl), 1)
    tri = (rowi >= coli).astype(BF16)
    levels = [1 << i for i in range(int(math.log2(cl)))]
    masks = {s: ((rowi // s) % 2 == 1) & ((coli // s) == (rowi // s) - 1) for s in levels}
    rid = _iota((cl, 1), 0)
    nw = nw_ref[...]
    ug = max(1, 4 // hb)
    if nch % ug:
        ug = 1

    def rows(c):
        return pl.ds(pl.multiple_of(c * cl, cl), cl)

    def group(it, carry):
        ch = [(it * ug + j, h) for j in range(ug) for h in range(hb)]
        ldk = lambda ref: [ref[rows(c), h * dk:(h + 1) * dk] for c, h in ch]
        ldv = lambda ref: [ref[rows(c), h * dv:(h + 1) * dv] for c, h in ch]
        q, k, v = ldk(q_ref), ldk(k_ref), ldv(v_ref)
        cum = [_xdot_l(tri, g) for g in ldk(g_ref)]
        att = [jnp.zeros((cl, cl), F32) for _ in ch]
        ends = cum
        for s in levels:
            prev = [jnp.where(rid >= s, pltpu.roll(e, s, 0), 0.0) for e in ends]
            a_s = [_bdot_nt(qq * jnp.exp(c - p), kk * jnp.exp(e - c))
                   for qq, kk, c, p, e in zip(q, k, cum, prev, ends)]
            att = [a + jnp.where(masks[s], x, 0.0) for a, x in zip(att, a_s)]
            if 2 * s < cl:
                ends = [jnp.where((rid % (2 * s)) >= s, e, pltpu.roll(e, cl - s, 0)) for e in ends]
        o_in = [_bdot(a, vv) + jnp.sum(qq * kk, axis=-1, keepdims=True) * vv
                for a, vv, qq, kk in zip(att, v, q, k)]
        last = [c[cl - 1:cl, :] for c in cum]
        upd = [_bdot_tn(kk * jnp.exp(l - c), vv) for kk, l, c, vv in zip(k, last, cum, v)]
        dec = [jnp.exp(jnp.transpose(jnp.broadcast_to(l, (LANES, dk)))) for l in last]
        qe = [qq * jnp.exp(c) for qq, c in zip(q, cum)]
        st = [s_scr[h] for h in range(hb)]
        for i, (c, h) in enumerate(ch):
            o = o_in[i] + _bdot(qe[i], st[h])
            st[h] = jnp.concatenate([st[h][:, j * LANES:(j + 1) * LANES] * dec[i] for j in range(dv // LANES)],
                                    axis=1) + upd[i]
            on = o * lax.rsqrt(jnp.mean(o * o, axis=-1, keepdims=True) + NORM_EPS) * nw
            vs = slice(h * dv, (h + 1) * dv)
            o_ref[rows(c), vs] = (on * gate_ref[rows(c), vs]).astype(o_ref.dtype)
        for h in range(hb):
            s_scr[h] = st[h]
        return carry

    lax.fori_loop(0, nch // ug, group, 0)
    sout_ref[0] = s_scr[...]


def _la_chunk_call(q, k, v, g, gate, norm_w, s0, bsz, t, cl, hb, name):
    m = q.shape[0]
    _, nh, dk, dv = s0.shape
    kern = functools.partial(_la_chunk_kernel, hb=hb, dk=dk, dv=dv, cl=cl, nch=t // cl)
    kspec = pl.BlockSpec((t, hb * dk), lambda bi, hi: (bi, hi))
    vspec = pl.BlockSpec((t, hb * dv), lambda bi, hi: (bi, hi))
    sspec = pl.BlockSpec((1, hb, dk, dv), lambda bi, hi: (bi, hi, 0, 0))
    return pl.pallas_call(
        kern,
        grid=(bsz, nh // hb),
        in_specs=[kspec, kspec, vspec, kspec, vspec, pl.BlockSpec((1, dv), lambda bi, hi: (0, 0)), sspec],
        out_specs=[vspec, sspec],
        out_shape=[jax.ShapeDtypeStruct((m, nh * dv), BF16), jax.ShapeDtypeStruct(s0.shape, F32)],
        scratch_shapes=[pltpu.VMEM((hb, dk, dv), F32)],
        compiler_params=_cparams(("parallel", "parallel")),
        name=name,
    )(q, k, v, g, gate, _row(norm_w), s0)


def _la_step_kernel(q_ref, k_ref, v_ref, g_ref, gate_ref, nw_ref, s0_ref, o_ref, sout_ref, *, nh, dk, dv):
    first = _iota((STEP_PAD, 1), 0) == 0

    def pad(x):
        return jnp.where(first, x, 0.0)

    hg = min(nh, 4)

    def group(gi, carry):
        hs = [gi * hg + j for j in range(hg)]
        rows = lambda ref: [ref[0, pl.ds(h, 1), :] for h in hs]
        dec = [jnp.exp(jnp.transpose(jnp.broadcast_to(g, (LANES, dk)))) for g in rows(g_ref)]
        upd = [_bdot_tn(pad(k), pad(v)) for k, v in zip(rows(k_ref), rows(v_ref))]
        sn = [jnp.concatenate([s0_ref[0, h][:, j * LANES:(j + 1) * LANES] * d for j in range(dv // LANES)], axis=1)
              + u for h, d, u in zip(hs, dec, upd)]
        for h, s in zip(hs, sn):
            sout_ref[0, h] = s
        os_ = [_bdot(pad(q), s)[0:1] for q, s in zip(rows(q_ref), sn)]
        for h, o, gt in zip(hs, os_, rows(gate_ref)):
            on = o * lax.rsqrt(jnp.mean(o * o, axis=-1, keepdims=True) + NORM_EPS) * nw_ref[...]
            o_ref[0, pl.ds(h, 1), :] = (on * gt).astype(o_ref.dtype)
        return carry

    lax.fori_loop(0, nh // hg, group, 0)


def _la_step_call(q, k, v, g, gate, norm_w, s0, name):
    bsz, nh, dk, dv = s0.shape
    kh = lambda x: x.reshape(bsz, nh, dk)
    vh = lambda x: x.reshape(bsz, nh, dv)
    kspec = pl.BlockSpec((1, nh, dk), lambda bi: (bi, 0, 0))
    vspec = pl.BlockSpec((1, nh, dv), lambda bi: (bi, 0, 0))
    sspec = pl.BlockSpec((1, nh, dk, dv), lambda bi: (bi, 0, 0, 0))
    o, s = pl.pallas_call(
        functools.partial(_la_step_kernel, nh=nh, dk=dk, dv=dv),
        grid=(bsz,),
        in_specs=[kspec, kspec, vspec, kspec, vspec, pl.BlockSpec((1, dv), lambda bi: (0, 0)), sspec],
        out_specs=[vspec, sspec],
        out_shape=[jax.ShapeDtypeStruct((bsz, nh, dv), F32), jax.ShapeDtypeStruct(s0.shape, F32)],
        compiler_params=_cparams(("parallel",)),
        name=name,
    )(kh(q), kh(k), vh(v), kh(g), vh(gate), _row(norm_w), s0)
    return o.reshape(bsz, nh * dv), s


def _la_run(q, k, v, g, gate, norm_w, s0, bsz, t, hb, name):
    if t == 1:
        return _la_step_call(q, k, v, g, gate, norm_w, s0, name + "_step")
    return _la_chunk_call(q, k, v, g, gate, norm_w, s0, bsz, t, LA_CHUNK, hb, name)


def _norm_prologue(h_ref, nw_ref, xn_s):
    @pl.when(pl.program_id(1) == 0)
    def _():
        xn_s[...] = _rms(h_ref[...], nw_ref[...]).astype(BF16)


def _gla_qkg_body(row, full, col, rc, out, scr, *, scale):
    nw_ref, gk1_ref = full
    wq, wk, gk2, bgk = col
    xn_s, t1_s = scr

    @pl.when(pl.program_id(1) == 0)
    def _():
        xn = _rms(row[0][...], nw_ref[...]).astype(BF16)
        xn_s[...] = xn
        t1_s[...] = jnp.dot(xn, gk1_ref[...], preferred_element_type=F32).astype(BF16)

    xn = xn_s[...]
    out[0][...] = jnp.dot(xn, wq[...], preferred_element_type=F32) * scale
    out[1][...] = jnp.dot(xn, wk[...], preferred_element_type=F32)
    lg = jnp.dot(t1_s[...], gk2[...], preferred_element_type=F32) + bgk[...]
    out[2][...] = jax.nn.log_sigmoid(lg) / GLA_GATE_TEMP


def _vg_body(row, full, col, rc, out, scr):
    _norm_prologue(row[0], full[0], scr[0])
    xn = scr[0][...]
    out[0][...] = jnp.dot(xn, col[0][...], preferred_element_type=F32)
    out[1][...] = jax.nn.silu(jnp.dot(xn, col[1][...], preferred_element_type=F32))


def _gla_layer(h, bsz, t, s0, norm_w, wts):
    m, dm = h.shape
    tm = _tiles(m, t)
    nh = GLA_HEADS
    dk = wts['w_q'].shape[1] // nh
    rank = wts['gk1'].shape[1]
    q, k, lg = _proj_call(functools.partial(_gla_qkg_body, scale=dk ** -0.5), m=m, n=nh * dk, tm=tm, tn=512,
                          row_ins=[h], full_ins=[_row(norm_w), wts['gk1']],
                          col_ins=[wts['w_q'], wts['w_k'], wts['gk2'], _row(wts['b_gk'])], rc_ins=[],
                          out_dtypes=[F32] * 3,
                          scratch=[pltpu.VMEM((tm, dm), BF16), pltpu.VMEM((tm, rank), BF16)], name="gla_qkg")
    v, gate = _proj_call(_vg_body, m=m, n=wts['w_v'].shape[1], tm=tm, tn=512, row_ins=[h],
                         full_ins=[_row(norm_w)], col_ins=[wts['w_v'], wts['w_g']], rc_ins=[],
                         out_dtypes=[F32] * 2, scratch=[pltpu.VMEM((tm, dm), BF16)], name="gla_vg")
    og, s_new = _la_run(q, k, v, lg, gate, wts['norm'], s0, bsz, t, 1, "gla_chunk")
    return _outproj_call(og, wts['w_o'], h, t, "gla_out"), s_new


def _hgrn_proj_body(row, full, col, rc, out, scr, *, scale, layer_idx):
    wq, wf, wi, wg, lbp = col
    _norm_prologue(row[0], full[0], scr[0])
    xn = scr[0][...]
    lbx = lbp[...]
    ex = jnp.exp(lbx - jnp.max(lbx, axis=0, keepdims=True))
    lb = jnp.sum(ex[1:layer_idx + 1], axis=0, keepdims=True) / jnp.sum(ex, axis=0, keepdims=True)
    z = jnp.dot(xn, wf[...], preferred_element_type=F32)
    l1m = jnp.log1p(-lb)
    out[0][...] = jnp.dot(xn, wq[...], preferred_element_type=F32) * scale
    out[1][...] = jnp.exp(l1m + jax.nn.log_sigmoid(-z))
    out[2][...] = jnp.logaddexp(jnp.log(lb), l1m + jax.nn.log_sigmoid(z))
    out[3][...] = jnp.dot(xn, wi[...], preferred_element_type=F32)
    out[4][...] = jax.nn.silu(jnp.dot(xn, wg[...], preferred_element_type=F32))


def _hgrn_layer(h, bsz, t, s0, norm_w, wts, layer_idx):
    m, dm = h.shape
    tm = _tiles(m, t)
    body = functools.partial(_hgrn_proj_body, scale=HG_EXPAND ** -0.5, layer_idx=layer_idx)
    q, k, lf, v, gate = _proj_call(body, m=m, n=dm, tm=tm, tn=512, row_ins=[h], full_ins=[_row(norm_w)],
                                   col_ins=[wts['w_q'], wts['w_f'], wts['w_i'], wts['w_g'], wts['lb']],
                                   rc_ins=[], out_dtypes=[F32] * 5, scratch=[pltpu.VMEM((tm, dm), BF16)],
                                   name="hgrn_proj")
    og, s_new = _la_run(q, k, v, lf, gate, wts['norm'], s0, bsz, t, 2, "hgrn_chunk")
    return _outproj_call(og, wts['w_o'], h, t, "hgrn_out"), s_new


def _ffn_kernel(*refs, step, tm, tf, nf, seq_tiles, final_norm):
    h_ref, nw_ref, wup_ref, wgate_ref, wc_ref, bc_ref, wdown_ref = refs[:7]
    p = 7
    if step:
        p0_ref, p1_ref = refs[p:p + 2]; p += 2
    if final_norm:
        fnw_ref = refs[p]; p += 1
    out_ref, tail_ref = refs[p:p + 2]
    xn_s, carry_s = refs[p + 2:]
    i = pl.program_id(0)
    f = pl.program_id(1)

    @pl.when(f == 0)
    def _():
        hv = h_ref[...]
        xn_s[...] = _rms(hv, nw_ref[...]).astype(BF16)
        out_ref[...] = hv

    xn = xn_s[...]
    if not step:
        @pl.when(i % seq_tiles == 0)
        def _():
            carry_s[f] = jnp.zeros((SUBLANES, tf), F32)

    hw = tf // 2
    halves = [slice(0, hw), slice(hw, tf)]
    us = [jnp.dot(xn, wup_ref[:, cs], preferred_element_type=F32) for cs in halves]
    gates = [jnp.dot(xn, wgate_ref[:, cs], preferred_element_type=F32) for cs in halves]
    rid = _iota((tm, 1), 0)
    down = None
    for cs, u, gate in zip(halves, us, gates):
        wc = wc_ref[:, cs]
        if step:
            u2, u1 = p0_ref[:, cs], p1_ref[:, cs]
            tail_ref[:, cs] = u
        else:
            prev = carry_s[f][:, cs]
            u1 = jnp.where(rid == 0, prev[SUBLANES - 1:SUBLANES], pltpu.roll(u, 1, 0))
            u2 = jnp.where(rid == 0, prev[SUBLANES - 2:SUBLANES - 1],
                           jnp.where(rid == 1, prev[SUBLANES - 1:SUBLANES], pltpu.roll(u, 2, 0)))
            tail_ref[0, :, cs] = u[tm - SUBLANES:tm]
        c = u2 * wc[0:1] + u1 * wc[1:2] + u * wc[2:3] + bc_ref[:, cs]
        act = (jax.nn.gelu(c) * gate).astype(BF16)
        d = jnp.dot(act, wdown_ref[cs, :], preferred_element_type=F32)
        down = d if down is None else down + d
    if not step:
        carry_s[f] = jnp.concatenate([u[tm - SUBLANES:tm] for u in us], axis=1)
    out_ref[...] += down
    if final_norm:
        @pl.when(f == nf - 1)
        def _():
            out_ref[...] = _rms(out_ref[...], fnw_ref[...])


def _ffn_call(h, bsz, t, conv_prev, norm_w, w_up, w_gate, w_conv, b_conv, w_down, final_norm_w):
    m, dm = h.shape
    dff = w_up.shape[1]
    step = t == 1
    tm = _tiles(m, t)
    tf = 512
    nf = dff // tf
    seq_tiles = 1 if step else t // tm
    final_norm = final_norm_w is not None
    args = [h, _row(norm_w), w_up, w_gate, w_conv.astype(F32), _row(b_conv.astype(F32)), w_down]
    specs = [pl.BlockSpec((tm, dm), lambda i, f: (i, 0)), pl.BlockSpec((1, dm), lambda i, f: (0, 0)),
             pl.BlockSpec((dm, tf), lambda i, f: (0, f)), pl.BlockSpec((dm, tf), lambda i, f: (0, f)),
             pl.BlockSpec((CONV_W, tf), lambda i, f: (0, f)), pl.BlockSpec((1, tf), lambda i, f: (0, f)),
             pl.BlockSpec((tf, dm), lambda i, f: (f, 0))]
    if step:
        args += [conv_prev[:, 0].astype(F32), conv_prev[:, 1].astype(F32)]
        specs += [pl.BlockSpec((tm, tf), lambda i, f: (i, f))] * 2
        tail_shape = jax.ShapeDtypeStruct((m, dff), F32)
        tail_spec = pl.BlockSpec((tm, tf), lambda i, f: (i, f))
    else:
        tail_shape = jax.ShapeDtypeStruct((m // tm, SUBLANES, dff), F32)
        tail_spec = pl.BlockSpec((1, SUBLANES, tf), lambda i, f: (i, 0, f))
    if final_norm:
        args.append(_row(final_norm_w))
        specs.append(pl.BlockSpec((1, dm), lambda i, f: (0, 0)))
    kern = functools.partial(_ffn_kernel, step=step, tm=tm, tf=tf, nf=nf, seq_tiles=seq_tiles,
                             final_norm=final_norm)
    out, tail = pl.pallas_call(
        kern,
        grid=(m // tm, nf),
        in_specs=specs,
        out_specs=[pl.BlockSpec((tm, dm), lambda i, f: (i, 0)), tail_spec],
        out_shape=[jax.ShapeDtypeStruct((m, dm), F32), tail_shape],
        scratch_shapes=[pltpu.VMEM((tm, dm), BF16), pltpu.VMEM((nf, SUBLANES, tf), F32)],
        compiler_params=_cparams(("arbitrary", "arbitrary")),
        name="conv_ffn",
    )(*args)
    if step:
        conv_new = jnp.stack([conv_prev[:, 1].astype(F32), tail], axis=1)
    else:
        conv_new = tail.reshape(bsz, seq_tiles, SUBLANES, dff)[:, -1, SUBLANES - (CONV_W - 1):]
    return out, conv_new


def _trunk(x, s5_re, s5_im, rw_shift, rw_wkv, gla_s, hg_s, ffn_conv, p):
    bsz, t, dm = x.shape
    h = x.reshape(bsz * t, dm).astype(F32)
    depth = p['norm_mix'].shape[0]
    conv_new = []
    for i in range(depth):
        kind = i % 4
        nw = p['norm_mix'][i]
        if kind == 0:
            h, s5_re, s5_im = _s5_layer(h, bsz, t, s5_re, s5_im, nw, p['s5'], p['s5_c_re'], p['s5_c_im'],
                                        p['s5_d'], p['s5_w_glu'])
        elif kind == 1:
            h, rw_shift, rw_wkv = _rwkv_layer(h, bsz, t, rw_shift, rw_wkv, nw, p['rw'])
        elif kind == 2:
            h, gla_s = _gla_layer(h, bsz, t, gla_s, nw, p['gla'])
        else:
            h, hg_s = _hgrn_layer(h, bsz, t, hg_s, nw, p['hg'], i)
        fin = p['norm_final'] if i == depth - 1 else None
        h, c = _ffn_call(h, bsz, t, ffn_conv[i], p['norm_ffn'][i], p['ffn_w_up'][i], p['ffn_w_gate'][i],
                         p['ffn_w_conv'][i], p['ffn_b_conv'][i], p['ffn_w_down'][i], fin)
        conv_new.append(c)
    return h.reshape(bsz, t, dm), s5_re, s5_im, rw_shift, rw_wkv, gla_s, hg_s, jnp.stack(conv_new)


def _pad_cols(w, n):
    return jnp.pad(w, ((0, 0), (0, n - w.shape[1])))


def _pad_rows(w, n):
    return jnp.pad(w, ((0, n - w.shape[0]), (0, 0)))


def kernel(x_prompt, x_sample, state_s5_re, state_s5_im, state_rwkv_shift, state_rwkv_wkv, state_gla, state_hgrn, state_ffn_conv, norm_mix, norm_ffn, norm_final, s5_lambda_re, s5_lambda_im, s5_log_dt, s5_b_re, s5_b_im, s5_c_re, s5_c_im, s5_d, s5_w_glu, rw_mix, rw_w_r, rw_w_k, rw_w_v, rw_w_o, rw_w0, rw_w1, rw_w2, rw_a0, rw_a1, rw_a2, rw_g1, rw_g2, rw_k_k, rw_k_a, rw_r_k, rw_ln_w, rw_ln_b, gla_w_q, gla_w_k, gla_w_v, gla_w_gk1, gla_w_gk2, gla_b_gk, gla_w_g, gla_norm, gla_w_o, hg_w_q, hg_w_f, hg_w_i, hg_w_g, hg_lb, hg_norm, hg_w_o, ffn_w_up, ffn_w_gate, ffn_w_conv, ffn_b_conv, ffn_w_down):
    bf = lambda w: w.astype(BF16)
    f32 = lambda w: w.astype(F32)
    lora = LANES * pl.cdiv(rw_w1.shape[1], LANES)
    alora = LANES * pl.cdiv(rw_a1.shape[1], LANES)
    grank = LANES * pl.cdiv(gla_w_gk1.shape[1], LANES)
    p = {
        'norm_mix': f32(norm_mix), 'norm_ffn': f32(norm_ffn), 'norm_final': f32(norm_final),
        's5': _s5_params(s5_lambda_re, s5_lambda_im, s5_log_dt, s5_b_re, s5_b_im, s5_c_re, s5_c_im, s5_d),
        's5_c_re': s5_c_re, 's5_c_im': s5_c_im, 's5_d': s5_d, 's5_w_glu': bf(s5_w_glu),
        'rw': dict(mix=_pad_rows(f32(rw_mix), SUBLANES), w_r=bf(rw_w_r), w_k=bf(rw_w_k), w_v=bf(rw_w_v),
                   w_o=bf(rw_w_o), w0=f32(rw_w0), w1=bf(_pad_cols(rw_w1, lora)), w2=bf(_pad_rows(rw_w2, lora)),
                   a0=f32(rw_a0), a1=bf(_pad_cols(rw_a1, alora)), a2=bf(_pad_rows(rw_a2, alora)),
                   g1=bf(rw_g1), g2=bf(rw_g2), k_k=f32(rw_k_k), k_a=f32(rw_k_a), r_k=f32(rw_r_k).reshape(-1),
                   ln_w=f32(rw_ln_w), ln_b=f32(rw_ln_b)),
        'gla': dict(w_q=bf(gla_w_q), w_k=bf(gla_w_k), w_v=bf(gla_w_v), gk1=bf(_pad_cols(gla_w_gk1, grank)),
                    gk2=bf(_pad_rows(gla_w_gk2, grank)), b_gk=f32(gla_b_gk), w_g=bf(gla_w_g), norm=f32(gla_norm),
                    w_o=bf(gla_w_o)),
        'hg': dict(w_q=bf(hg_w_q), w_f=bf(hg_w_f), w_i=bf(hg_w_i), w_g=bf(hg_w_g), lb=f32(hg_lb),
                   norm=f32(hg_norm), w_o=bf(hg_w_o)),
        'ffn_w_up': bf(ffn_w_up), 'ffn_w_gate': bf(ffn_w_gate), 'ffn_w_conv': ffn_w_conv,
        'ffn_b_conv': ffn_b_conv, 'ffn_w_down': bf(ffn_w_down),
    }
    nb, _, dm = x_prompt.shape
    depth = norm_mix.shape[0]
    dff = ffn_w_up.shape[-1]
    z_s5 = jnp.zeros((nb,) + state_s5_re.shape[1:], F32)
    outs_p = _trunk(x_prompt, z_s5, z_s5, jnp.zeros((nb, dm), F32),
                    jnp.zeros((nb,) + state_rwkv_wkv.shape[1:], F32), jnp.zeros((nb,) + state_gla.shape[1:], F32),
                    jnp.zeros((nb,) + state_hgrn.shape[1:], F32), jnp.zeros((depth, nb, CONV_W - 1, dff), F32), p)
    outs_s = _trunk(x_sample, f32(state_s5_re), f32(state_s5_im), state_rwkv_shift, f32(state_rwkv_wkv),
                    f32(state_gla), f32(state_hgrn), state_ffn_conv, p)
    return (outs_p[0], outs_s[0]) + tuple(outs_p[1:]) + tuple(outs_s[1:])
```

```python
import functools
import math

import jax
import jax.numpy as jnp
from jax import lax
from jax.experimental import pallas as pl
from jax.experimental.pallas import tpu as pltpu

F32 = jnp.float32
BF16 = jnp.bfloat16

NORM_EPS = 1e-6
RW_LN_EPS = 64e-5
RW_HEAD = 64
S5_GROUP = 16
S5_STATE = 64
S5_CHUNK = 16
GLA_HEADS = 4
GLA_GATE_TEMP = 16.0
HG_EXPAND = 128
CONV_W = 3
LA_CHUNK = 64
RW_CHUNK = 64
STEP_PAD = 16
LANES = 128
SUBLANES = 8
VMEM_LIMIT = 56 * 1024 * 1024


def _cparams(sem):
    return pltpu.CompilerParams(dimension_semantics=sem, vmem_limit_bytes=VMEM_LIMIT)


def _bdot(a, b):
    return jnp.dot(a.astype(BF16), b.astype(BF16), preferred_element_type=F32)


def _bdot_nt(a, b):
    return lax.dot_general(a.astype(BF16), b.astype(BF16), (((1,), (1,)), ((), ())),
                           preferred_element_type=F32)


def _bdot_tn(a, b):
    return lax.dot_general(a.astype(BF16), b.astype(BF16), (((0,), (0,)), ((), ())),
                           preferred_element_type=F32)


def _split3(x):
    hi = x.astype(BF16)
    r = x - hi.astype(F32)
    mid = r.astype(BF16)
    lo = (r - mid.astype(F32)).astype(BF16)
    return hi, mid, lo


def _xdot_l(m, x):
    hi, mid, lo = _split3(x)
    d = lambda p: jnp.dot(m, p, preferred_element_type=F32)
    return d(hi) + d(mid) + d(lo)


def _xdot_r(x, m):
    hi, mid, lo = _split3(x)
    d = lambda p: jnp.dot(p, m, preferred_element_type=F32)
    return d(hi) + d(mid) + d(lo)


def _rms(x, g):
    y = x * lax.rsqrt(jnp.mean(x * x, axis=-1, keepdims=True) + NORM_EPS)
    return y * g


def _iota(shape, dim):
    return lax.broadcasted_iota(jnp.int32, shape, dim)


def _rmsnorm_kernel(x_ref, g_ref, o_ref):
    o_ref[...] = _rms(x_ref[...], g_ref[...])


def _rmsnorm_call(x, g):
    r, d = x.shape
    tr = 512 if r % 512 == 0 else r
    return pl.pallas_call(
        _rmsnorm_kernel,
        grid=(r // tr,),
        in_specs=[pl.BlockSpec((tr, d), lambda i: (i, 0)), pl.BlockSpec((1, d), lambda i: (0, 0))],
        out_specs=pl.BlockSpec((tr, d), lambda i: (i, 0)),
        out_shape=jax.ShapeDtypeStruct((r, d), F32),
        compiler_params=_cparams(("parallel",)),
        name="rmsnorm",
    )(x, g.reshape(1, d))


def _proj_call(body, *, m, n, tm, tn, row_ins, full_ins, col_ins, rc_ins, out_dtypes, scratch, name):
    args, specs = [], []
    for a in row_ins:
        if isinstance(a, tuple):
            args.append(a[0]); specs.append(a[1])
        else:
            args.append(a); specs.append(pl.BlockSpec((tm, a.shape[1]), lambda i, j: (i, 0)))
    for a in full_ins:
        args.append(a); specs.append(pl.BlockSpec(a.shape, lambda i, j, nd=a.ndim: (0,) * nd))
    for a in col_ins:
        args.append(a); specs.append(pl.BlockSpec((a.shape[0], tn), lambda i, j: (0, j)))
    for a in rc_ins:
        args.append(a); specs.append(pl.BlockSpec((tm, tn), lambda i, j: (i, j)))
    nr, nf, nc, nrc, no = len(row_ins), len(full_ins), len(col_ins), len(rc_ins), len(out_dtypes)

    def kernel(*refs):
        p = 0
        groups = []
        for cnt in (nr, nf, nc, nrc, no):
            groups.append(refs[p:p + cnt]); p += cnt
        body(*groups, refs[p:])

    outs = pl.pallas_call(
        kernel,
        grid=(m // tm, n // tn),
        in_specs=specs,
        out_specs=[pl.BlockSpec((tm, tn), lambda i, j: (i, j)) for _ in out_dtypes],
        out_shape=[jax.ShapeDtypeStruct((m, n), dt) for dt in out_dtypes],
        scratch_shapes=scratch,
        compiler_params=_cparams(("parallel", "arbitrary")),
        name=name,
    )(*args)
    return outs


def _row(v):
    return v.reshape(1, -1)


def _tiles(m, t):
    if t == 1:
        return 512 if m % 512 == 0 else m
    return min(512, t)


def _s5_params(lam_re, lam_im, log_dt, b_re, b_im, c_re, c_im, d):
    hp = lax.Precision.HIGHEST
    g, n = lam_re.shape
    p = b_re.shape[-1]
    cl = S5_CHUNK
    lr = jnp.minimum(lam_re.astype(F32), -1e-4)
    li = lam_im.astype(F32)
    dt = jnp.exp(log_dt.astype(F32))[:, None]
    mag = jnp.exp(lr * dt)
    ab_re = mag * jnp.cos(li * dt)
    ab_im = mag * jnp.sin(li * dt)
    den = lr * lr + li * li
    f_re = ((ab_re - 1.0) * lr + ab_im * li) / den
    f_im = (ab_im * lr - (ab_re - 1.0) * li) / den
    br, bi = b_re.astype(F32), b_im.astype(F32)
    bb_re = f_re[..., None] * br - f_im[..., None] * bi
    bb_im = f_re[..., None] * bi + f_im[..., None] * br
    cr, ci = c_re.astype(F32), c_im.astype(F32)

    pr, pi = [jnp.ones_like(ab_re)], [jnp.zeros_like(ab_im)]
    for _ in range(cl):
        pr.append(pr[-1] * ab_re - pi[-1] * ab_im)
        pi.append(pr[-2] * ab_im + pi[-1] * ab_re)
    pw_re = jnp.stack(pr)
    pw_im = jnp.stack(pi)
    ajb_re = pw_re[:cl, :, :, None] * bb_re - pw_im[:cl, :, :, None] * bb_im
    ajb_im = pw_re[:cl, :, :, None] * bb_im + pw_im[:cl, :, :, None] * bb_re
    kj = (jnp.einsum('gpn,jgnq->jgpq', cr, ajb_re, precision=hp)
          - jnp.einsum('gpn,jgnq->jgpq', ci, ajb_im, precision=hp))
    s_idx = jnp.arange(cl)[:, None]
    t_idx = jnp.arange(cl)[None, :]
    lag = t_idx - s_idx
    kst = kj[jnp.clip(lag, 0, cl - 1)]
    kst = jnp.where((lag >= 0)[:, :, None, None, None], kst, 0.0)
    mt = kst.transpose(2, 0, 4, 1, 3).reshape(g, cl * p, cl * p)
    w_re = ajb_re[::-1].transpose(1, 0, 3, 2).reshape(g, cl * p, n)
    w_im = ajb_im[::-1].transpose(1, 0, 3, 2).reshape(g, cl * p, n)
    wm = jnp.concatenate([w_re, w_im], axis=-1)
    ca_re = cr[None] * pw_re[1:, :, None, :] - ci[None] * pw_im[1:, :, None, :]
    ca_im = cr[None] * pw_im[1:, :, None, :] + ci[None] * pw_re[1:, :, None, :]
    v_re = ca_re.transpose(1, 3, 0, 2).reshape(g, n, cl * p)
    v_im = -ca_im.transpose(1, 3, 0, 2).reshape(g, n, cl * p)
    vm = jnp.concatenate([v_re, v_im], axis=1)
    ar, ai = pw_re[cl], pw_im[cl]
    pcs, qcs = [], []
    for _ in range(SUBLANES):
        pcs.append(jnp.concatenate([ar, ar], axis=-1))
        qcs.append(jnp.concatenate([-ai, ai], axis=-1))
        ar, ai = ar * ar - ai * ai, 2.0 * ar * ai
    pc = jnp.stack(pcs, axis=1)
    qc = jnp.stack(qcs, axis=1)
    return dict(ab_re=ab_re, ab_im=ab_im, bb_re=bb_re, bb_im=bb_im, mt=mt.astype(BF16), wm=wm.astype(BF16),
                vm=vm.astype(BF16), pc=pc, qc=qc)


def _s5_seq_kernel(x_ref, mt_ref, w_ref, vm_ref, pc_ref, qc_ref, d_ref, h0_ref, z_ref, hf_ref, *, gb, nc):
    half = S5_STATE
    cl, p = S5_CHUNK, S5_GROUP
    per = LANES // p
    rid = _iota((nc, 1), 0)
    lane_blk = _iota((1, LANES), 1) // p
    nlev = int(math.log2(nc))
    xs = [x_ref[0, :, s, :] for s in range(cl)]
    groups = range(gb)

    def to_chunk(g):
        pieces = []
        for hf in range(cl // per):
            acc = None
            for j in range(per):
                src = xs[hf * per + j]
                sh = ((j - g) * p) % LANES
                src = pltpu.roll(src, sh, 1) if sh else src
                acc = src if acc is None else jnp.where(lane_blk == j, src, acc)
            pieces.append(acc)
        return jnp.concatenate(pieces, axis=1).astype(BF16)

    ub = [to_chunk(g) for g in groups]
    y = [jnp.dot(u, mt_ref[g], preferred_element_type=F32) for g, u in zip(groups, ub)]
    x = [jnp.dot(u, w_ref[g], preferred_element_type=F32) for g, u in zip(groups, ub)]
    hprev = []
    for g in groups:
        pc = pc_ref[g]
        qc = qc_ref[g]
        h0 = h0_ref[0, g]
        h0c = pc[0:1] * h0 + qc[0:1] * pltpu.roll(h0, half, 1)
        xg = x[g] + jnp.where(rid == 0, h0c, 0.0)
        for lv in range(nlev):
            sh = 1 << lv
            xsft = jnp.where(rid >= sh, pltpu.roll(xg, sh, 0), 0.0)
            xg = xg + pc[lv:lv + 1] * xsft + qc[lv:lv + 1] * pltpu.roll(xsft, half, 1)
        hf_ref[0, g] = xg[nc - 1:nc, :]
        hprev.append(jnp.where(rid == 0, h0, pltpu.roll(xg, 1, 0)).astype(BF16))
    y = [yy + jnp.dot(hp, vm_ref[g], preferred_element_type=F32) for g, yy, hp in zip(groups, y, hprev)]
    for t in range(cl):
        hf, j = divmod(t, per)
        acc = None
        for g in groups:
            src = y[g][:, hf * LANES:(hf + 1) * LANES]
            sh = ((g - j) * p) % LANES
            src = pltpu.roll(src, sh, 1) if sh else src
            acc = src if acc is None else jnp.where(lane_blk == g, src, acc)
        z_ref[0, :, t, :] = jax.nn.gelu(acc + d_ref[...] * xs[t])


def _s5_seq_call(x4, prm, d, h0):
    b, nc, cl, dm = x4.shape
    g = dm // S5_GROUP
    w = cl * S5_GROUP
    n2 = 2 * S5_STATE
    gb = LANES // S5_GROUP
    kern = functools.partial(_s5_seq_kernel, gb=gb, nc=nc)
    gspec = lambda shape: pl.BlockSpec((gb,) + shape, lambda gi, bi: (gi, 0, 0))
    xspec = pl.BlockSpec((1, nc, cl, LANES), lambda gi, bi: (bi, 0, 0, gi))
    return pl.pallas_call(
        kern,
        grid=(g // gb, b),
        in_specs=[xspec,
                  gspec((w, w)), gspec((w, n2)), gspec((n2, w)), gspec((SUBLANES, n2)), gspec((SUBLANES, n2)),
                  pl.BlockSpec((1, LANES), lambda gi, bi: (0, gi)),
                  pl.BlockSpec((1, gb, 1, n2), lambda gi, bi: (bi, gi, 0, 0))],
        out_specs=[xspec, pl.BlockSpec((1, gb, 1, n2), lambda gi, bi: (bi, gi, 0, 0))],
        out_shape=[jax.ShapeDtypeStruct(x4.shape, F32), jax.ShapeDtypeStruct((b, g, 1, n2), F32)],
        compiler_params=_cparams(("parallel", "arbitrary")),
        name="s5_seq",
    )(x4, prm['mt'], prm['wm'], prm['vm'], prm['pc'], prm['qc'], _row(d.astype(F32)), h0)


def _s5_step_kernel(u_ref, hr_ref, hi_ref, ar_ref, ai_ref, bdr_ref, bdi_ref, cdr_ref, cdi_ref, d_ref,
                    z_ref, hro_ref, hio_ref):
    u = u_ref[...]
    ar, ai = ar_ref[...], ai_ref[...]
    hr, hi = hr_ref[...], hi_ref[...]
    hr2 = ar * hr - ai * hi + _bdot(u, bdr_ref[0])
    hi2 = ar * hi + ai * hr + _bdot(u, bdi_ref[0])
    y = _bdot(hr2, cdr_ref[0]) - _bdot(hi2, cdi_ref[0]) + d_ref[...] * u
    z_ref[...] = jax.nn.gelu(y)
    hro_ref[...] = hr2
    hio_ref[...] = hi2


def _s5_step_call(u, h_re, h_im, prm, c_re, c_im, d):
    b, dm = u.shape
    g, n = prm['ab_re'].shape
    p = S5_GROUP
    gpb = LANES // p
    nb = g // gpb
    eye = jnp.eye(gpb, dtype=F32)
    bd = lambda bb: jnp.einsum('kgnp,gh->kgphn', bb.reshape(nb, gpb, n, p), eye).reshape(nb, gpb * p, gpb * n)
    cd = lambda cc: jnp.einsum('kgpn,gh->kgnhp', cc.astype(F32).reshape(nb, gpb, p, n), eye).reshape(
        nb, gpb * n, gpb * p)
    bdr, bdi = bd(prm['bb_re']).astype(BF16), bd(prm['bb_im']).astype(BF16)
    cdr, cdi = cd(c_re).astype(BF16), cd(c_im).astype(BF16)
    wn = gpb * n
    cspec = lambda wd: pl.BlockSpec((b, wd), lambda k: (0, k))
    rspec = lambda wd: pl.BlockSpec((1, wd), lambda k: (0, k))
    z, hr2, hi2 = pl.pallas_call(
        _s5_step_kernel,
        grid=(nb,),
        in_specs=[cspec(LANES), cspec(wn), cspec(wn), rspec(wn), rspec(wn),
                  pl.BlockSpec((1, LANES, wn), lambda k: (k, 0, 0)), pl.BlockSpec((1, LANES, wn), lambda k: (k, 0, 0)),
                  pl.BlockSpec((1, wn, LANES), lambda k: (k, 0, 0)), pl.BlockSpec((1, wn, LANES), lambda k: (k, 0, 0)),
                  rspec(LANES)],
        out_specs=[cspec(LANES), cspec(wn), cspec(wn)],
        out_shape=[jax.ShapeDtypeStruct((b, dm), F32), jax.ShapeDtypeStruct((b, g * n), F32),
                   jax.ShapeDtypeStruct((b, g * n), F32)],
        compiler_params=_cparams(("parallel",)),
        name="s5_step",
    )(u, h_re.reshape(b, g * n), h_im.reshape(b, g * n), prm['ab_re'].reshape(1, g * n),
      prm['ab_im'].reshape(1, g * n), bdr, bdi, cdr, cdi, _row(d.astype(F32)))
    return z, hr2.reshape(b, g, n), hi2.reshape(b, g, n)


def _glu_body(row, full, col, rc, out, scr):
    (z_ref,) = row
    (w_ref,) = col
    zt_ref, h_ref = rc
    (zb,) = scr

    @pl.when(pl.program_id(1) == 0)
    def _():
        zb[...] = z_ref[...].astype(BF16)

    zt = zt_ref[...]
    gate = jax.nn.sigmoid(jnp.dot(zb[...], w_ref[...], preferred_element_type=F32))
    out[0][...] = h_ref[...] + zt * gate


def _s5_layer(h, bsz, t, st_re, st_im, norm_w, prm, c_re, c_im, d, w_glu):
    m, dm = h.shape
    g, n, p = dm // S5_GROUP, S5_STATE, S5_GROUP
    xn = _rmsnorm_call(h, norm_w)
    if t == 1:
        z, s_re, s_im = _s5_step_call(xn, st_re, st_im, prm, c_re, c_im, d)
    else:
        cl = S5_CHUNK
        nc = t // cl
        h0 = jnp.concatenate([st_re, st_im], axis=-1).reshape(bsz, g, 1, 2 * n)
        z4, hf = _s5_seq_call(xn.reshape(bsz, nc, cl, dm), prm, d, h0)
        z = z4.reshape(m, dm)
        s_re, s_im = hf[:, :, 0, :n], hf[:, :, 0, n:]
    tm = _tiles(m, t)
    (h_new,) = _proj_call(_glu_body, m=m, n=dm, tm=tm, tn=512, row_ins=[z], full_ins=[], col_ins=[w_glu],
                          rc_ins=[z, h], out_dtypes=[F32], scratch=[pltpu.VMEM((tm, dm), BF16)], name="s5_glu")
    return h_new, s_re, s_im


def _rwkv_proj_body(row, full, col, rc, out, scr, *, tm, seq_tiles, step):
    h_ref, prev_ref = row
    nw_ref, mix_ref, w1_ref, a1_ref, g1_ref, bd_ref = full
    wr, wk, wv, w2, a2, g2, w0, a0, kk_ref, ka_ref = col
    r_o, lw_o, k_o, v_o, a_o, b_o, g_o = out
    xr_s, xk_s, xv_s, tw_s, ta_s, tg_s = scr
    i = pl.program_id(0)

    @pl.when(pl.program_id(1) == 0)
    def _():
        nw = nw_ref[...]
        xn = _rms(h_ref[...], nw)
        if step:
            xprev = prev_ref[...]
        else:
            last = _rms(prev_ref[...], nw)[SUBLANES - 1:SUBLANES, :]
            last = jnp.where(i % seq_tiles == 0, 0.0, last)
            xprev = jnp.where(_iota((tm, 1), 0) == 0, last, pltpu.roll(xn, 1, 0))
        xx = xprev - xn
        mix = mix_ref[...]
        xr_s[...] = (xn + xx * mix[0:1]).astype(BF16)
        xk_s[...] = (xn + xx * mix[2:3]).astype(BF16)
        xv_s[...] = (xn + xx * mix[3:4]).astype(BF16)
        tw_s[...] = jnp.tanh(_bdot(xn + xx * mix[1:2], w1_ref[...])).astype(BF16)
        ta_s[...] = _bdot(xn + xx * mix[4:5], a1_ref[...]).astype(BF16)
        tg_s[...] = jax.nn.sigmoid(_bdot(xn + xx * mix[5:6], g1_ref[...])).astype(BF16)

    dot = lambda a, b: jnp.dot(a[...], b[...], preferred_element_type=F32)
    r = dot(xr_s, wr)
    k = dot(xk_s, wk)
    v = dot(xv_s, wv)
    lw = -jax.nn.sigmoid(w0[...] + dot(tw_s, w2)) * math.exp(-0.5)
    a = jax.nn.sigmoid(a0[...] + dot(ta_s, a2))
    kk = k * kk_ref[...]
    ss = _bdot(kk * kk, bd_ref[...])
    kk = kk / jnp.maximum(jnp.sqrt(ss), 1e-12)
    r_o[...] = r
    lw_o[...] = lw
    k_o[...] = k * (1.0 + (a - 1.0) * ka_ref[...])
    v_o[...] = v
    a_o[...] = -kk
    b_o[...] = kk * a
    g_o[...] = dot(tg_s, g2)


def _rwkv_chunk_kernel(r_ref, lw_ref, k_ref, v_ref, a_ref, b_ref, gate_ref, lnw_ref, lnb_ref, rk_ref, s0_ref,
                       o_ref, sout_ref, p_scr, rp_scr, o0_scr, m_scr, d_scr, g_scr, *, npair, cl, nch):
    hd = RW_HEAD
    pw = 2 * hd
    lane = _iota((1, pw), 1)
    head0 = lane < hd
    bdmask = (_iota((pw, pw), 0) // hd) == (_iota((pw, pw), 1) // hd)
    e_dup = ((_iota((hd, pw), 1) % hd) == _iota((hd, pw), 0)).astype(BF16)
    e_fold = ((_iota((pw, hd), 0) % hd) == _iota((pw, hd), 1)).astype(BF16)
    ones_bd = bdmask.astype(BF16)
    avg = (bdmask.astype(F32) * (1.0 / hd)).astype(BF16)
    for p in range(npair):
        p_scr[p] = jnp.where(bdmask, _xdot_r(s0_ref[0, p], e_dup), 0.0)
    s2 = 2 * cl
    rr = _iota((s2, s2), 0)
    cc = _iota((s2, s2), 1)
    same = (rr // cl) == (cc // cl)
    strict = same & ((rr % cl) > (cc % cl))
    incl = same & ((rr % cl) >= (cc % cl))
    eye_s = (rr == cc).astype(F32)
    rid = _iota((cl, 1), 0)
    zeros_s = jnp.zeros((s2, pw), F32)
    nround = int(math.log2(cl)) - 1

    def stack(x):
        return jnp.concatenate([jnp.where(head0, x, 0.0), jnp.where(head0, 0.0, x)], axis=0)

    ug = 4 if nch % 4 == 0 else 1
    cat = jnp.concatenate

    def tile(ref, c, p):
        return ref[pl.ds(pl.multiple_of(c * cl, cl), cl), p * pw:(p + 1) * pw]

    def phase_a(it, carry):
        ch = [(it * ug + j, p) for j in range(ug) for p in range(npair)]
        ld = lambda ref: [tile(ref, c, p) for c, p in ch]
        lw = ld(lw_ref)
        cs = lw
        sh = 1
        while sh < cl:
            cs = [x + jnp.where(rid >= sh, pltpu.roll(x, sh, 0), 0.0) for x in cs]
            sh *= 2
        gam = [jnp.exp(x) for x in cs]
        ginv = [jnp.exp(-x) for x in cs]
        ats = [stack(a * jnp.exp(x - l)) for a, x, l in zip(ld(a_ref), cs, lw)]
        rts = [stack(r * g) for r, g in zip(ld(r_ref), gam)]
        x1s = [cat([a, r], axis=0).astype(BF16) for a, r in zip(ats, rts)]
        x2s = [cat([stack(b * g), stack(k * g)], axis=0).astype(BF16)
               for b, k, g in zip(ld(b_ref), ld(k_ref), ginv)]
        gs = [_bdot_nt(x1, x2) for x1, x2 in zip(x1s, x2s)]
        nmat = [jnp.where(strict, g[:s2, :s2], 0.0) for g in gs]
        aak = [jnp.where(strict, g[:s2, s2:], 0.0) for g in gs]
        arb = [jnp.where(incl, g[s2:, :s2], 0.0) for g in gs]
        ark = [jnp.where(incl, g[s2:, s2:], 0.0) for g in gs]
        vs = [stack(v) for v in ld(v_ref)]
        akv = [_bdot(x, v) for x, v in zip(aak, vs)]
        rkv = [_bdot(x, v) for x, v in zip(ark, vs)]
        tinv = [eye_s + n for n in nmat]
        pk = nmat
        for _ in range(nround):
            pk = [_bdot(x, x) for x in pk]
            tinv = [t + _bdot(x, t) for x, t in zip(pk, tinv)]
        tz = [_bdot(t, cat([x, a], axis=1)) for t, x, a in zip(tinv, akv, ats)]
        u0s = [x[:, :pw] for x in tz]
        aps = [x[:, pw:] for x in tz]
        z = [_bdot(x, cat([a, u], axis=1)) for x, a, u in zip(arb, aps, u0s)]
        md = [_bdot_tn(cat([cat([a, u], axis=1), cat([zeros_s, v], axis=1)], axis=0), x2)
              for a, u, v, x2 in zip(aps, u0s, vs, x2s)]
        for i, (c, p) in enumerate(ch):
            rps = rts[i] + z[i][:, :pw]
            o0s = z[i][:, pw:] + rkv[i]
            rp_scr[c, p] = (rps[:cl] + rps[cl:]).astype(BF16)
            o0_scr[c, p] = o0s[:cl] + o0s[cl:]
            gl = gam[i][cl - 1:cl, :]
            m_scr[c, p] = (md[i][:pw] * gl).astype(BF16)
            d_scr[c, p] = md[i][pw:] * gl
            g_scr[c, p] = gl
        return carry

    lax.fori_loop(0, nch // ug, phase_a, 0)

    def phase_b(it, carry):
        cs = [it * ug + j for j in range(ug)]
        prs = range(npair)
        bonus = [[_bdot(tile(r_ref, c, p) * tile(k_ref, c, p) * rk_ref[:, p * pw:(p + 1) * pw], ones_bd)
                  * tile(v_ref, c, p) for p in prs] for c in cs]
        st = [p_scr[p] for p in prs]
        outs, mus, dlts, vrs = [], [], [], []
        for j in range(ug + 2):
            if j < ug:
                c = cs[j]
                new = [st[p] * g_scr[c, p] + _bdot(st[p], m_scr[c, p]) + d_scr[c, p] for p in prs]
                outs.append([_bdot_nt(rp_scr[c, p], st[p]) + o0_scr[c, p] for p in prs])
                st = new
            if 1 <= j <= ug:
                mus.append([_bdot(o, avg) for o in outs[j - 1]])
            if 2 <= j:
                dl = [o - m for o, m in zip(outs[j - 2], mus[j - 2])]
                dlts.append(dl)
                vrs.append([_bdot(d * d, avg) for d in dl])
        for p in prs:
            p_scr[p] = st[p]
        for j, c in enumerate(cs):
            for p in prs:
                sl = slice(p * pw, (p + 1) * pw)
                on = dlts[j][p] * lax.rsqrt(vrs[j][p] + RW_LN_EPS) * lnw_ref[:, sl] + lnb_ref[:, sl]
                o_ref[pl.ds(pl.multiple_of(c * cl, cl), cl), sl] = (
                    (on + bonus[j][p]) * tile(gate_ref, c, p)).astype(o_ref.dtype)
        return carry

    lax.fori_loop(0, nch // ug, phase_b, 0)
    for p in range(npair):
        sout_ref[0, p] = _xdot_r(jnp.where(bdmask, p_scr[p], 0.0), e_fold)


def _rwkv_chunk_call(r, lw, k, v, a, b, gate, ln_w, ln_b, r_k, s0, bsz, t, cl):
    m, dm = r.shape
    hd = RW_HEAD
    pw = 2 * hd
    npairs = dm // pw
    npair = 2
    nch = t // cl
    s0p = s0.reshape(bsz, npairs, pw, hd)
    kern = functools.partial(_rwkv_chunk_kernel, npair=npair, cl=cl, nch=nch)
    tspec = pl.BlockSpec((t, npair * pw), lambda bi, pi: (bi, pi))
    vspec = pl.BlockSpec((1, npair * pw), lambda bi, pi: (0, pi))
    sspec = pl.BlockSpec((1, npair, pw, hd), lambda bi, pi: (bi, pi, 0, 0))
    o, s = pl.pallas_call(
        kern,
        grid=(bsz, npairs // npair),
        in_specs=[tspec] * 7 + [vspec] * 3 + [sspec],
        out_specs=[tspec, sspec],
        out_shape=[jax.ShapeDtypeStruct((m, dm), BF16), jax.ShapeDtypeStruct((bsz, npairs, pw, hd), F32)],
        scratch_shapes=[pltpu.VMEM((npair, pw, pw), F32),
                        pltpu.VMEM((nch, npair, cl, pw), BF16), pltpu.VMEM((nch, npair, cl, pw), F32),
                        pltpu.VMEM((nch, npair, pw, pw), BF16), pltpu.VMEM((nch, npair, pw, pw), F32),
                        pltpu.VMEM((nch, npair, 1, pw), F32)],
        compiler_params=_cparams(("parallel", "parallel")),
        name="rwkv_chunk",
    )(r, lw, k, v, a, b, gate, _row(ln_w), _row(ln_b), _row(r_k), s0p)
    return o, s.reshape(bsz, dm // hd, hd, hd)


def _rwkv_step_kernel(r_ref, lw_ref, k_ref, v_ref, a_ref, b_ref, gate_ref, lnw_ref, lnb_ref, rk_ref, s0_ref,
                      o_ref, sout_ref, oacc, *, nh):
    rid = _iota((STEP_PAD, 1), 0)
    hg = 8

    def rows6(hi, mid, lo, order):
        terms = (hi.astype(F32), mid.astype(F32), lo.astype(F32))
        out = jnp.zeros((STEP_PAD, hi.shape[-1]), F32)
        for i, t in enumerate(order):
            out = jnp.where(rid == i, terms[t], out)
        return out.astype(BF16)

    def group(gi, carry):
        hs = [gi * hg + j for j in range(hg)]
        rows = lambda ref: [ref[0, pl.ds(h, 1), :] for h in hs]
        st = [s0_ref[0, h] for h in hs]
        sa = [jnp.sum(s * a, axis=-1, keepdims=True) for s, a in zip(st, rows(a_ref))]
        vx = [rows6(*_split3(v), (0, 0, 1, 0, 1, 2)) for v in rows(v_ref)]
        kx = [rows6(*_split3(k), (0, 1, 0, 2, 1, 0)) for k in rows(k_ref)]
        vk = [lax.dot_general(x, y, (((0,), (0,)), ((), ())), preferred_element_type=F32) for x, y in zip(vx, kx)]
        sn = [s * jnp.exp(w) + c * b + o for s, w, c, b, o in zip(st, rows(lw_ref), sa, rows(b_ref), vk)]
        for h, s in zip(hs, sn):
            sout_ref[0, h] = s
        os_ = [_bdot_nt(jnp.where(rid == 0, r, 0.0), s) for r, s in zip(rows(r_ref), sn)]
        for h, o in zip(hs, os_):
            oacc[pl.ds(h, 1), :] = o[0:1]
        return carry

    lax.fori_loop(0, nh // hg, group, 0)
    o = oacc[...]
    mu = jnp.mean(o, axis=-1, keepdims=True)
    dlt = o - mu
    var = jnp.mean(dlt * dlt, axis=-1, keepdims=True)
    on = dlt * lax.rsqrt(var + RW_LN_EPS) * lnw_ref[...] + lnb_ref[...]
    bonus = jnp.sum(r_ref[0] * k_ref[0] * rk_ref[...], axis=-1, keepdims=True) * v_ref[0]
    o_ref[0] = ((on + bonus) * gate_ref[0]).astype(o_ref.dtype)


def _rwkv_step_call(r, lw, k, v, a, b, gate, ln_w, ln_b, r_k, s0):
    bsz, dm = r.shape
    hd = RW_HEAD
    nh = dm // hd
    heads = lambda x: x.reshape(bsz, nh, hd)
    hspec = pl.BlockSpec((1, nh, hd), lambda bi: (bi, 0, 0))
    pspec = pl.BlockSpec((nh, hd), lambda bi: (0, 0))
    sspec = pl.BlockSpec((1, nh, hd, hd), lambda bi: (bi, 0, 0, 0))
    o, s = pl.pallas_call(
        functools.partial(_rwkv_step_kernel, nh=nh),
        grid=(bsz,),
        in_specs=[hspec] * 7 + [pspec] * 3 + [sspec],
        out_specs=[hspec, sspec],
        out_shape=[jax.ShapeDtypeStruct((bsz, nh, hd), BF16), jax.ShapeDtypeStruct(s0.shape, F32)],
        scratch_shapes=[pltpu.VMEM((nh, hd), F32)],
        compiler_params=_cparams(("parallel",)),
        name="rwkv_step",
    )(*[heads(x) for x in (r, lw, k, v, a, b, gate)], ln_w.reshape(nh, hd), ln_b.reshape(nh, hd),
      r_k.reshape(nh, hd), s0)
    return o.reshape(bsz, dm), s


def _outproj_body(row, full, col, rc, out, scr):
    out[0][...] = rc[0][...] + jnp.dot(row[0][...].astype(BF16), col[0][...], preferred_element_type=F32)


def _outproj_call(a, w, h, t, name):
    m, kdim = a.shape
    n = w.shape[1]
    (o,) = _proj_call(_outproj_body, m=m, n=n, tm=_tiles(m, t), tn=512, row_ins=[a], full_ins=[], col_ins=[w],
                      rc_ins=[h], out_dtypes=[F32], scratch=[], name=name)
    return o


def _rwkv_layer(h, bsz, t, shift, wkv, norm_w, wts):
    m, dm = h.shape
    tm = _tiles(m, t)
    step = t == 1
    if step:
        prev = (shift.astype(F32), pl.BlockSpec((tm, dm), lambda i, j: (i, 0)))
        seq_tiles = 1
    else:
        seq_tiles = t // tm
        prev = (h, pl.BlockSpec((SUBLANES, dm), lambda i, j: (jnp.maximum(i * (tm // SUBLANES) - 1, 0), 0)))
    tn = 256
    bd = (jnp.arange(tn)[:, None] // RW_HEAD == jnp.arange(tn)[None, :] // RW_HEAD).astype(BF16)
    body = functools.partial(_rwkv_proj_body, tm=tm, seq_tiles=seq_tiles, step=step)
    lora = wts['w1'].shape[1]
    glora = wts['g1'].shape[1]
    r, lw, k, v, a, b, g = _proj_call(
        body, m=m, n=dm, tm=tm, tn=tn, row_ins=[h, prev],
        full_ins=[_row(norm_w), wts['mix'], wts['w1'], wts['a1'], wts['g1'], bd],
        col_ins=[wts['w_r'], wts['w_k'], wts['w_v'], wts['w2'], wts['a2'], wts['g2'],
                 _row(wts['w0']), _row(wts['a0']), _row(wts['k_k']), _row(wts['k_a'])],
        rc_ins=[], out_dtypes=[F32] * 7,
        scratch=[pltpu.VMEM((tm, dm), BF16)] * 3 + [pltpu.VMEM((tm, lora), BF16)] * 2
        + [pltpu.VMEM((tm, glora), BF16)],
        name="rwkv_proj")
    if step:
        shift_new = _rmsnorm_call(h, norm_w)
        og, s_new = _rwkv_step_call(r, lw, k, v, a, b, g, wts['ln_w'], wts['ln_b'], wts['r_k'], wkv)
    else:
        shift_new = _rmsnorm_call(h.reshape(bsz, t, dm)[:, -1], norm_w)
        og, s_new = _rwkv_chunk_call(r, lw, k, v, a, b, g, wts['ln_w'], wts['ln_b'], wts['r_k'], wkv, bsz, t,
                                     RW_CHUNK)
    return _outproj_call(og, wts['w_o'], h, t, "rwkv_out"), shift_new, s_new


def _la_chunk_kernel(q_ref, k_ref, v_ref, g_ref, gate_ref, nw_ref, s0_ref, o_ref, sout_ref, s_scr,
                     *, hb, dk, dv, cl, nch):
    s_scr[...] = s0_ref[0]
    rowi = _iota((cl, cl), 0)
    coli = _iota((cl, cl), 1)
    tri = (rowi >= coli).astype(BF16)
    levels = [1 << i for i in range(int(math.log2(cl)))]
    masks = {s: ((rowi // s) % 2 == 1) & ((coli // s) == (rowi // s) - 1) for s in levels}
    rid = _iota((cl, 1), 0)
    nw = nw_ref[...]
    ug = max(1, 4 // hb)
    if nch % ug:
        ug = 1

    def rows(c):
        return pl.ds(pl.multiple_of(c * cl, cl), cl)

    def group(it, carry):
        ch = [(it * ug + j, h) for j in range(ug) for h in range(hb)]
        ldk = lambda ref: [ref[rows(c), h * dk:(h + 1) * dk] for c, h in ch]
        ldv = lambda ref: [ref[rows(c), h * dv:(h + 1) * dv] for c, h in ch]
        q, k, v = ldk(q_ref), ldk(k_ref), ldv(v_ref)
        cum = [_xdot_l(tri, g) for g in ldk(g_ref)]
        att = [jnp.zeros((cl, cl), F32) for _ in ch]
        ends = cum
        for s in levels:
            prev = [jnp.where(rid >= s, pltpu.roll(e, s, 0), 0.0) for e in ends]
            a_s = [_bdot_nt(qq * jnp.exp(c - p), kk * jnp.exp(e - c))
                   for qq, kk, c, p, e in zip(q, k, cum, prev, ends)]
            att = [a + jnp.where(masks[s], x, 0.0) for a, x in zip(att, a_s)]
            if 2 * s < cl:
                ends = [jnp.where((rid % (2 * s)) >= s, e, pltpu.roll(e, cl - s, 0)) for e in ends]
        o_in = [_bdot(a, vv) + jnp.sum(qq * kk, axis=-1, keepdims=True) * vv
                for a, vv, qq, kk in zip(att, v, q, k)]
        last = [c[cl - 1:cl, :] for c in cum]
        upd = [_bdot_tn(kk * jnp.exp(l - c), vv) for kk, l, c, vv in zip(k, last, cum, v)]
        dec = [jnp.exp(jnp.transpose(jnp.broadcast_to(l, (LANES, dk)))) for l in last]
        qe = [qq * jnp.exp(c) for qq, c in zip(q, cum)]
        st = [s_scr[h] for h in range(hb)]
        for i, (c, h) in enumerate(ch):
            o = o_in[i] + _bdot(qe[i], st[h])
            st[h] = jnp.concatenate([st[h][:, j * LANES:(j + 1) * LANES] * dec[i] for j in range(dv // LANES)],
                                    axis=1) + upd[i]
            on = o * lax.rsqrt(jnp.mean(o * o, axis=-1, keepdims=True) + NORM_EPS) * nw
            vs = slice(h * dv, (h + 1) * dv)
            o_ref[rows(c), vs] = (on * gate_ref[rows(c), vs]).astype(o_ref.dtype)
        for h in range(hb):
            s_scr[h] = st[h]
        return carry

    lax.fori_loop(0, nch // ug, group, 0)
    sout_ref[0] = s_scr[...]


def _la_chunk_call(q, k, v, g, gate, norm_w, s0, bsz, t, cl, hb, name):
    m = q.shape[0]
    _, nh, dk, dv = s0.shape
    kern = functools.partial(_la_chunk_kernel, hb=hb, dk=dk, dv=dv, cl=cl, nch=t // cl)
    kspec = pl.BlockSpec((t, hb * dk), lambda bi, hi: (bi, hi))
    vspec = pl.BlockSpec((t, hb * dv), lambda bi, hi: (bi, hi))
    sspec = pl.BlockSpec((1, hb, dk, dv), lambda bi, hi: (bi, hi, 0, 0))
    return pl.pallas_call(
        kern,
        grid=(bsz, nh // hb),
        in_specs=[kspec, kspec, vspec, kspec, vspec, pl.BlockSpec((1, dv), lambda bi, hi: (0, 0)), sspec],
        out_specs=[vspec, sspec],
        out_shape=[jax.ShapeDtypeStruct((m, nh * dv), BF16), jax.ShapeDtypeStruct(s0.shape, F32)],
        scratch_shapes=[pltpu.VMEM((hb, dk, dv), F32)],
        compiler_params=_cparams(("parallel", "parallel")),
        name=name,
    )(q, k, v, g, gate, _row(norm_w), s0)


def _la_step_kernel(q_ref, k_ref, v_ref, g_ref, gate_ref, nw_ref, s0_ref, o_ref, sout_ref, *, nh, dk, dv):
    first = _iota((STEP_PAD, 1), 0) == 0

    def pad(x):
        return jnp.where(first, x, 0.0)

    hg = min(nh, 4)

    def group(gi, carry):
        hs = [gi * hg + j for j in range(hg)]
        rows = lambda ref: [ref[0, pl.ds(h, 1), :] for h in hs]
        dec = [jnp.exp(jnp.transpose(jnp.broadcast_to(g, (LANES, dk)))) for g in rows(g_ref)]
        upd = [_bdot_tn(pad(k), pad(v)) for k, v in zip(rows(k_ref), rows(v_ref))]
        sn = [jnp.concatenate([s0_ref[0, h][:, j * LANES:(j + 1) * LANES] * d for j in range(dv // LANES)], axis=1)
              + u for h, d, u in zip(hs, dec, upd)]
        for h, s in zip(hs, sn):
            sout_ref[0, h] = s
        os_ = [_bdot(pad(q), s)[0:1] for q, s in zip(rows(q_ref), sn)]
        for h, o, gt in zip(hs, os_, rows(gate_ref)):
            on = o * lax.rsqrt(jnp.mean(o * o, axis=-1, keepdims=True) + NORM_EPS) * nw_ref[...]
            o_ref[0, pl.ds(h, 1), :] = (on * gt).astype(o_ref.dtype)
        return carry

    lax.fori_loop(0, nh // hg, group, 0)


def _la_step_call(q, k, v, g, gate, norm_w, s0, name):
    bsz, nh, dk, dv = s0.shape
    kh = lambda x: x.reshape(bsz, nh, dk)
    vh = lambda x: x.reshape(bsz, nh, dv)
    kspec = pl.BlockSpec((1, nh, dk), lambda bi: (bi, 0, 0))
    vspec = pl.BlockSpec((1, nh, dv), lambda bi: (bi, 0, 0))
    sspec = pl.BlockSpec((1, nh, dk, dv), lambda bi: (bi, 0, 0, 0))
    o, s = pl.pallas_call(
        functools.partial(_la_step_kernel, nh=nh, dk=dk, dv=dv),
        grid=(bsz,),
        in_specs=[kspec, kspec, vspec, kspec, vspec, pl.BlockSpec((1, dv), lambda bi: (0, 0)), sspec],
        out_specs=[vspec, sspec],
        out_shape=[jax.ShapeDtypeStruct((bsz, nh, dv), F32), jax.ShapeDtypeStruct(s0.shape, F32)],
        compiler_params=_cparams(("parallel",)),
        name=name,
    )(kh(q), kh(k), vh(v), kh(g), vh(gate), _row(norm_w), s0)
    return o.reshape(bsz, nh * dv), s


def _la_run(q, k, v, g, gate, norm_w, s0, bsz, t, hb, name):
    if t == 1:
        return _la_step_call(q, k, v, g, gate, norm_w, s0, name + "_step")
    return _la_chunk_call(q, k, v, g, gate, norm_w, s0, bsz, t, LA_CHUNK, hb, name)


def _norm_prologue(h_ref, nw_ref, xn_s):
    @pl.when(pl.program_id(1) == 0)
    def _():
        xn_s[...] = _rms(h_ref[...], nw_ref[...]).astype(BF16)


def _gla_qkg_body(row, full, col, rc, out, scr, *, scale):
    nw_ref, gk1_ref = full
    wq, wk, gk2, bgk = col
    xn_s, t1_s = scr

    @pl.when(pl.program_id(1) == 0)
    def _():
        xn = _rms(row[0][...], nw_ref[...]).astype(BF16)
        xn_s[...] = xn
        t1_s[...] = jnp.dot(xn, gk1_ref[...], preferred_element_type=F32).astype(BF16)

    xn = xn_s[...]
    out[0][...] = jnp.dot(xn, wq[...], preferred_element_type=F32) * scale
    out[1][...] = jnp.dot(xn, wk[...], preferred_element_type=F32)
    lg = jnp.dot(t1_s[...], gk2[...], preferred_element_type=F32) + bgk[...]
    out[2][...] = jax.nn.log_sigmoid(lg) / GLA_GATE_TEMP


def _vg_body(row, full, col, rc, out, scr):
    _norm_prologue(row[0], full[0], scr[0])
    xn = scr[0][...]
    tn = col[0].shape[1]
    for cs in (slice(0, tn // 2), slice(tn // 2, tn)):
        out[0][:, cs] = jnp.dot(xn, col[0][:, cs], preferred_element_type=F32)
        out[1][:, cs] = jax.nn.silu(jnp.dot(xn, col[1][:, cs], preferred_element_type=F32))


def _gla_layer(h, bsz, t, s0, norm_w, wts):
    m, dm = h.shape
    tm = _tiles(m, t)
    nh = GLA_HEADS
    dk = wts['w_q'].shape[1] // nh
    rank = wts['gk1'].shape[1]
    q, k, lg = _proj_call(functools.partial(_gla_qkg_body, scale=dk ** -0.5), m=m, n=nh * dk, tm=tm, tn=512,
                          row_ins=[h], full_ins=[_row(norm_w), wts['gk1']],
                          col_ins=[wts['w_q'], wts['w_k'], wts['gk2'], _row(wts['b_gk'])], rc_ins=[],
                          out_dtypes=[F32] * 3,
                          scratch=[pltpu.VMEM((tm, dm), BF16), pltpu.VMEM((tm, rank), BF16)], name="gla_qkg")
    v, gate = _proj_call(_vg_body, m=m, n=wts['w_v'].shape[1], tm=tm, tn=512, row_ins=[h],
                         full_ins=[_row(norm_w)], col_ins=[wts['w_v'], wts['w_g']], rc_ins=[],
                         out_dtypes=[F32] * 2, scratch=[pltpu.VMEM((tm, dm), BF16)], name="gla_vg")
    og, s_new = _la_run(q, k, v, lg, gate, wts['norm'], s0, bsz, t, 1, "gla_chunk")
    return _outproj_call(og, wts['w_o'], h, t, "gla_out"), s_new


def _hgrn_proj_body(row, full, col, rc, out, scr, *, scale, layer_idx):
    wq, wf, wi, wg, lbp = col
    _norm_prologue(row[0], full[0], scr[0])
    xn = scr[0][...]
    tn = wq.shape[1]
    for cs in (slice(0, tn // 2), slice(tn // 2, tn)):
        dot = lambda w: jnp.dot(xn, w[:, cs], preferred_element_type=F32)
        z, qv, iv, gv = dot(wf), dot(wq), dot(wi), dot(wg)
        lbx = lbp[:, cs]
        ex = jnp.exp(lbx - jnp.max(lbx, axis=0, keepdims=True))
        lb = jnp.sum(ex[1:layer_idx + 1], axis=0, keepdims=True) / jnp.sum(ex, axis=0, keepdims=True)
        e = jnp.exp(-jnp.abs(z))
        rcp = 1.0 / (1.0 + e)
        pos = z >= 0
        sig = jnp.where(pos, rcp, e * rcp)
        sig_neg = jnp.where(pos, e * rcp, rcp)
        out[0][:, cs] = qv * scale
        out[1][:, cs] = (1.0 - lb) * sig_neg
        out[2][:, cs] = jnp.log(lb + (1.0 - lb) * sig)
        out[3][:, cs] = iv
        out[4][:, cs] = jax.nn.silu(gv)


def _hgrn_layer(h, bsz, t, s0, norm_w, wts, layer_idx):
    m, dm = h.shape
    tm = _tiles(m, t)
    body = functools.partial(_hgrn_proj_body, scale=HG_EXPAND ** -0.5, layer_idx=layer_idx)
    q, k, lf, v, gate = _proj_call(body, m=m, n=dm, tm=tm, tn=512, row_ins=[h], full_ins=[_row(norm_w)],
                                   col_ins=[wts['w_q'], wts['w_f'], wts['w_i'], wts['w_g'], wts['lb']],
                                   rc_ins=[], out_dtypes=[F32] * 5, scratch=[pltpu.VMEM((tm, dm), BF16)],
                                   name="hgrn_proj")
    og, s_new = _la_run(q, k, v, lf, gate, wts['norm'], s0, bsz, t, 2, "hgrn_chunk")
    return _outproj_call(og, wts['w_o'], h, t, "hgrn_out"), s_new


def _ffn_kernel(*refs, step, tm, tf, nf, seq_tiles, final_norm):
    h_ref, nw_ref, wup_ref, wgate_ref, wc_ref, bc_ref, wdown_ref = refs[:7]
    p = 7
    if step:
        p0_ref, p1_ref = refs[p:p + 2]; p += 2
    if final_norm:
        fnw_ref = refs[p]; p += 1
    out_ref, tail_ref = refs[p:p + 2]
    xn_s, carry_s = refs[p + 2:]
    i = pl.program_id(0)
    f = pl.program_id(1)

    @pl.when(f == 0)
    def _():
        hv = h_ref[...]
        xn_s[...] = _rms(hv, nw_ref[...]).astype(BF16)
        out_ref[...] = hv

    xn = xn_s[...]
    if not step:
        @pl.when(i % seq_tiles == 0)
        def _():
            carry_s[f] = jnp.zeros((SUBLANES, tf), F32)

    hw = tf // 2
    halves = [slice(0, hw), slice(hw, tf)]
    us = [jnp.dot(xn, wup_ref[:, cs], preferred_element_type=F32) for cs in halves]
    gates = [jnp.dot(xn, wgate_ref[:, cs], preferred_element_type=F32) for cs in halves]
    rid = _iota((tm, 1), 0)
    down = None
    for cs, u, gate in zip(halves, us, gates):
        wc = wc_ref[:, cs]
        if step:
            u2, u1 = p0_ref[:, cs], p1_ref[:, cs]
            tail_ref[:, cs] = u
        else:
            prev = carry_s[f][:, cs]
            u1 = jnp.where(rid == 0, prev[SUBLANES - 1:SUBLANES], pltpu.roll(u, 1, 0))
            u2 = jnp.where(rid == 0, prev[SUBLANES - 2:SUBLANES - 1],
                           jnp.where(rid == 1, prev[SUBLANES - 1:SUBLANES], pltpu.roll(u, 2, 0)))
            tail_ref[0, :, cs] = u[tm - SUBLANES:tm]
        c = u2 * wc[0:1] + u1 * wc[1:2] + u * wc[2:3] + bc_ref[:, cs]
        act = (jax.nn.gelu(c) * gate).astype(BF16)
        d = jnp.dot(act, wdown_ref[cs, :], preferred_element_type=F32)
        down = d if down is None else down + d
    if not step:
        carry_s[f] = jnp.concatenate([u[tm - SUBLANES:tm] for u in us], axis=1)
    out_ref[...] += down
    if final_norm:
        @pl.when(f == nf - 1)
        def _():
            out_ref[...] = _rms(out_ref[...], fnw_ref[...])


def _ffn_call(h, bsz, t, conv_prev, norm_w, w_up, w_gate, w_conv, b_conv, w_down, final_norm_w):
    m, dm = h.shape
    dff = w_up.shape[1]
    step = t == 1
    tm = _tiles(m, t)
    tf = 512
    nf = dff // tf
    seq_tiles = 1 if step else t // tm
    final_norm = final_norm_w is not None
    args = [h, _row(norm_w), w_up, w_gate, w_conv.astype(F32), _row(b_conv.astype(F32)), w_down]
    specs = [pl.BlockSpec((tm, dm), lambda i, f: (i, 0)), pl.BlockSpec((1, dm), lambda i, f: (0, 0)),
             pl.BlockSpec((dm, tf), lambda i, f: (0, f)), pl.BlockSpec((dm, tf), lambda i, f: (0, f)),
             pl.BlockSpec((CONV_W, tf), lambda i, f: (0, f)), pl.BlockSpec((1, tf), lambda i, f: (0, f)),
             pl.BlockSpec((tf, dm), lambda i, f: (f, 0))]
    if step:
        args += [conv_prev[:, 0].astype(F32), conv_prev[:, 1].astype(F32)]
        specs += [pl.BlockSpec((tm, tf), lambda i, f: (i, f))] * 2
        tail_shape = jax.ShapeDtypeStruct((m, dff), F32)
        tail_spec = pl.BlockSpec((tm, tf), lambda i, f: (i, f))
    else:
        tail_shape = jax.ShapeDtypeStruct((m // tm, SUBLANES, dff), F32)
        tail_spec = pl.BlockSpec((1, SUBLANES, tf), lambda i, f: (i, 0, f))
    if final_norm:
        args.append(_row(final_norm_w))
        specs.append(pl.BlockSpec((1, dm), lambda i, f: (0, 0)))
    kern = functools.partial(_ffn_kernel, step=step, tm=tm, tf=tf, nf=nf, seq_tiles=seq_tiles,
                             final_norm=final_norm)
    out, tail = pl.pallas_call(
        kern,
        grid=(m // tm, nf),
        in_specs=specs,
        out_specs=[pl.BlockSpec((tm, dm), lambda i, f: (i, 0)), tail_spec],
        out_shape=[jax.ShapeDtypeStruct((m, dm), F32), tail_shape],
        scratch_shapes=[pltpu.VMEM((tm, dm), BF16), pltpu.VMEM((nf, SUBLANES, tf), F32)],
        compiler_params=_cparams(("arbitrary", "arbitrary")),
        name="conv_ffn",
    )(*args)
    if step:
        conv_new = jnp.stack([conv_prev[:, 1].astype(F32), tail], axis=1)
    else:
        conv_new = tail.reshape(bsz, seq_tiles, SUBLANES, dff)[:, -1, SUBLANES - (CONV_W - 1):]
    return out, conv_new


def _trunk(x, s5_re, s5_im, rw_shift, rw_wkv, gla_s, hg_s, ffn_conv, p):
    bsz, t, dm = x.shape
    h = x.reshape(bsz * t, dm).astype(F32)
    depth = p['norm_mix'].shape[0]
    conv_new = []
    for i in range(depth):
        kind = i % 4
        nw = p['norm_mix'][i]
        if kind == 0:
            h, s5_re, s5_im = _s5_layer(h, bsz, t, s5_re, s5_im, nw, p['s5'], p['s5_c_re'], p['s5_c_im'],
                                        p['s5_d'], p['s5_w_glu'])
        elif kind == 1:
            h, rw_shift, rw_wkv = _rwkv_layer(h, bsz, t, rw_shift, rw_wkv, nw, p['rw'])
        elif kind == 2:
            h, gla_s = _gla_layer(h, bsz, t, gla_s, nw, p['gla'])
        else:
            h, hg_s = _hgrn_layer(h, bsz, t, hg_s, nw, p['hg'], i)
        fin = p['norm_final'] if i == depth - 1 else None
        h, c = _ffn_call(h, bsz, t, ffn_conv[i], p['norm_ffn'][i], p['ffn_w_up'][i], p['ffn_w_gate'][i],
                         p['ffn_w_conv'][i], p['ffn_b_conv'][i], p['ffn_w_down'][i], fin)
        conv_new.append(c)
    return h.reshape(bsz, t, dm), s5_re, s5_im, rw_shift, rw_wkv, gla_s, hg_s, jnp.stack(conv_new)


def _pad_cols(w, n):
    return jnp.pad(w, ((0, 0), (0, n - w.shape[1])))


def _pad_rows(w, n):
    return jnp.pad(w, ((0, n - w.shape[0]), (0, 0)))


def kernel(x_prompt, x_sample, state_s5_re, state_s5_im, state_rwkv_shift, state_rwkv_wkv, state_gla, state_hgrn, state_ffn_conv, norm_mix, norm_ffn, norm_final, s5_lambda_re, s5_lambda_im, s5_log_dt, s5_b_re, s5_b_im, s5_c_re, s5_c_im, s5_d, s5_w_glu, rw_mix, rw_w_r, rw_w_k, rw_w_v, rw_w_o, rw_w0, rw_w1, rw_w2, rw_a0, rw_a1, rw_a2, rw_g1, rw_g2, rw_k_k, rw_k_a, rw_r_k, rw_ln_w, rw_ln_b, gla_w_q, gla_w_k, gla_w_v, gla_w_gk1, gla_w_gk2, gla_b_gk, gla_w_g, gla_norm, gla_w_o, hg_w_q, hg_w_f, hg_w_i, hg_w_g, hg_lb, hg_norm, hg_w_o, ffn_w_up, ffn_w_gate, ffn_w_conv, ffn_b_conv, ffn_w_down):
    bf = lambda w: w.astype(BF16)
    f32 = lambda w: w.astype(F32)
    lora = LANES * pl.cdiv(rw_w1.shape[1], LANES)
    alora = LANES * pl.cdiv(rw_a1.shape[1], LANES)
    grank = LANES * pl.cdiv(gla_w_gk1.shape[1], LANES)
    p = {
        'norm_mix': f32(norm_mix), 'norm_ffn': f32(norm_ffn), 'norm_final': f32(norm_final),
        's5': _s5_params(s5_lambda_re, s5_lambda_im, s5_log_dt, s5_b_re, s5_b_im, s5_c_re, s5_c_im, s5_d),
        's5_c_re': s5_c_re, 's5_c_im': s5_c_im, 's5_d': s5_d, 's5_w_glu': bf(s5_w_glu),
        'rw': dict(mix=_pad_rows(f32(rw_mix), SUBLANES), w_r=bf(rw_w_r), w_k=bf(rw_w_k), w_v=bf(rw_w_v),
                   w_o=bf(rw_w_o), w0=f32(rw_w0), w1=bf(_pad_cols(rw_w1, lora)), w2=bf(_pad_rows(rw_w2, lora)),
                   a0=f32(rw_a0), a1=bf(_pad_cols(rw_a1, alora)), a2=bf(_pad_rows(rw_a2, alora)),
                   g1=bf(rw_g1), g2=bf(rw_g2), k_k=f32(rw_k_k), k_a=f32(rw_k_a), r_k=f32(rw_r_k).reshape(-1),
                   ln_w=f32(rw_ln_w), ln_b=f32(rw_ln_b)),
        'gla': dict(w_q=bf(gla_w_q), w_k=bf(gla_w_k), w_v=bf(gla_w_v), gk1=bf(_pad_cols(gla_w_gk1, grank)),
                    gk2=bf(_pad_rows(gla_w_gk2, grank)), b_gk=f32(gla_b_gk), w_g=bf(gla_w_g), norm=f32(gla_norm),
                    w_o=bf(gla_w_o)),
        'hg': dict(w_q=bf(hg_w_q), w_f=bf(hg_w_f), w_i=bf(hg_w_i), w_g=bf(hg_w_g), lb=f32(hg_lb),
                   norm=f32(hg_norm), w_o=bf(hg_w_o)),
        'ffn_w_up': bf(ffn_w_up), 'ffn_w_gate': bf(ffn_w_gate), 'ffn_w_conv': ffn_w_conv,
        'ffn_b_conv': ffn_b_conv, 'ffn_w_down': bf(ffn_w_down),
    }
    nb, _, dm = x_prompt.shape
    depth = norm_mix.shape[0]
    dff = ffn_w_up.shape[-1]
    z_s5 = jnp.zeros((nb,) + state_s5_re.shape[1:], F32)
    outs_p = _trunk(x_prompt, z_s5, z_s5, jnp.zeros((nb, dm), F32),
                    jnp.zeros((nb,) + state_rwkv_wkv.shape[1:], F32), jnp.zeros((nb,) + state_gla.shape[1:], F32),
                    jnp.zeros((nb,) + state_hgrn.shape[1:], F32), jnp.zeros((depth, nb, CONV_W - 1, dff), F32), p)
    outs_s = _trunk(x_sample, f32(state_s5_re), f32(state_s5_im), state_rwkv_shift, f32(state_rwkv_wkv),
                    f32(state_gla), f32(state_hgrn), state_ffn_conv, p)
    return (outs_p[0], outs_s[0]) + tuple(outs_p[1:]) + tuple(outs_s[1:])
```

```python
import functools
import math

import jax
import jax.numpy as jnp
from jax import lax
from jax.experimental import pallas as pl
from jax.experimental.pallas import tpu as pltpu

F32 = jnp.float32
BF16 = jnp.bfloat16

NORM_EPS = 1e-6
RW_LN_EPS = 64e-5
RW_HEAD = 64
S5_GROUP = 16
S5_STATE = 64
S5_CHUNK = 16
GLA_HEADS = 4
GLA_GATE_TEMP = 16.0
HG_EXPAND = 128
CONV_W = 3
LA_CHUNK = 64
RW_CHUNK = 64
STEP_PAD = 16
LANES = 128
SUBLANES = 8
VMEM_LIMIT = 56 * 1024 * 1024


def _cparams(sem):
    return pltpu.CompilerParams(dimension_semantics=sem, vmem_limit_bytes=VMEM_LIMIT)


def _bdot(a, b):
    return jnp.dot(a.astype(BF16), b.astype(BF16), preferred_element_type=F32)


def _bdot_nt(a, b):
    return lax.dot_general(a.astype(BF16), b.astype(BF16), (((1,), (1,)), ((), ())),
                           preferred_element_type=F32)


def _bdot_tn(a, b):
    return lax.dot_general(a.astype(BF16), b.astype(BF16), (((0,), (0,)), ((), ())),
                           preferred_element_type=F32)


def _split3(x):
    hi = x.astype(BF16)
    r = x - hi.astype(F32)
    mid = r.astype(BF16)
    lo = (r - mid.astype(F32)).astype(BF16)
    return hi, mid, lo


def _xdot_l(m, x):
    hi, mid, lo = _split3(x)
    d = lambda p: jnp.dot(m, p, preferred_element_type=F32)
    return d(hi) + d(mid) + d(lo)


def _xdot_r(x, m):
    hi, mid, lo = _split3(x)
    d = lambda p: jnp.dot(p, m, preferred_element_type=F32)
    return d(hi) + d(mid) + d(lo)


def _rms(x, g):
    y = x * lax.rsqrt(jnp.mean(x * x, axis=-1, keepdims=True) + NORM_EPS)
    return y * g


def _iota(shape, dim):
    return lax.broadcasted_iota(jnp.int32, shape, dim)


def _rmsnorm_kernel(x_ref, g_ref, o_ref):
    o_ref[...] = _rms(x_ref[...], g_ref[...])


def _rmsnorm_call(x, g):
    r, d = x.shape
    tr = 512 if r % 512 == 0 else r
    return pl.pallas_call(
        _rmsnorm_kernel,
        grid=(r // tr,),
        in_specs=[pl.BlockSpec((tr, d), lambda i: (i, 0)), pl.BlockSpec((1, d), lambda i: (0, 0))],
        out_specs=pl.BlockSpec((tr, d), lambda i: (i, 0)),
        out_shape=jax.ShapeDtypeStruct((r, d), F32),
        compiler_params=_cparams(("parallel",)),
        name="rmsnorm",
    )(x, g.reshape(1, d))


def _proj_call(body, *, m, n, tm, tn, row_ins, full_ins, col_ins, rc_ins, out_dtypes, scratch, name):
    args, specs = [], []
    for a in row_ins:
        if isinstance(a, tuple):
            args.append(a[0]); specs.append(a[1])
        else:
            args.append(a); specs.append(pl.BlockSpec((tm, a.shape[1]), lambda i, j: (i, 0)))
    for a in full_ins:
        args.append(a); specs.append(pl.BlockSpec(a.shape, lambda i, j, nd=a.ndim: (0,) * nd))
    for a in col_ins:
        args.append(a); specs.append(pl.BlockSpec((a.shape[0], tn), lambda i, j: (0, j)))
    for a in rc_ins:
        args.append(a); specs.append(pl.BlockSpec((tm, tn), lambda i, j: (i, j)))
    nr, nf, nc, nrc, no = len(row_ins), len(full_ins), len(col_ins), len(rc_ins), len(out_dtypes)

    def kernel(*refs):
        p = 0
        groups = []
        for cnt in (nr, nf, nc, nrc, no):
            groups.append(refs[p:p + cnt]); p += cnt
        body(*groups, refs[p:])

    outs = pl.pallas_call(
        kernel,
        grid=(m // tm, n // tn),
        in_specs=specs,
        out_specs=[pl.BlockSpec((tm, tn), lambda i, j: (i, j)) for _ in out_dtypes],
        out_shape=[jax.ShapeDtypeStruct((m, n), dt) for dt in out_dtypes],
        scratch_shapes=scratch,
        compiler_params=_cparams(("parallel", "arbitrary")),
        name=name,
    )(*args)
    return outs


def _row(v):
    return v.reshape(1, -1)


def _tiles(m, t):
    if t == 1:
        return 512 if m % 512 == 0 else m
    return min(512, t)


def _s5_params(lam_re, lam_im, log_dt, b_re, b_im, c_re, c_im, d):
    hp = lax.Precision.HIGHEST
    g, n = lam_re.shape
    p = b_re.shape[-1]
    cl = S5_CHUNK
    lr = jnp.minimum(lam_re.astype(F32), -1e-4)
    li = lam_im.astype(F32)
    dt = jnp.exp(log_dt.astype(F32))[:, None]
    mag = jnp.exp(lr * dt)
    ab_re = mag * jnp.cos(li * dt)
    ab_im = mag * jnp.sin(li * dt)
    den = lr * lr + li * li
    f_re = ((ab_re - 1.0) * lr + ab_im * li) / den
    f_im = (ab_im * lr - (ab_re - 1.0) * li) / den
    br, bi = b_re.astype(F32), b_im.astype(F32)
    bb_re = f_re[..., None] * br - f_im[..., None] * bi
    bb_im = f_re[..., None] * bi + f_im[..., None] * br
    cr, ci = c_re.astype(F32), c_im.astype(F32)

    pr, pi = [jnp.ones_like(ab_re)], [jnp.zeros_like(ab_im)]
    for _ in range(cl):
        pr.append(pr[-1] * ab_re - pi[-1] * ab_im)
        pi.append(pr[-2] * ab_im + pi[-1] * ab_re)
    pw_re = jnp.stack(pr)
    pw_im = jnp.stack(pi)
    ajb_re = pw_re[:cl, :, :, None] * bb_re - pw_im[:cl, :, :, None] * bb_im
    ajb_im = pw_re[:cl, :, :, None] * bb_im + pw_im[:cl, :, :, None] * bb_re
    kj = (jnp.einsum('gpn,jgnq->jgpq', cr, ajb_re, precision=hp)
          - jnp.einsum('gpn,jgnq->jgpq', ci, ajb_im, precision=hp))
    kq = kj.transpose(1, 3, 0, 2).astype(BF16)
    mt = jnp.stack([jnp.pad(kq[:, :, :cl - s, :], ((0, 0), (0, 0), (s, 0), (0, 0))) for s in range(cl)],
                   axis=1).reshape(g, cl * p, cl * p)
    w_re = ajb_re[::-1].transpose(1, 0, 3, 2).reshape(g, cl * p, n)
    w_im = ajb_im[::-1].transpose(1, 0, 3, 2).reshape(g, cl * p, n)
    wm = jnp.concatenate([w_re, w_im], axis=-1)
    ca_re = cr[None] * pw_re[1:, :, None, :] - ci[None] * pw_im[1:, :, None, :]
    ca_im = cr[None] * pw_im[1:, :, None, :] + ci[None] * pw_re[1:, :, None, :]
    v_re = ca_re.transpose(1, 3, 0, 2).reshape(g, n, cl * p)
    v_im = -ca_im.transpose(1, 3, 0, 2).reshape(g, n, cl * p)
    vm = jnp.concatenate([v_re, v_im], axis=1)
    ar, ai = pw_re[cl], pw_im[cl]
    pcs, qcs = [], []
    for _ in range(SUBLANES):
        pcs.append(jnp.concatenate([ar, ar], axis=-1))
        qcs.append(jnp.concatenate([-ai, ai], axis=-1))
        ar, ai = ar * ar - ai * ai, 2.0 * ar * ai
    pc = jnp.stack(pcs, axis=1)
    qc = jnp.stack(qcs, axis=1)
    return dict(ab_re=ab_re, ab_im=ab_im, bb_re=bb_re, bb_im=bb_im, mt=mt.astype(BF16), wm=wm.astype(BF16),
                vm=vm.astype(BF16), pc=pc, qc=qc)


def _s5_seq_kernel(x_ref, mt_ref, w_ref, vm_ref, pc_ref, qc_ref, d_ref, h0_ref, z_ref, hf_ref, *, gb, nc):
    half = S5_STATE
    cl, p = S5_CHUNK, S5_GROUP
    per = LANES // p
    rid = _iota((nc, 1), 0)
    lane_blk = _iota((1, LANES), 1) // p
    nlev = int(math.log2(nc))
    xs = [x_ref[0, :, s, :] for s in range(cl)]
    groups = range(gb)

    def to_chunk(g):
        pieces = []
        for hf in range(cl // per):
            acc = None
            for j in range(per):
                src = xs[hf * per + j]
                sh = ((j - g) * p) % LANES
                src = pltpu.roll(src, sh, 1) if sh else src
                acc = src if acc is None else jnp.where(lane_blk == j, src, acc)
            pieces.append(acc)
        return jnp.concatenate(pieces, axis=1).astype(BF16)

    ub = [to_chunk(g) for g in groups]
    y = [jnp.dot(u, mt_ref[g], preferred_element_type=F32) for g, u in zip(groups, ub)]
    x = [jnp.dot(u, w_ref[g], preferred_element_type=F32) for g, u in zip(groups, ub)]
    hprev = []
    for g in groups:
        pc = pc_ref[g]
        qc = qc_ref[g]
        h0 = h0_ref[0, g]
        h0c = pc[0:1] * h0 + qc[0:1] * pltpu.roll(h0, half, 1)
        xg = x[g] + jnp.where(rid == 0, h0c, 0.0)
        for lv in range(nlev):
            sh = 1 << lv
            xsft = jnp.where(rid >= sh, pltpu.roll(xg, sh, 0), 0.0)
            xg = xg + pc[lv:lv + 1] * xsft + qc[lv:lv + 1] * pltpu.roll(xsft, half, 1)
        hf_ref[0, g] = xg[nc - 1:nc, :]
        hprev.append(jnp.where(rid == 0, h0, pltpu.roll(xg, 1, 0)).astype(BF16))
    y = [yy + jnp.dot(hp, vm_ref[g], preferred_element_type=F32) for g, yy, hp in zip(groups, y, hprev)]
    for t in range(cl):
        hf, j = divmod(t, per)
        acc = None
        for g in groups:
            src = y[g][:, hf * LANES:(hf + 1) * LANES]
            sh = ((g - j) * p) % LANES
            src = pltpu.roll(src, sh, 1) if sh else src
            acc = src if acc is None else jnp.where(lane_blk == g, src, acc)
        z_ref[0, :, t, :] = jax.nn.gelu(acc + d_ref[...] * xs[t])


def _s5_seq_call(x4, prm, d, h0):
    b, nc, cl, dm = x4.shape
    g = dm // S5_GROUP
    w = cl * S5_GROUP
    n2 = 2 * S5_STATE
    gb = LANES // S5_GROUP
    kern = functools.partial(_s5_seq_kernel, gb=gb, nc=nc)
    gspec = lambda shape: pl.BlockSpec((gb,) + shape, lambda gi, bi: (gi, 0, 0))
    xspec = pl.BlockSpec((1, nc, cl, LANES), lambda gi, bi: (bi, 0, 0, gi))
    return pl.pallas_call(
        kern,
        grid=(g // gb, b),
        in_specs=[xspec,
                  gspec((w, w)), gspec((w, n2)), gspec((n2, w)), gspec((SUBLANES, n2)), gspec((SUBLANES, n2)),
                  pl.BlockSpec((1, LANES), lambda gi, bi: (0, gi)),
                  pl.BlockSpec((1, gb, 1, n2), lambda gi, bi: (bi, gi, 0, 0))],
        out_specs=[xspec, pl.BlockSpec((1, gb, 1, n2), lambda gi, bi: (bi, gi, 0, 0))],
        out_shape=[jax.ShapeDtypeStruct(x4.shape, F32), jax.ShapeDtypeStruct((b, g, 1, n2), F32)],
        compiler_params=_cparams(("parallel", "arbitrary")),
        name="s5_seq",
    )(x4, prm['mt'], prm['wm'], prm['vm'], prm['pc'], prm['qc'], _row(d.astype(F32)), h0)


def _s5_step_kernel(u_ref, hr_ref, hi_ref, ar_ref, ai_ref, bdr_ref, bdi_ref, cdr_ref, cdi_ref, d_ref,
                    z_ref, hro_ref, hio_ref):
    u = u_ref[...]
    ar, ai = ar_ref[...], ai_ref[...]
    hr, hi = hr_ref[...], hi_ref[...]
    hr2 = ar * hr - ai * hi + _bdot(u, bdr_ref[0])
    hi2 = ar * hi + ai * hr + _bdot(u, bdi_ref[0])
    y = _bdot(hr2, cdr_ref[0]) - _bdot(hi2, cdi_ref[0]) + d_ref[...] * u
    z_ref[...] = jax.nn.gelu(y)
    hro_ref[...] = hr2
    hio_ref[...] = hi2


def _s5_step_call(u, h_re, h_im, prm, c_re, c_im, d):
    b, dm = u.shape
    g, n = prm['ab_re'].shape
    p = S5_GROUP
    gpb = LANES // p
    nb = g // gpb
    eye = jnp.eye(gpb, dtype=F32)
    bd = lambda bb: jnp.einsum('kgnp,gh->kgphn', bb.reshape(nb, gpb, n, p), eye).reshape(nb, gpb * p, gpb * n)
    cd = lambda cc: jnp.einsum('kgpn,gh->kgnhp', cc.astype(F32).reshape(nb, gpb, p, n), eye).reshape(
        nb, gpb * n, gpb * p)
    bdr, bdi = bd(prm['bb_re']).astype(BF16), bd(prm['bb_im']).astype(BF16)
    cdr, cdi = cd(c_re).astype(BF16), cd(c_im).astype(BF16)
    wn = gpb * n
    cspec = lambda wd: pl.BlockSpec((b, wd), lambda k: (0, k))
    rspec = lambda wd: pl.BlockSpec((1, wd), lambda k: (0, k))
    z, hr2, hi2 = pl.pallas_call(
        _s5_step_kernel,
        grid=(nb,),
        in_specs=[cspec(LANES), cspec(wn), cspec(wn), rspec(wn), rspec(wn),
                  pl.BlockSpec((1, LANES, wn), lambda k: (k, 0, 0)), pl.BlockSpec((1, LANES, wn), lambda k: (k, 0, 0)),
                  pl.BlockSpec((1, wn, LANES), lambda k: (k, 0, 0)), pl.BlockSpec((1, wn, LANES), lambda k: (k, 0, 0)),
                  rspec(LANES)],
        out_specs=[cspec(LANES), cspec(wn), cspec(wn)],
        out_shape=[jax.ShapeDtypeStruct((b, dm), F32), jax.ShapeDtypeStruct((b, g * n), F32),
                   jax.ShapeDtypeStruct((b, g * n), F32)],
        compiler_params=_cparams(("parallel",)),
        name="s5_step",
    )(u, h_re.reshape(b, g * n), h_im.reshape(b, g * n), prm['ab_re'].reshape(1, g * n),
      prm['ab_im'].reshape(1, g * n), bdr, bdi, cdr, cdi, _row(d.astype(F32)))
    return z, hr2.reshape(b, g, n), hi2.reshape(b, g, n)


def _glu_body(row, full, col, rc, out, scr):
    (z_ref,) = row
    (w_ref,) = col
    zt_ref, h_ref = rc
    (zb,) = scr

    @pl.when(pl.program_id(1) == 0)
    def _():
        zb[...] = z_ref[...].astype(BF16)

    zt = zt_ref[...]
    gate = jax.nn.sigmoid(jnp.dot(zb[...], w_ref[...], preferred_element_type=F32))
    out[0][...] = h_ref[...] + zt * gate


def _s5_layer(h, bsz, t, st_re, st_im, norm_w, prm, c_re, c_im, d, w_glu):
    m, dm = h.shape
    g, n, p = dm // S5_GROUP, S5_STATE, S5_GROUP
    xn = _rmsnorm_call(h, norm_w)
    if t == 1:
        z, s_re, s_im = _s5_step_call(xn, st_re, st_im, prm, c_re, c_im, d)
    else:
        cl = S5_CHUNK
        nc = t // cl
        h0 = jnp.concatenate([st_re, st_im], axis=-1).reshape(bsz, g, 1, 2 * n)
        z4, hf = _s5_seq_call(xn.reshape(bsz, nc, cl, dm), prm, d, h0)
        z = z4.reshape(m, dm)
        s_re, s_im = hf[:, :, 0, :n], hf[:, :, 0, n:]
    tm = _tiles(m, t)
    (h_new,) = _proj_call(_glu_body, m=m, n=dm, tm=tm, tn=512, row_ins=[z], full_ins=[], col_ins=[w_glu],
                          rc_ins=[z, h], out_dtypes=[F32], scratch=[pltpu.VMEM((tm, dm), BF16)], name="s5_glu")
    return h_new, s_re, s_im


def _rwkv_proj_body(row, full, col, rc, out, scr, *, tm, seq_tiles, step):
    h_ref, prev_ref = row
    nw_ref, mix_ref, w1_ref, a1_ref, g1_ref, bd_ref = full
    wr, wk, wv, w2, a2, g2, w0, a0, kk_ref, ka_ref = col
    r_o, lw_o, k_o, v_o, a_o, b_o, g_o = out
    xr_s, xk_s, xv_s, tw_s, ta_s, tg_s = scr
    i = pl.program_id(0)

    @pl.when(pl.program_id(1) == 0)
    def _():
        nw = nw_ref[...]
        xn = _rms(h_ref[...], nw)
        if step:
            xprev = prev_ref[...]
        else:
            last = _rms(prev_ref[...], nw)[SUBLANES - 1:SUBLANES, :]
            last = jnp.where(i % seq_tiles == 0, 0.0, last)
            xprev = jnp.where(_iota((tm, 1), 0) == 0, last, pltpu.roll(xn, 1, 0))
        xx = xprev - xn
        mix = mix_ref[...]
        xr_s[...] = (xn + xx * mix[0:1]).astype(BF16)
        xk_s[...] = (xn + xx * mix[2:3]).astype(BF16)
        xv_s[...] = (xn + xx * mix[3:4]).astype(BF16)
        tw_s[...] = jnp.tanh(_bdot(xn + xx * mix[1:2], w1_ref[...])).astype(BF16)
        ta_s[...] = _bdot(xn + xx * mix[4:5], a1_ref[...]).astype(BF16)
        tg_s[...] = jax.nn.sigmoid(_bdot(xn + xx * mix[5:6], g1_ref[...])).astype(BF16)

    dot = lambda a, b: jnp.dot(a[...], b[...], preferred_element_type=F32)
    r = dot(xr_s, wr)
    k = dot(xk_s, wk)
    v = dot(xv_s, wv)
    lw = -jax.nn.sigmoid(w0[...] + dot(tw_s, w2)) * math.exp(-0.5)
    a = jax.nn.sigmoid(a0[...] + dot(ta_s, a2))
    kk = k * kk_ref[...]
    ss = _bdot(kk * kk, bd_ref[...])
    kk = kk / jnp.maximum(jnp.sqrt(ss), 1e-12)
    r_o[...] = r
    lw_o[...] = lw
    k_o[...] = k * (1.0 + (a - 1.0) * ka_ref[...])
    v_o[...] = v
    a_o[...] = -kk
    b_o[...] = kk * a
    g_o[...] = dot(tg_s, g2)


def _rwkv_chunk_kernel(r_ref, lw_ref, k_ref, v_ref, a_ref, b_ref, gate_ref, lnw_ref, lnb_ref, rk_ref, s0_ref,
                       o_ref, sout_ref, p_scr, rp_scr, o0_scr, m_scr, d_scr, g_scr, *, npair, cl, nch):
    hd = RW_HEAD
    pw = 2 * hd
    lane = _iota((1, pw), 1)
    head0 = lane < hd
    bdmask = (_iota((pw, pw), 0) // hd) == (_iota((pw, pw), 1) // hd)
    e_dup = ((_iota((hd, pw), 1) % hd) == _iota((hd, pw), 0)).astype(BF16)
    e_fold = ((_iota((pw, hd), 0) % hd) == _iota((pw, hd), 1)).astype(BF16)
    ones_bd = bdmask.astype(BF16)
    avg = (bdmask.astype(F32) * (1.0 / hd)).astype(BF16)
    for p in range(npair):
        p_scr[p] = jnp.where(bdmask, _xdot_r(s0_ref[0, p], e_dup), 0.0)
    s2 = 2 * cl
    rr = _iota((s2, s2), 0)
    cc = _iota((s2, s2), 1)
    same = (rr // cl) == (cc // cl)
    strict = same & ((rr % cl) > (cc % cl))
    incl = same & ((rr % cl) >= (cc % cl))
    eye_s = (rr == cc).astype(F32)
    rid = _iota((cl, 1), 0)
    zeros_s = jnp.zeros((s2, pw), F32)
    nround = int(math.log2(cl)) - 1

    def stack(x):
        return jnp.concatenate([jnp.where(head0, x, 0.0), jnp.where(head0, 0.0, x)], axis=0)

    ug = 4 if nch % 4 == 0 else 1
    cat = jnp.concatenate

    def tile(ref, c, p):
        return ref[pl.ds(pl.multiple_of(c * cl, cl), cl), p * pw:(p + 1) * pw]

    def phase_a(it, carry):
        ch = [(it * ug + j, p) for j in range(ug) for p in range(npair)]
        ld = lambda ref: [tile(ref, c, p) for c, p in ch]
        lw = ld(lw_ref)
        cs = lw
        sh = 1
        while sh < cl:
            cs = [x + jnp.where(rid >= sh, pltpu.roll(x, sh, 0), 0.0) for x in cs]
            sh *= 2
        gam = [jnp.exp(x) for x in cs]
        ginv = [jnp.exp(-x) for x in cs]
        ats = [stack(a * jnp.exp(x - l)) for a, x, l in zip(ld(a_ref), cs, lw)]
        rts = [stack(r * g) for r, g in zip(ld(r_ref), gam)]
        x1s = [cat([a, r], axis=0).astype(BF16) for a, r in zip(ats, rts)]
        x2s = [cat([stack(b * g), stack(k * g)], axis=0).astype(BF16)
               for b, k, g in zip(ld(b_ref), ld(k_ref), ginv)]
        gs = [_bdot_nt(x1, x2) for x1, x2 in zip(x1s, x2s)]
        nmat = [jnp.where(strict, g[:s2, :s2], 0.0) for g in gs]
        aak = [jnp.where(strict, g[:s2, s2:], 0.0) for g in gs]
        arb = [jnp.where(incl, g[s2:, :s2], 0.0) for g in gs]
        ark = [jnp.where(incl, g[s2:, s2:], 0.0) for g in gs]
        vs = [stack(v) for v in ld(v_ref)]
        akv = [_bdot(x, v) for x, v in zip(aak, vs)]
        rkv = [_bdot(x, v) for x, v in zip(ark, vs)]
        tinv = [eye_s + n for n in nmat]
        pk = nmat
        for _ in range(nround):
            pk = [_bdot(x, x) for x in pk]
            tinv = [t + _bdot(x, t) for x, t in zip(pk, tinv)]
        tz = [_bdot(t, cat([x, a], axis=1)) for t, x, a in zip(tinv, akv, ats)]
        u0s = [x[:, :pw] for x in tz]
        aps = [x[:, pw:] for x in tz]
        z = [_bdot(x, cat([a, u], axis=1)) for x, a, u in zip(arb, aps, u0s)]
        md = [_bdot_tn(cat([cat([a, u], axis=1), cat([zeros_s, v], axis=1)], axis=0), x2)
              for a, u, v, x2 in zip(aps, u0s, vs, x2s)]
        for i, (c, p) in enumerate(ch):
            rps = rts[i] + z[i][:, :pw]
            o0s = z[i][:, pw:] + rkv[i]
            rp_scr[c, p] = (rps[:cl] + rps[cl:]).astype(BF16)
            o0_scr[c, p] = o0s[:cl] + o0s[cl:]
            gl = gam[i][cl - 1:cl, :]
            m_scr[c, p] = (md[i][:pw] * gl).astype(BF16)
            d_scr[c, p] = md[i][pw:] * gl
            g_scr[c, p] = gl
        return carry

    lax.fori_loop(0, nch // ug, phase_a, 0)

    def phase_b(it, carry):
        cs = [it * ug + j for j in range(ug)]
        prs = range(npair)
        bonus = [[_bdot(tile(r_ref, c, p) * tile(k_ref, c, p) * rk_ref[:, p * pw:(p + 1) * pw], ones_bd)
                  * tile(v_ref, c, p) for p in prs] for c in cs]
        st = [p_scr[p] for p in prs]
        outs, mus, dlts, vrs = [], [], [], []
        for j in range(ug + 2):
            if j < ug:
                c = cs[j]
                new = [st[p] * g_scr[c, p] + _bdot(st[p], m_scr[c, p]) + d_scr[c, p] for p in prs]
                outs.append([_bdot_nt(rp_scr[c, p], st[p]) + o0_scr[c, p] for p in prs])
                st = new
            if 1 <= j <= ug:
                mus.append([_bdot(o, avg) for o in outs[j - 1]])
            if 2 <= j:
                dl = [o - m for o, m in zip(outs[j - 2], mus[j - 2])]
                dlts.append(dl)
                vrs.append([_bdot(d * d, avg) for d in dl])
        for p in prs:
            p_scr[p] = st[p]
        for j, c in enumerate(cs):
            for p in prs:
                sl = slice(p * pw, (p + 1) * pw)
                on = dlts[j][p] * lax.rsqrt(vrs[j][p] + RW_LN_EPS) * lnw_ref[:, sl] + lnb_ref[:, sl]
                o_ref[pl.ds(pl.multiple_of(c * cl, cl), cl), sl] = (
                    (on + bonus[j][p]) * tile(gate_ref, c, p)).astype(o_ref.dtype)
        return carry

    lax.fori_loop(0, nch // ug, phase_b, 0)
    for p in range(npair):
        sout_ref[0, p] = _xdot_r(jnp.where(bdmask, p_scr[p], 0.0), e_fold)


def _rwkv_chunk_call(r, lw, k, v, a, b, gate, ln_w, ln_b, r_k, s0, bsz, t, cl):
    m, dm = r.shape
    hd = RW_HEAD
    pw = 2 * hd
    npairs = dm // pw
    npair = 2
    nch = t // cl
    s0p = s0.reshape(bsz, npairs, pw, hd)
    kern = functools.partial(_rwkv_chunk_kernel, npair=npair, cl=cl, nch=nch)
    tspec = pl.BlockSpec((t, npair * pw), lambda bi, pi: (bi, pi))
    vspec = pl.BlockSpec((1, npair * pw), lambda bi, pi: (0, pi))
    sspec = pl.BlockSpec((1, npair, pw, hd), lambda bi, pi: (bi, pi, 0, 0))
    o, s = pl.pallas_call(
        kern,
        grid=(bsz, npairs // npair),
        in_specs=[tspec] * 7 + [vspec] * 3 + [sspec],
        out_specs=[tspec, sspec],
        out_shape=[jax.ShapeDtypeStruct((m, dm), BF16), jax.ShapeDtypeStruct((bsz, npairs, pw, hd), F32)],
        scratch_shapes=[pltpu.VMEM((npair, pw, pw), F32),
                        pltpu.VMEM((nch, npair, cl, pw), BF16), pltpu.VMEM((nch, npair, cl, pw), F32),
                        pltpu.VMEM((nch, npair, pw, pw), BF16), pltpu.VMEM((nch, npair, pw, pw), F32),
                        pltpu.VMEM((nch, npair, 1, pw), F32)],
        compiler_params=_cparams(("parallel", "parallel")),
        name="rwkv_chunk",
    )(r, lw, k, v, a, b, gate, _row(ln_w), _row(ln_b), _row(r_k), s0p)
    return o, s.reshape(bsz, dm // hd, hd, hd)


def _rwkv_step_kernel(r_ref, lw_ref, k_ref, v_ref, a_ref, b_ref, gate_ref, lnw_ref, lnb_ref, rk_ref, s0_ref,
                      o_ref, sout_ref, oacc, *, nh):
    rid = _iota((STEP_PAD, 1), 0)
    hg = 8

    def rows6(hi, mid, lo, order):
        terms = (hi.astype(F32), mid.astype(F32), lo.astype(F32))
        out = jnp.zeros((STEP_PAD, hi.shape[-1]), F32)
        for i, t in enumerate(order):
            out = jnp.where(rid == i, terms[t], out)
        return out.astype(BF16)

    def group(gi, carry):
        hs = [gi * hg + j for j in range(hg)]
        rows = lambda ref: [ref[0, pl.ds(h, 1), :] for h in hs]
        st = [s0_ref[0, h] for h in hs]
        sa = [jnp.sum(s * a, axis=-1, keepdims=True) for s, a in zip(st, rows(a_ref))]
        vx = [rows6(*_split3(v), (0, 0, 1, 0, 1, 2)) for v in rows(v_ref)]
        kx = [rows6(*_split3(k), (0, 1, 0, 2, 1, 0)) for k in rows(k_ref)]
        vk = [lax.dot_general(x, y, (((0,), (0,)), ((), ())), preferred_element_type=F32) for x, y in zip(vx, kx)]
        sn = [s * jnp.exp(w) + c * b + o for s, w, c, b, o in zip(st, rows(lw_ref), sa, rows(b_ref), vk)]
        for h, s in zip(hs, sn):
            sout_ref[0, h] = s
        os_ = [_bdot_nt(jnp.where(rid == 0, r, 0.0), s) for r, s in zip(rows(r_ref), sn)]
        for h, o in zip(hs, os_):
            oacc[pl.ds(h, 1), :] = o[0:1]
        return carry

    lax.fori_loop(0, nh // hg, group, 0)
    o = oacc[...]
    mu = jnp.mean(o, axis=-1, keepdims=True)
    dlt = o - mu
    var = jnp.mean(dlt * dlt, axis=-1, keepdims=True)
    on = dlt * lax.rsqrt(var + RW_LN_EPS) * lnw_ref[...] + lnb_ref[...]
    bonus = jnp.sum(r_ref[0] * k_ref[0] * rk_ref[...], axis=-1, keepdims=True) * v_ref[0]
    o_ref[0] = ((on + bonus) * gate_ref[0]).astype(o_ref.dtype)


def _rwkv_step_call(r, lw, k, v, a, b, gate, ln_w, ln_b, r_k, s0):
    bsz, dm = r.shape
    hd = RW_HEAD
    nh = dm // hd
    heads = lambda x: x.reshape(bsz, nh, hd)
    hspec = pl.BlockSpec((1, nh, hd), lambda bi: (bi, 0, 0))
    pspec = pl.BlockSpec((nh, hd), lambda bi: (0, 0))
    sspec = pl.BlockSpec((1, nh, hd, hd), lambda bi: (bi, 0, 0, 0))
    o, s = pl.pallas_call(
        functools.partial(_rwkv_step_kernel, nh=nh),
        grid=(bsz,),
        in_specs=[hspec] * 7 + [pspec] * 3 + [sspec],
        out_specs=[hspec, sspec],
        out_shape=[jax.ShapeDtypeStruct((bsz, nh, hd), BF16), jax.ShapeDtypeStruct(s0.shape, F32)],
        scratch_shapes=[pltpu.VMEM((nh, hd), F32)],
        compiler_params=_cparams(("parallel",)),
        name="rwkv_step",
    )(*[heads(x) for x in (r, lw, k, v, a, b, gate)], ln_w.reshape(nh, hd), ln_b.reshape(nh, hd),
      r_k.reshape(nh, hd), s0)
    return o.reshape(bsz, dm), s


def _outproj_body(row, full, col, rc, out, scr):
    out[0][...] = rc[0][...] + jnp.dot(row[0][...].astype(BF16), col[0][...], preferred_element_type=F32)


def _outproj_call(a, w, h, t, name):
    m, kdim = a.shape
    n = w.shape[1]
    (o,) = _proj_call(_outproj_body, m=m, n=n, tm=_tiles(m, t), tn=512, row_ins=[a], full_ins=[], col_ins=[w],
                      rc_ins=[h], out_dtypes=[F32], scratch=[], name=name)
    return o


def _rwkv_layer(h, bsz, t, shift, wkv, norm_w, wts):
    m, dm = h.shape
    tm = _tiles(m, t)
    step = t == 1
    if step:
        prev = (shift.astype(F32), pl.BlockSpec((tm, dm), lambda i, j: (i, 0)))
        seq_tiles = 1
    else:
        seq_tiles = t // tm
        prev = (h, pl.BlockSpec((SUBLANES, dm), lambda i, j: (jnp.maximum(i * (tm // SUBLANES) - 1, 0), 0)))
    tn = 256
    bd = (jnp.arange(tn)[:, None] // RW_HEAD == jnp.arange(tn)[None, :] // RW_HEAD).astype(BF16)
    body = functools.partial(_rwkv_proj_body, tm=tm, seq_tiles=seq_tiles, step=step)
    lora = wts['w1'].shape[1]
    glora = wts['g1'].shape[1]
    r, lw, k, v, a, b, g = _proj_call(
        body, m=m, n=dm, tm=tm, tn=tn, row_ins=[h, prev],
        full_ins=[_row(norm_w), wts['mix'], wts['w1'], wts['a1'], wts['g1'], bd],
        col_ins=[wts['w_r'], wts['w_k'], wts['w_v'], wts['w2'], wts['a2'], wts['g2'],
                 _row(wts['w0']), _row(wts['a0']), _row(wts['k_k']), _row(wts['k_a'])],
        rc_ins=[], out_dtypes=[F32] * 7,
        scratch=[pltpu.VMEM((tm, dm), BF16)] * 3 + [pltpu.VMEM((tm, lora), BF16)] * 2
        + [pltpu.VMEM((tm, glora), BF16)],
        name="rwkv_proj")
    if step:
        shift_new = _rmsnorm_call(h, norm_w)
        og, s_new = _rwkv_step_call(r, lw, k, v, a, b, g, wts['ln_w'], wts['ln_b'], wts['r_k'], wkv)
    else:
        shift_new = _rmsnorm_call(h.reshape(bsz, t, dm)[:, -1], norm_w)
        og, s_new = _rwkv_chunk_call(r, lw, k, v, a, b, g, wts['ln_w'], wts['ln_b'], wts['r_k'], wkv, bsz, t,
                                     RW_CHUNK)
    return _outproj_call(og, wts['w_o'], h, t, "rwkv_out"), shift_new, s_new


def _la_chunk_kernel(q_ref, k_ref, v_ref, g_ref, gate_ref, nw_ref, s0_ref, o_ref, sout_ref, s_scr,
                     *, hb, dk, dv, cl, nch):
    s_scr[...] = s0_ref[0]
    rowi = _iota((cl, cl), 0)
    coli = _iota((cl, cl), 1)
    tri = (rowi >= coli).astype(BF16)
    levels = [1 << i for i in range(int(math.log2(cl)))]
    masks = {s: ((rowi // s) % 2 == 1) & ((coli // s) == (rowi // s) - 1) for s in levels}
    rid = _iota((cl, 1), 0)
    nw = nw_ref[...]
    ug = max(1, 8 // hb)
    if nch % ug:
        ug = 1

    def rows(c):
        return pl.ds(pl.multiple_of(c * cl, cl), cl)

    def group(it, carry):
        ch = [(it * ug + j, h) for j in range(ug) for h in range(hb)]
        ldk = lambda ref: [ref[rows(c), h * dk:(h + 1) * dk] for c, h in ch]
        ldv = lambda ref: [ref[rows(c), h * dv:(h + 1) * dv] for c, h in ch]
        q, k, v = ldk(q_ref), ldk(k_ref), ldv(v_ref)
        cum = [_xdot_l(tri, g) for g in ldk(g_ref)]
        att = [jnp.zeros((cl, cl), F32) for _ in ch]
        ends = cum
        for s in levels:
            prev = [jnp.where(rid >= s, pltpu.roll(e, s, 0), 0.0) for e in ends]
            a_s = [_bdot_nt(qq * jnp.exp(c - p), kk * jnp.exp(e - c))
                   for qq, kk, c, p, e in zip(q, k, cum, prev, ends)]
            att = [a + jnp.where(masks[s], x, 0.0) for a, x in zip(att, a_s)]
            if 2 * s < cl:
                ends = [jnp.where((rid % (2 * s)) >= s, e, pltpu.roll(e, cl - s, 0)) for e in ends]
        o_in = [_bdot(a, vv) + jnp.sum(qq * kk, axis=-1, keepdims=True) * vv
                for a, vv, qq, kk in zip(att, v, q, k)]
        last = [c[cl - 1:cl, :] for c in cum]
        upd = [_bdot_tn(kk * jnp.exp(l - c), vv) for kk, l, c, vv in zip(k, last, cum, v)]
        dec = [jnp.exp(jnp.transpose(jnp.broadcast_to(l, (LANES, dk)))) for l in last]
        qe = [qq * jnp.exp(c) for qq, c in zip(q, cum)]
        st = [s_scr[h] for h in range(hb)]
        for i, (c, h) in enumerate(ch):
            o = o_in[i] + _bdot(qe[i], st[h])
            st[h] = jnp.concatenate([st[h][:, j * LANES:(j + 1) * LANES] * dec[i] for j in range(dv // LANES)],
                                    axis=1) + upd[i]
            on = o * lax.rsqrt(jnp.mean(o * o, axis=-1, keepdims=True) + NORM_EPS) * nw
            vs = slice(h * dv, (h + 1) * dv)
            o_ref[rows(c), vs] = (on * gate_ref[rows(c), vs]).astype(o_ref.dtype)
        for h in range(hb):
            s_scr[h] = st[h]
        return carry

    lax.fori_loop(0, nch // ug, group, 0)
    sout_ref[0] = s_scr[...]


def _la_chunk_call(q, k, v, g, gate, norm_w, s0, bsz, t, cl, hb, name):
    m = q.shape[0]
    _, nh, dk, dv = s0.shape
    kern = functools.partial(_la_chunk_kernel, hb=hb, dk=dk, dv=dv, cl=cl, nch=t // cl)
    kspec = pl.BlockSpec((t, hb * dk), lambda bi, hi: (bi, hi))
    vspec = pl.BlockSpec((t, hb * dv), lambda bi, hi: (bi, hi))
    sspec = pl.BlockSpec((1, hb, dk, dv), lambda bi, hi: (bi, hi, 0, 0))
    return pl.pallas_call(
        kern,
        grid=(bsz, nh // hb),
        in_specs=[kspec, kspec, vspec, kspec, vspec, pl.BlockSpec((1, dv), lambda bi, hi: (0, 0)), sspec],
        out_specs=[vspec, sspec],
        out_shape=[jax.ShapeDtypeStruct((m, nh * dv), BF16), jax.ShapeDtypeStruct(s0.shape, F32)],
        scratch_shapes=[pltpu.VMEM((hb, dk, dv), F32)],
        compiler_params=_cparams(("parallel", "parallel")),
        name=name,
    )(q, k, v, g, gate, _row(norm_w), s0)


def _la_step_kernel(q_ref, k_ref, v_ref, g_ref, gate_ref, nw_ref, s0_ref, o_ref, sout_ref, *, nh, dk, dv):
    first = _iota((STEP_PAD, 1), 0) == 0

    def pad(x):
        return jnp.where(first, x, 0.0)

    hg = min(nh, 4)

    def group(gi, carry):
        hs = [gi * hg + j for j in range(hg)]
        rows = lambda ref: [ref[0, pl.ds(h, 1), :] for h in hs]
        dec = [jnp.exp(jnp.transpose(jnp.broadcast_to(g, (LANES, dk)))) for g in rows(g_ref)]
        upd = [_bdot_tn(pad(k), pad(v)) for k, v in zip(rows(k_ref), rows(v_ref))]
        sn = [jnp.concatenate([s0_ref[0, h][:, j * LANES:(j + 1) * LANES] * d for j in range(dv // LANES)], axis=1)
              + u for h, d, u in zip(hs, dec, upd)]
        for h, s in zip(hs, sn):
            sout_ref[0, h] = s
        os_ = [_bdot(pad(q), s)[0:1] for q, s in zip(rows(q_ref), sn)]
        for h, o, gt in zip(hs, os_, rows(gate_ref)):
            on = o * lax.rsqrt(jnp.mean(o * o, axis=-1, keepdims=True) + NORM_EPS) * nw_ref[...]
            o_ref[0, pl.ds(h, 1), :] = (on * gt).astype(o_ref.dtype)
        return carry

    lax.fori_loop(0, nh // hg, group, 0)


def _la_step_call(q, k, v, g, gate, norm_w, s0, name):
    bsz, nh, dk, dv = s0.shape
    kh = lambda x: x.reshape(bsz, nh, dk)
    vh = lambda x: x.reshape(bsz, nh, dv)
    kspec = pl.BlockSpec((1, nh, dk), lambda bi: (bi, 0, 0))
    vspec = pl.BlockSpec((1, nh, dv), lambda bi: (bi, 0, 0))
    sspec = pl.BlockSpec((1, nh, dk, dv), lambda bi: (bi, 0, 0, 0))
    o, s = pl.pallas_call(
        functools.partial(_la_step_kernel, nh=nh, dk=dk, dv=dv),
        grid=(bsz,),
        in_specs=[kspec, kspec, vspec, kspec, vspec, pl.BlockSpec((1, dv), lambda bi: (0, 0)), sspec],
        out_specs=[vspec, sspec],
        out_shape=[jax.ShapeDtypeStruct((bsz, nh, dv), F32), jax.ShapeDtypeStruct(s0.shape, F32)],
        compiler_params=_cparams(("parallel",)),
        name=name,
    )(kh(q), kh(k), vh(v), kh(g), vh(gate), _row(norm_w), s0)
    return o.reshape(bsz, nh * dv), s


def _la_run(q, k, v, g, gate, norm_w, s0, bsz, t, hb, name):
    if t == 1:
        return _la_step_call(q, k, v, g, gate, norm_w, s0, name + "_step")
    return _la_chunk_call(q, k, v, g, gate, norm_w, s0, bsz, t, LA_CHUNK, hb, name)


def _norm_prologue(h_ref, nw_ref, xn_s):
    @pl.when(pl.program_id(1) == 0)
    def _():
        xn_s[...] = _rms(h_ref[...], nw_ref[...]).astype(BF16)


def _gla_qkg_body(row, full, col, rc, out, scr, *, scale):
    nw_ref, gk1_ref = full
    wq, wk, gk2, bgk = col
    xn_s, t1_s = scr

    @pl.when(pl.program_id(1) == 0)
    def _():
        xn = _rms(row[0][...], nw_ref[...]).astype(BF16)
        xn_s[...] = xn
        t1_s[...] = jnp.dot(xn, gk1_ref[...], preferred_element_type=F32).astype(BF16)

    xn = xn_s[...]
    out[0][...] = jnp.dot(xn, wq[...], preferred_element_type=F32) * scale
    out[1][...] = jnp.dot(xn, wk[...], preferred_element_type=F32)
    lg = jnp.dot(t1_s[...], gk2[...], preferred_element_type=F32) + bgk[...]
    out[2][...] = jax.nn.log_sigmoid(lg) / GLA_GATE_TEMP


def _vg_body(row, full, col, rc, out, scr):
    _norm_prologue(row[0], full[0], scr[0])
    xn = scr[0][...]
    tn = col[0].shape[1]
    for cs in (slice(0, tn // 2), slice(tn // 2, tn)):
        out[0][:, cs] = jnp.dot(xn, col[0][:, cs], preferred_element_type=F32)
        out[1][:, cs] = jax.nn.silu(jnp.dot(xn, col[1][:, cs], preferred_element_type=F32))


def _gla_layer(h, bsz, t, s0, norm_w, wts):
    m, dm = h.shape
    tm = _tiles(m, t)
    nh = GLA_HEADS
    dk = wts['w_q'].shape[1] // nh
    rank = wts['gk1'].shape[1]
    q, k, lg = _proj_call(functools.partial(_gla_qkg_body, scale=dk ** -0.5), m=m, n=nh * dk, tm=tm, tn=512,
                          row_ins=[h], full_ins=[_row(norm_w), wts['gk1']],
                          col_ins=[wts['w_q'], wts['w_k'], wts['gk2'], _row(wts['b_gk'])], rc_ins=[],
                          out_dtypes=[F32] * 3,
                          scratch=[pltpu.VMEM((tm, dm), BF16), pltpu.VMEM((tm, rank), BF16)], name="gla_qkg")
    v, gate = _proj_call(_vg_body, m=m, n=wts['w_v'].shape[1], tm=tm, tn=512, row_ins=[h],
                         full_ins=[_row(norm_w)], col_ins=[wts['w_v'], wts['w_g']], rc_ins=[],
                         out_dtypes=[F32] * 2, scratch=[pltpu.VMEM((tm, dm), BF16)], name="gla_vg")
    og, s_new = _la_run(q, k, v, lg, gate, wts['norm'], s0, bsz, t, 1, "gla_chunk")
    return _outproj_call(og, wts['w_o'], h, t, "gla_out"), s_new


def _hgrn_proj_body(row, full, col, rc, out, scr, *, scale, layer_idx):
    wq, wf, wi, wg, lbp = col
    _norm_prologue(row[0], full[0], scr[0])
    xn = scr[0][...]
    tn = wq.shape[1]
    for cs in (slice(0, tn // 2), slice(tn // 2, tn)):
        dot = lambda w: jnp.dot(xn, w[:, cs], preferred_element_type=F32)
        z, qv, iv, gv = dot(wf), dot(wq), dot(wi), dot(wg)
        lbx = lbp[:, cs]
        ex = jnp.exp(lbx - jnp.max(lbx, axis=0, keepdims=True))
        lb = jnp.sum(ex[1:layer_idx + 1], axis=0, keepdims=True) / jnp.sum(ex, axis=0, keepdims=True)
        e = jnp.exp(-jnp.abs(z))
        rcp = 1.0 / (1.0 + e)
        pos = z >= 0
        sig = jnp.where(pos, rcp, e * rcp)
        sig_neg = jnp.where(pos, e * rcp, rcp)
        out[0][:, cs] = qv * scale
        out[1][:, cs] = (1.0 - lb) * sig_neg
        out[2][:, cs] = jnp.log(lb + (1.0 - lb) * sig)
        out[3][:, cs] = iv
        out[4][:, cs] = jax.nn.silu(gv)


def _hgrn_layer(h, bsz, t, s0, norm_w, wts, layer_idx):
    m, dm = h.shape
    tm = _tiles(m, t)
    body = functools.partial(_hgrn_proj_body, scale=HG_EXPAND ** -0.5, layer_idx=layer_idx)
    q, k, lf, v, gate = _proj_call(body, m=m, n=dm, tm=tm, tn=512, row_ins=[h], full_ins=[_row(norm_w)],
                                   col_ins=[wts['w_q'], wts['w_f'], wts['w_i'], wts['w_g'], wts['lb']],
                                   rc_ins=[], out_dtypes=[F32] * 5, scratch=[pltpu.VMEM((tm, dm), BF16)],
                                   name="hgrn_proj")
    og, s_new = _la_run(q, k, v, lf, gate, wts['norm'], s0, bsz, t, 2, "hgrn_chunk")
    return _outproj_call(og, wts['w_o'], h, t, "hgrn_out"), s_new


def _ffn_kernel(*refs, step, tm, tf, nf, seq_tiles, final_norm):
    h_ref, nw_ref, wup_ref, wgate_ref, wc_ref, bc_ref, wdown_ref = refs[:7]
    p = 7
    if step:
        p0_ref, p1_ref = refs[p:p + 2]; p += 2
    if final_norm:
        fnw_ref = refs[p]; p += 1
    out_ref, tail_ref = refs[p:p + 2]
    xn_s, carry_s = refs[p + 2:]
    i = pl.program_id(0)
    f = pl.program_id(1)

    @pl.when(f == 0)
    def _():
        hv = h_ref[...]
        xn_s[...] = _rms(hv, nw_ref[...]).astype(BF16)
        out_ref[...] = hv

    xn = xn_s[...]
    if not step:
        @pl.when(i % seq_tiles == 0)
        def _():
            carry_s[f] = jnp.zeros((SUBLANES, tf), F32)

    hw = tf // 2
    halves = [slice(0, hw), slice(hw, tf)]
    us = [jnp.dot(xn, wup_ref[:, cs], preferred_element_type=F32) for cs in halves]
    gates = [jnp.dot(xn, wgate_ref[:, cs], preferred_element_type=F32) for cs in halves]
    rid = _iota((tm, 1), 0)
    down = None
    for cs, u, gate in zip(halves, us, gates):
        wc = wc_ref[:, cs]
        if step:
            u2, u1 = p0_ref[:, cs], p1_ref[:, cs]
            tail_ref[:, cs] = u
        else:
            prev = carry_s[f][:, cs]
            u1 = jnp.where(rid == 0, prev[SUBLANES - 1:SUBLANES], pltpu.roll(u, 1, 0))
            u2 = jnp.where(rid == 0, prev[SUBLANES - 2:SUBLANES - 1],
                           jnp.where(rid == 1, prev[SUBLANES - 1:SUBLANES], pltpu.roll(u, 2, 0)))
            tail_ref[0, :, cs] = u[tm - SUBLANES:tm]
        c = u2 * wc[0:1] + u1 * wc[1:2] + u * wc[2:3] + bc_ref[:, cs]
        act = (jax.nn.gelu(c) * gate).astype(BF16)
        d = jnp.dot(act, wdown_ref[cs, :], preferred_element_type=F32)
        down = d if down is None else down + d
    if not step:
        carry_s[f] = jnp.concatenate([u[tm - SUBLANES:tm] for u in us], axis=1)
    out_ref[...] += down
    if final_norm:
        @pl.when(f == nf - 1)
        def _():
            out_ref[...] = _rms(out_ref[...], fnw_ref[...])


def _ffn_call(h, bsz, t, conv_prev, norm_w, layer, w_up, w_gate, w_conv, b_conv, w_down, final_norm_w):
    m, dm = h.shape
    dff = w_up.shape[-1]
    step = t == 1
    tm = _tiles(m, t)
    tf = 512
    nf = dff // tf
    seq_tiles = 1 if step else t // tm
    final_norm = final_norm_w is not None
    args = [h, _row(norm_w), w_up, w_gate, w_conv.astype(F32), _row(b_conv.astype(F32)), w_down]
    specs = [pl.BlockSpec((tm, dm), lambda i, f: (i, 0)), pl.BlockSpec((1, dm), lambda i, f: (0, 0)),
             pl.BlockSpec((None, dm, tf), lambda i, f: (layer, 0, f)),
             pl.BlockSpec((None, dm, tf), lambda i, f: (layer, 0, f)),
             pl.BlockSpec((CONV_W, tf), lambda i, f: (0, f)), pl.BlockSpec((1, tf), lambda i, f: (0, f)),
             pl.BlockSpec((None, tf, dm), lambda i, f: (layer, f, 0))]
    if step:
        args += [conv_prev[:, 0].astype(F32), conv_prev[:, 1].astype(F32)]
        specs += [pl.BlockSpec((tm, tf), lambda i, f: (i, f))] * 2
        tail_shape = jax.ShapeDtypeStruct((m, dff), F32)
        tail_spec = pl.BlockSpec((tm, tf), lambda i, f: (i, f))
    else:
        tail_shape = jax.ShapeDtypeStruct((m // tm, SUBLANES, dff), F32)
        tail_spec = pl.BlockSpec((1, SUBLANES, tf), lambda i, f: (i, 0, f))
    if final_norm:
        args.append(_row(final_norm_w))
        specs.append(pl.BlockSpec((1, dm), lambda i, f: (0, 0)))
    kern = functools.partial(_ffn_kernel, step=step, tm=tm, tf=tf, nf=nf, seq_tiles=seq_tiles,
                             final_norm=final_norm)
    out, tail = pl.pallas_call(
        kern,
        grid=(m // tm, nf),
        in_specs=specs,
        out_specs=[pl.BlockSpec((tm, dm), lambda i, f: (i, 0)), tail_spec],
        out_shape=[jax.ShapeDtypeStruct((m, dm), F32), tail_shape],
        scratch_shapes=[pltpu.VMEM((tm, dm), BF16), pltpu.VMEM((nf, SUBLANES, tf), F32)],
        compiler_params=_cparams(("arbitrary", "arbitrary")),
        name="conv_ffn",
    )(*args)
    if step:
        conv_new = jnp.stack([conv_prev[:, 1].astype(F32), tail], axis=1)
    else:
        conv_new = tail.reshape(bsz, seq_tiles, SUBLANES, dff)[:, -1, SUBLANES - (CONV_W - 1):]
    return out, conv_new


def _trunk(x, s5_re, s5_im, rw_shift, rw_wkv, gla_s, hg_s, ffn_conv, p):
    bsz, t, dm = x.shape
    h = x.reshape(bsz * t, dm).astype(F32)
    depth = p['norm_mix'].shape[0]
    conv_new = []
    for i in range(depth):
        kind = i % 4
        nw = p['norm_mix'][i]
        if kind == 0:
            h, s5_re, s5_im = _s5_layer(h, bsz, t, s5_re, s5_im, nw, p['s5'], p['s5_c_re'], p['s5_c_im'],
                                        p['s5_d'], p['s5_w_glu'])
        elif kind == 1:
            h, rw_shift, rw_wkv = _rwkv_layer(h, bsz, t, rw_shift, rw_wkv, nw, p['rw'])
        elif kind == 2:
            h, gla_s = _gla_layer(h, bsz, t, gla_s, nw, p['gla'])
        else:
            h, hg_s = _hgrn_layer(h, bsz, t, hg_s, nw, p['hg'], i)
        fin = p['norm_final'] if i == depth - 1 else None
        h, c = _ffn_call(h, bsz, t, ffn_conv[i], p['norm_ffn'][i], i, p['ffn_w_up'], p['ffn_w_gate'],
                         p['ffn_w_conv'][i], p['ffn_b_conv'][i], p['ffn_w_down'], fin)
        conv_new.append(c)
    return h.reshape(bsz, t, dm), s5_re, s5_im, rw_shift, rw_wkv, gla_s, hg_s, jnp.stack(conv_new)


def _pad_cols(w, n):
    return jnp.pad(w, ((0, 0), (0, n - w.shape[1])))


def _pad_rows(w, n):
    return jnp.pad(w, ((0, n - w.shape[0]), (0, 0)))


def kernel(x_prompt, x_sample, state_s5_re, state_s5_im, state_rwkv_shift, state_rwkv_wkv, state_gla, state_hgrn, state_ffn_conv, norm_mix, norm_ffn, norm_final, s5_lambda_re, s5_lambda_im, s5_log_dt, s5_b_re, s5_b_im, s5_c_re, s5_c_im, s5_d, s5_w_glu, rw_mix, rw_w_r, rw_w_k, rw_w_v, rw_w_o, rw_w0, rw_w1, rw_w2, rw_a0, rw_a1, rw_a2, rw_g1, rw_g2, rw_k_k, rw_k_a, rw_r_k, rw_ln_w, rw_ln_b, gla_w_q, gla_w_k, gla_w_v, gla_w_gk1, gla_w_gk2, gla_b_gk, gla_w_g, gla_norm, gla_w_o, hg_w_q, hg_w_f, hg_w_i, hg_w_g, hg_lb, hg_norm, hg_w_o, ffn_w_up, ffn_w_gate, ffn_w_conv, ffn_b_conv, ffn_w_down):
    bf = lambda w: w.astype(BF16)
    f32 = lambda w: w.astype(F32)
    lora = LANES * pl.cdiv(rw_w1.shape[1], LANES)
    alora = LANES * pl.cdiv(rw_a1.shape[1], LANES)
    grank = LANES * pl.cdiv(gla_w_gk1.shape[1], LANES)
    p = {
        'norm_mix': f32(norm_mix), 'norm_ffn': f32(norm_ffn), 'norm_final': f32(norm_final),
        's5': _s5_params(s5_lambda_re, s5_lambda_im, s5_log_dt, s5_b_re, s5_b_im, s5_c_re, s5_c_im, s5_d),
        's5_c_re': s5_c_re, 's5_c_im': s5_c_im, 's5_d': s5_d, 's5_w_glu': bf(s5_w_glu),
        'rw': dict(mix=_pad_rows(f32(rw_mix), SUBLANES), w_r=bf(rw_w_r), w_k=bf(rw_w_k), w_v=bf(rw_w_v),
                   w_o=bf(rw_w_o), w0=f32(rw_w0), w1=bf(_pad_cols(rw_w1, lora)), w2=bf(_pad_rows(rw_w2, lora)),
                   a0=f32(rw_a0), a1=bf(_pad_cols(rw_a1, alora)), a2=bf(_pad_rows(rw_a2, alora)),
                   g1=bf(rw_g1), g2=bf(rw_g2), k_k=f32(rw_k_k), k_a=f32(rw_k_a), r_k=f32(rw_r_k).reshape(-1),
                   ln_w=f32(rw_ln_w), ln_b=f32(rw_ln_b)),
        'gla': dict(w_q=bf(gla_w_q), w_k=bf(gla_w_k), w_v=bf(gla_w_v), gk1=bf(_pad_cols(gla_w_gk1, grank)),
                    gk2=bf(_pad_rows(gla_w_gk2, grank)), b_gk=f32(gla_b_gk), w_g=bf(gla_w_g), norm=f32(gla_norm),
                    w_o=bf(gla_w_o)),
        'hg': dict(w_q=bf(hg_w_q), w_f=bf(hg_w_f), w_i=bf(hg_w_i), w_g=bf(hg_w_g), lb=f32(hg_lb),
                   norm=f32(hg_norm), w_o=bf(hg_w_o)),
        'ffn_w_up': bf(ffn_w_up), 'ffn_w_gate': bf(ffn_w_gate), 'ffn_w_conv': ffn_w_conv,
        'ffn_b_conv': ffn_b_conv, 'ffn_w_down': bf(ffn_w_down),
    }
    nb, _, dm = x_prompt.shape
    depth = norm_mix.shape[0]
    dff = ffn_w_up.shape[-1]
    z_s5 = jnp.zeros((nb,) + state_s5_re.shape[1:], F32)
    outs_p = _trunk(x_prompt, z_s5, z_s5, jnp.zeros((nb, dm), F32),
                    jnp.zeros((nb,) + state_rwkv_wkv.shape[1:], F32), jnp.zeros((nb,) + state_gla.shape[1:], F32),
                    jnp.zeros((nb,) + state_hgrn.shape[1:], F32), jnp.zeros((depth, nb, CONV_W - 1, dff), F32), p)
    outs_s = _trunk(x_sample, f32(state_s5_re), f32(state_s5_im), state_rwkv_shift, f32(state_rwkv_wkv),
                    f32(state_gla), f32(state_hgrn), state_ffn_conv, p)
    return (outs_p[0], outs_s[0]) + tuple(outs_p[1:]) + tuple(outs_s[1:])
```

```python
import functools
import math

import jax
import jax.numpy as jnp
from jax import lax
from jax.experimental import pallas as pl
from jax.experimental.pallas import tpu as pltpu

F32 = jnp.float32
BF16 = jnp.bfloat16

NORM_EPS = 1e-6
RW_LN_EPS = 64e-5
RW_HEAD = 64
S5_GROUP = 16
S5_STATE = 64
S5_CHUNK = 16
GLA_HEADS = 4
GLA_GATE_TEMP = 16.0
HG_EXPAND = 128
CONV_W = 3
LA_CHUNK = 64
RW_CHUNK = 64
STEP_PAD = 16
LANES = 128
SUBLANES = 8
VMEM_LIMIT = 56 * 1024 * 1024


def _cparams(sem):
    return pltpu.CompilerParams(dimension_semantics=sem, vmem_limit_bytes=VMEM_LIMIT)


def _bdot(a, b):
    return jnp.dot(a.astype(BF16), b.astype(BF16), preferred_element_type=F32)


def _bdot_nt(a, b):
    return lax.dot_general(a.astype(BF16), b.astype(BF16), (((1,), (1,)), ((), ())),
                           preferred_element_type=F32)


def _bdot_tn(a, b):
    return lax.dot_general(a.astype(BF16), b.astype(BF16), (((0,), (0,)), ((), ())),
                           preferred_element_type=F32)


def _split3(x):
    hi = x.astype(BF16)
    r = x - hi.astype(F32)
    mid = r.astype(BF16)
    lo = (r - mid.astype(F32)).astype(BF16)
    return hi, mid, lo


def _xdot_l(m, x):
    hi, mid, lo = _split3(x)
    d = lambda p: jnp.dot(m, p, preferred_element_type=F32)
    return d(hi) + d(mid) + d(lo)


def _xdot_r(x, m):
    hi, mid, lo = _split3(x)
    d = lambda p: jnp.dot(p, m, preferred_element_type=F32)
    return d(hi) + d(mid) + d(lo)


def _rms(x, g):
    y = x * lax.rsqrt(jnp.mean(x * x, axis=-1, keepdims=True) + NORM_EPS)
    return y * g


def _iota(shape, dim):
    return lax.broadcasted_iota(jnp.int32, shape, dim)


def _rmsnorm_kernel(x_ref, g_ref, o_ref):
    o_ref[...] = _rms(x_ref[...], g_ref[...])


def _rmsnorm_call(x, g):
    r, d = x.shape
    tr = 512 if r % 512 == 0 else r
    return pl.pallas_call(
        _rmsnorm_kernel,
        grid=(r // tr,),
        in_specs=[pl.BlockSpec((tr, d), lambda i: (i, 0)), pl.BlockSpec((1, d), lambda i: (0, 0))],
        out_specs=pl.BlockSpec((tr, d), lambda i: (i, 0)),
        out_shape=jax.ShapeDtypeStruct((r, d), F32),
        compiler_params=_cparams(("parallel",)),
        name="rmsnorm",
    )(x, g.reshape(1, d))


def _proj_call(body, *, m, n, tm, tn, row_ins, full_ins, col_ins, rc_ins, out_dtypes, scratch, name):
    args, specs = [], []
    for a in row_ins:
        if isinstance(a, tuple):
            args.append(a[0]); specs.append(a[1])
        else:
            args.append(a); specs.append(pl.BlockSpec((tm, a.shape[1]), lambda i, j: (i, 0)))
    for a in full_ins:
        args.append(a); specs.append(pl.BlockSpec(a.shape, lambda i, j, nd=a.ndim: (0,) * nd))
    for a in col_ins:
        args.append(a); specs.append(pl.BlockSpec((a.shape[0], tn), lambda i, j: (0, j)))
    for a in rc_ins:
        args.append(a); specs.append(pl.BlockSpec((tm, tn), lambda i, j: (i, j)))
    nr, nf, nc, nrc, no = len(row_ins), len(full_ins), len(col_ins), len(rc_ins), len(out_dtypes)

    def kernel(*refs):
        p = 0
        groups = []
        for cnt in (nr, nf, nc, nrc, no):
            groups.append(refs[p:p + cnt]); p += cnt
        body(*groups, refs[p:])

    outs = pl.pallas_call(
        kernel,
        grid=(m // tm, n // tn),
        in_specs=specs,
        out_specs=[pl.BlockSpec((tm, tn), lambda i, j: (i, j)) for _ in out_dtypes],
        out_shape=[jax.ShapeDtypeStruct((m, n), dt) for dt in out_dtypes],
        scratch_shapes=scratch,
        compiler_params=_cparams(("parallel", "arbitrary")),
        name=name,
    )(*args)
    return outs


def _row(v):
    return v.reshape(1, -1)


def _tiles(m, t):
    if t == 1:
        return 512 if m % 512 == 0 else m
    return min(512, t)


def _s5_params(lam_re, lam_im, log_dt, b_re, b_im, c_re, c_im, d):
    hp = lax.Precision.HIGHEST
    g, n = lam_re.shape
    p = b_re.shape[-1]
    cl = S5_CHUNK
    lr = jnp.minimum(lam_re.astype(F32), -1e-4)
    li = lam_im.astype(F32)
    dt = jnp.exp(log_dt.astype(F32))[:, None]
    mag = jnp.exp(lr * dt)
    ab_re = mag * jnp.cos(li * dt)
    ab_im = mag * jnp.sin(li * dt)
    den = lr * lr + li * li
    f_re = ((ab_re - 1.0) * lr + ab_im * li) / den
    f_im = (ab_im * lr - (ab_re - 1.0) * li) / den
    br, bi = b_re.astype(F32), b_im.astype(F32)
    bb_re = f_re[..., None] * br - f_im[..., None] * bi
    bb_im = f_re[..., None] * bi + f_im[..., None] * br
    cr, ci = c_re.astype(F32), c_im.astype(F32)

    pr, pi = [jnp.ones_like(ab_re)], [jnp.zeros_like(ab_im)]
    for _ in range(cl):
        pr.append(pr[-1] * ab_re - pi[-1] * ab_im)
        pi.append(pr[-2] * ab_im + pi[-1] * ab_re)
    pw_re = jnp.stack(pr)
    pw_im = jnp.stack(pi)
    ajb_re = pw_re[:cl, :, :, None] * bb_re - pw_im[:cl, :, :, None] * bb_im
    ajb_im = pw_re[:cl, :, :, None] * bb_im + pw_im[:cl, :, :, None] * bb_re
    kj = (jnp.einsum('gpn,jgnq->jgpq', cr, ajb_re, precision=hp)
          - jnp.einsum('gpn,jgnq->jgpq', ci, ajb_im, precision=hp))
    kq = kj.transpose(1, 3, 0, 2).reshape(g, p, cl * p)
    w_re = ajb_re[::-1].transpose(1, 0, 3, 2).reshape(g, cl * p, n)
    w_im = ajb_im[::-1].transpose(1, 0, 3, 2).reshape(g, cl * p, n)
    wm = jnp.concatenate([w_re, w_im], axis=-1)
    ca_re = cr[None] * pw_re[1:, :, None, :] - ci[None] * pw_im[1:, :, None, :]
    ca_im = cr[None] * pw_im[1:, :, None, :] + ci[None] * pw_re[1:, :, None, :]
    v_re = ca_re.transpose(1, 3, 0, 2).reshape(g, n, cl * p)
    v_im = -ca_im.transpose(1, 3, 0, 2).reshape(g, n, cl * p)
    vm = jnp.concatenate([v_re, v_im], axis=1)
    ar, ai = pw_re[cl], pw_im[cl]
    pcs, qcs = [], []
    for _ in range(SUBLANES):
        pcs.append(jnp.concatenate([ar, ar], axis=-1))
        qcs.append(jnp.concatenate([-ai, ai], axis=-1))
        ar, ai = ar * ar - ai * ai, 2.0 * ar * ai
    pc = jnp.stack(pcs, axis=1)
    qc = jnp.stack(qcs, axis=1)
    return dict(ab_re=ab_re, ab_im=ab_im, bb_re=bb_re, bb_im=bb_im, kq=kq, wm=wm.astype(BF16),
                vm=vm.astype(BF16), pc=pc, qc=qc)


def _s5_seq_kernel(x_ref, kq_ref, w_ref, vm_ref, pc_ref, qc_ref, d_ref, h0_ref, z_ref, hf_ref, mt_scr, *, gb, nc):
    half = S5_STATE
    cl, p = S5_CHUNK, S5_GROUP
    per = LANES // p
    rid = _iota((nc, 1), 0)
    lane_blk = _iota((1, LANES), 1) // p
    nlev = int(math.log2(nc))

    @pl.when(pl.program_id(1) == 0)
    def _():
        lane_w = _iota((1, cl * p), 1)
        for g in range(gb):
            kq = kq_ref[g]
            blocks = [kq] + [jnp.where(lane_w >= s * p, pltpu.roll(kq, s * p, 1), 0.0) for s in range(1, cl)]
            mt_scr[g] = jnp.concatenate(blocks, axis=0).astype(BF16)

    xs = [x_ref[0, :, s, :] for s in range(cl)]
    groups = range(gb)

    def to_chunk(g):
        pieces = []
        for hf in range(cl // per):
            acc = None
            for j in range(per):
                src = xs[hf * per + j]
                sh = ((j - g) * p) % LANES
                src = pltpu.roll(src, sh, 1) if sh else src
                acc = src if acc is None else jnp.where(lane_blk == j, src, acc)
            pieces.append(acc)
        return jnp.concatenate(pieces, axis=1).astype(BF16)

    ub = [to_chunk(g) for g in groups]
    y = [jnp.dot(u, mt_scr[g], preferred_element_type=F32) for g, u in zip(groups, ub)]
    x = [jnp.dot(u, w_ref[g], preferred_element_type=F32) for g, u in zip(groups, ub)]
    hprev = []
    for g in groups:
        pc = pc_ref[g]
        qc = qc_ref[g]
        h0 = h0_ref[0, g]
        h0c = pc[0:1] * h0 + qc[0:1] * pltpu.roll(h0, half, 1)
        xg = x[g] + jnp.where(rid == 0, h0c, 0.0)
        for lv in range(nlev):
            sh = 1 << lv
            xsft = jnp.where(rid >= sh, pltpu.roll(xg, sh, 0), 0.0)
            xg = xg + pc[lv:lv + 1] * xsft + qc[lv:lv + 1] * pltpu.roll(xsft, half, 1)
        hf_ref[0, g] = xg[nc - 1:nc, :]
        hprev.append(jnp.where(rid == 0, h0, pltpu.roll(xg, 1, 0)).astype(BF16))
    y = [yy + jnp.dot(hp, vm_ref[g], preferred_element_type=F32) for g, yy, hp in zip(groups, y, hprev)]
    for t in range(cl):
        hf, j = divmod(t, per)
        acc = None
        for g in groups:
            src = y[g][:, hf * LANES:(hf + 1) * LANES]
            sh = ((g - j) * p) % LANES
            src = pltpu.roll(src, sh, 1) if sh else src
            acc = src if acc is None else jnp.where(lane_blk == g, src, acc)
        z_ref[0, :, t, :] = jax.nn.gelu(acc + d_ref[...] * xs[t])


def _s5_seq_call(x4, prm, d, h0):
    b, nc, cl, dm = x4.shape
    g = dm // S5_GROUP
    w = cl * S5_GROUP
    n2 = 2 * S5_STATE
    gb = LANES // S5_GROUP
    kern = functools.partial(_s5_seq_kernel, gb=gb, nc=nc)
    gspec = lambda shape: pl.BlockSpec((gb,) + shape, lambda gi, bi: (gi, 0, 0))
    xspec = pl.BlockSpec((1, nc, cl, LANES), lambda gi, bi: (bi, 0, 0, gi))
    return pl.pallas_call(
        kern,
        grid=(g // gb, b),
        in_specs=[xspec,
                  gspec((S5_GROUP, w)), gspec((w, n2)), gspec((n2, w)), gspec((SUBLANES, n2)), gspec((SUBLANES, n2)),
                  pl.BlockSpec((1, LANES), lambda gi, bi: (0, gi)),
                  pl.BlockSpec((1, gb, 1, n2), lambda gi, bi: (bi, gi, 0, 0))],
        out_specs=[xspec, pl.BlockSpec((1, gb, 1, n2), lambda gi, bi: (bi, gi, 0, 0))],
        out_shape=[jax.ShapeDtypeStruct(x4.shape, F32), jax.ShapeDtypeStruct((b, g, 1, n2), F32)],
        scratch_shapes=[pltpu.VMEM((gb, w, w), BF16)],
        compiler_params=_cparams(("parallel", "arbitrary")),
        name="s5_seq",
    )(x4, prm['kq'], prm['wm'], prm['vm'], prm['pc'], prm['qc'], _row(d.astype(F32)), h0)


def _s5_step_kernel(u_ref, hr_ref, hi_ref, ar_ref, ai_ref, bdr_ref, bdi_ref, cdr_ref, cdi_ref, d_ref,
                    z_ref, hro_ref, hio_ref):
    u = u_ref[...]
    ar, ai = ar_ref[...], ai_ref[...]
    hr, hi = hr_ref[...], hi_ref[...]
    hr2 = ar * hr - ai * hi + _bdot(u, bdr_ref[0])
    hi2 = ar * hi + ai * hr + _bdot(u, bdi_ref[0])
    y = _bdot(hr2, cdr_ref[0]) - _bdot(hi2, cdi_ref[0]) + d_ref[...] * u
    z_ref[...] = jax.nn.gelu(y)
    hro_ref[...] = hr2
    hio_ref[...] = hi2


def _s5_step_call(u, h_re, h_im, prm, c_re, c_im, d):
    b, dm = u.shape
    g, n = prm['ab_re'].shape
    p = S5_GROUP
    gpb = LANES // p
    nb = g // gpb
    eye = jnp.eye(gpb, dtype=F32)
    bd = lambda bb: jnp.einsum('kgnp,gh->kgphn', bb.reshape(nb, gpb, n, p), eye).reshape(nb, gpb * p, gpb * n)
    cd = lambda cc: jnp.einsum('kgpn,gh->kgnhp', cc.astype(F32).reshape(nb, gpb, p, n), eye).reshape(
        nb, gpb * n, gpb * p)
    bdr, bdi = bd(prm['bb_re']).astype(BF16), bd(prm['bb_im']).astype(BF16)
    cdr, cdi = cd(c_re).astype(BF16), cd(c_im).astype(BF16)
    wn = gpb * n
    cspec = lambda wd: pl.BlockSpec((b, wd), lambda k: (0, k))
    rspec = lambda wd: pl.BlockSpec((1, wd), lambda k: (0, k))
    z, hr2, hi2 = pl.pallas_call(
        _s5_step_kernel,
        grid=(nb,),
        in_specs=[cspec(LANES), cspec(wn), cspec(wn), rspec(wn), rspec(wn),
                  pl.BlockSpec((1, LANES, wn), lambda k: (k, 0, 0)), pl.BlockSpec((1, LANES, wn), lambda k: (k, 0, 0)),
                  pl.BlockSpec((1, wn, LANES), lambda k: (k, 0, 0)), pl.BlockSpec((1, wn, LANES), lambda k: (k, 0, 0)),
                  rspec(LANES)],
        out_specs=[cspec(LANES), cspec(wn), cspec(wn)],
        out_shape=[jax.ShapeDtypeStruct((b, dm), F32), jax.ShapeDtypeStruct((b, g * n), F32),
                   jax.ShapeDtypeStruct((b, g * n), F32)],
        compiler_params=_cparams(("parallel",)),
        name="s5_step",
    )(u, h_re.reshape(b, g * n), h_im.reshape(b, g * n), prm['ab_re'].reshape(1, g * n),
      prm['ab_im'].reshape(1, g * n), bdr, bdi, cdr, cdi, _row(d.astype(F32)))
    return z, hr2.reshape(b, g, n), hi2.reshape(b, g, n)


def _glu_body(row, full, col, rc, out, scr):
    (z_ref,) = row
    (w_ref,) = col
    zt_ref, h_ref = rc
    (zb,) = scr

    @pl.when(pl.program_id(1) == 0)
    def _():
        zb[...] = z_ref[...].astype(BF16)

    zt = zt_ref[...]
    gate = jax.nn.sigmoid(jnp.dot(zb[...], w_ref[...], preferred_element_type=F32))
    out[0][...] = h_ref[...] + zt * gate


def _s5_layer(h, bsz, t, st_re, st_im, norm_w, prm, c_re, c_im, d, w_glu):
    m, dm = h.shape
    g, n, p = dm // S5_GROUP, S5_STATE, S5_GROUP
    xn = _rmsnorm_call(h, norm_w)
    if t == 1:
        z, s_re, s_im = _s5_step_call(xn, st_re, st_im, prm, c_re, c_im, d)
    else:
        cl = S5_CHUNK
        nc = t // cl
        h0 = jnp.concatenate([st_re, st_im], axis=-1).reshape(bsz, g, 1, 2 * n)
        z4, hf = _s5_seq_call(xn.reshape(bsz, nc, cl, dm), prm, d, h0)
        z = z4.reshape(m, dm)
        s_re, s_im = hf[:, :, 0, :n], hf[:, :, 0, n:]
    tm = _tiles(m, t)
    (h_new,) = _proj_call(_glu_body, m=m, n=dm, tm=tm, tn=512, row_ins=[z], full_ins=[], col_ins=[w_glu],
                          rc_ins=[z, h], out_dtypes=[F32], scratch=[pltpu.VMEM((tm, dm), BF16)], name="s5_glu")
    return h_new, s_re, s_im


def _rwkv_proj_body(row, full, col, rc, out, scr, *, tm, seq_tiles, step):
    h_ref, prev_ref = row
    nw_ref, mix_ref, w1_ref, a1_ref, g1_ref, bd_ref = full
    wr, wk, wv, w2, a2, g2, w0, a0, kk_ref, ka_ref = col
    r_o, lw_o, k_o, v_o, a_o, b_o, g_o = out
    xr_s, xk_s, xv_s, tw_s, ta_s, tg_s = scr
    i = pl.program_id(0)

    @pl.when(pl.program_id(1) == 0)
    def _():
        nw = nw_ref[...]
        xn = _rms(h_ref[...], nw)
        if step:
            xprev = prev_ref[...]
        else:
            last = _rms(prev_ref[...], nw)[SUBLANES - 1:SUBLANES, :]
            last = jnp.where(i % seq_tiles == 0, 0.0, last)
            xprev = jnp.where(_iota((tm, 1), 0) == 0, last, pltpu.roll(xn, 1, 0))
        xx = xprev - xn
        mix = mix_ref[...]
        xr_s[...] = (xn + xx * mix[0:1]).astype(BF16)
        xk_s[...] = (xn + xx * mix[2:3]).astype(BF16)
        xv_s[...] = (xn + xx * mix[3:4]).astype(BF16)
        tw_s[...] = jnp.tanh(_bdot(xn + xx * mix[1:2], w1_ref[...])).astype(BF16)
        ta_s[...] = _bdot(xn + xx * mix[4:5], a1_ref[...]).astype(BF16)
        tg_s[...] = jax.nn.sigmoid(_bdot(xn + xx * mix[5:6], g1_ref[...])).astype(BF16)

    dot = lambda a, b: jnp.dot(a[...], b[...], preferred_element_type=F32)
    r = dot(xr_s, wr)
    k = dot(xk_s, wk)
    v = dot(xv_s, wv)
    lw = -jax.nn.sigmoid(w0[...] + dot(tw_s, w2)) * math.exp(-0.5)
    a = jax.nn.sigmoid(a0[...] + dot(ta_s, a2))
    kk = k * kk_ref[...]
    ss = _bdot(kk * kk, bd_ref[...])
    kk = kk / jnp.maximum(jnp.sqrt(ss), 1e-12)
    r_o[...] = r
    lw_o[...] = lw
    k_o[...] = k * (1.0 + (a - 1.0) * ka_ref[...])
    v_o[...] = v
    a_o[...] = -kk
    b_o[...] = kk * a
    g_o[...] = dot(tg_s, g2)


def _rwkv_chunk_kernel(r_ref, lw_ref, k_ref, v_ref, a_ref, b_ref, gate_ref, lnw_ref, lnb_ref, rk_ref, s0_ref,
                       o_ref, sout_ref, p_scr, rp_scr, o0_scr, m_scr, d_scr, g_scr, *, npair, cl, nch):
    hd = RW_HEAD
    pw = 2 * hd
    lane = _iota((1, pw), 1)
    head0 = lane < hd
    bdmask = (_iota((pw, pw), 0) // hd) == (_iota((pw, pw), 1) // hd)
    e_dup = ((_iota((hd, pw), 1) % hd) == _iota((hd, pw), 0)).astype(BF16)
    e_fold = ((_iota((pw, hd), 0) % hd) == _iota((pw, hd), 1)).astype(BF16)
    ones_bd = bdmask.astype(BF16)
    avg = (bdmask.astype(F32) * (1.0 / hd)).astype(BF16)
    for p in range(npair):
        p_scr[p] = jnp.where(bdmask, _xdot_r(s0_ref[0, p], e_dup), 0.0)
    s2 = 2 * cl
    rr = _iota((s2, s2), 0)
    cc = _iota((s2, s2), 1)
    same = (rr // cl) == (cc // cl)
    strict = same & ((rr % cl) > (cc % cl))
    incl = same & ((rr % cl) >= (cc % cl))
    eye_s = (rr == cc).astype(F32)
    rid = _iota((cl, 1), 0)
    zeros_s = jnp.zeros((s2, pw), F32)
    nround = int(math.log2(cl)) - 1

    def stack(x):
        return jnp.concatenate([jnp.where(head0, x, 0.0), jnp.where(head0, 0.0, x)], axis=0)

    ug = 4 if nch % 4 == 0 else 1
    cat = jnp.concatenate

    def tile(ref, c, p):
        return ref[pl.ds(pl.multiple_of(c * cl, cl), cl), p * pw:(p + 1) * pw]

    def phase_a(it, carry):
        ch = [(it * ug + j, p) for j in range(ug) for p in range(npair)]
        ld = lambda ref: [tile(ref, c, p) for c, p in ch]
        lw = ld(lw_ref)
        cs = lw
        sh = 1
        while sh < cl:
            cs = [x + jnp.where(rid >= sh, pltpu.roll(x, sh, 0), 0.0) for x in cs]
            sh *= 2
        gam = [jnp.exp(x) for x in cs]
        ginv = [jnp.exp(-x) for x in cs]
        ats = [stack(a * jnp.exp(x - l)) for a, x, l in zip(ld(a_ref), cs, lw)]
        rts = [stack(r * g) for r, g in zip(ld(r_ref), gam)]
        x1s = [cat([a, r], axis=0).astype(BF16) for a, r in zip(ats, rts)]
        x2s = [cat([stack(b * g), stack(k * g)], axis=0).astype(BF16)
               for b, k, g in zip(ld(b_ref), ld(k_ref), ginv)]
        gs = [_bdot_nt(x1, x2) for x1, x2 in zip(x1s, x2s)]
        nmat = [jnp.where(strict, g[:s2, :s2], 0.0) for g in gs]
        aak = [jnp.where(strict, g[:s2, s2:], 0.0) for g in gs]
        arb = [jnp.where(incl, g[s2:, :s2], 0.0) for g in gs]
        ark = [jnp.where(incl, g[s2:, s2:], 0.0) for g in gs]
        vs = [stack(v) for v in ld(v_ref)]
        akv = [_bdot(x, v) for x, v in zip(aak, vs)]
        rkv = [_bdot(x, v) for x, v in zip(ark, vs)]
        tinv = [eye_s + n for n in nmat]
        pk = nmat
        for _ in range(nround):
            pk = [_bdot(x, x) for x in pk]
            tinv = [t + _bdot(x, t) for x, t in zip(pk, tinv)]
        tz = [_bdot(t, cat([x, a], axis=1)) for t, x, a in zip(tinv, akv, ats)]
        u0s = [x[:, :pw] for x in tz]
        aps = [x[:, pw:] for x in tz]
        z = [_bdot(x, cat([a, u], axis=1)) for x, a, u in zip(arb, aps, u0s)]
        md = [_bdot_tn(cat([cat([a, u], axis=1), cat([zeros_s, v], axis=1)], axis=0), x2)
              for a, u, v, x2 in zip(aps, u0s, vs, x2s)]
        for i, (c, p) in enumerate(ch):
            rps = rts[i] + z[i][:, :pw]
            o0s = z[i][:, pw:] + rkv[i]
            rp_scr[c, p] = (rps[:cl] + rps[cl:]).astype(BF16)
            o0_scr[c, p] = o0s[:cl] + o0s[cl:]
            gl = gam[i][cl - 1:cl, :]
            m_scr[c, p] = (md[i][:pw] * gl).astype(BF16)
            d_scr[c, p] = md[i][pw:] * gl
            g_scr[c, p] = gl
        return carry

    lax.fori_loop(0, nch // ug, phase_a, 0)

    def phase_b(it, carry):
        cs = [it * ug + j for j in range(ug)]
        prs = range(npair)
        bonus = [[_bdot(tile(r_ref, c, p) * tile(k_ref, c, p) * rk_ref[:, p * pw:(p + 1) * pw], ones_bd)
                  * tile(v_ref, c, p) for p in prs] for c in cs]
        st = [p_scr[p] for p in prs]
        outs, mus, dlts, vrs = [], [], [], []
        for j in range(ug + 2):
            if j < ug:
                c = cs[j]
                new = [st[p] * g_scr[c, p] + _bdot(st[p], m_scr[c, p]) + d_scr[c, p] for p in prs]
                outs.append([_bdot_nt(rp_scr[c, p], st[p]) + o0_scr[c, p] for p in prs])
                st = new
            if 1 <= j <= ug:
                mus.append([_bdot(o, avg) for o in outs[j - 1]])
            if 2 <= j:
                dl = [o - m for o, m in zip(outs[j - 2], mus[j - 2])]
                dlts.append(dl)
                vrs.append([_bdot(d * d, avg) for d in dl])
        for p in prs:
            p_scr[p] = st[p]
        for j, c in enumerate(cs):
            for p in prs:
                sl = slice(p * pw, (p + 1) * pw)
                on = dlts[j][p] * lax.rsqrt(vrs[j][p] + RW_LN_EPS) * lnw_ref[:, sl] + lnb_ref[:, sl]
                o_ref[pl.ds(pl.multiple_of(c * cl, cl), cl), sl] = (
                    (on + bonus[j][p]) * tile(gate_ref, c, p)).astype(o_ref.dtype)
        return carry

    lax.fori_loop(0, nch // ug, phase_b, 0)
    for p in range(npair):
        sout_ref[0, p] = _xdot_r(jnp.where(bdmask, p_scr[p], 0.0), e_fold)


def _rwkv_chunk_call(r, lw, k, v, a, b, gate, ln_w, ln_b, r_k, s0, bsz, t, cl):
    m, dm = r.shape
    hd = RW_HEAD
    pw = 2 * hd
    npairs = dm // pw
    npair = 2
    nch = t // cl
    s0p = s0.reshape(bsz, npairs, pw, hd)
    kern = functools.partial(_rwkv_chunk_kernel, npair=npair, cl=cl, nch=nch)
    tspec = pl.BlockSpec((t, npair * pw), lambda bi, pi: (bi, pi))
    vspec = pl.BlockSpec((1, npair * pw), lambda bi, pi: (0, pi))
    sspec = pl.BlockSpec((1, npair, pw, hd), lambda bi, pi: (bi, pi, 0, 0))
    o, s = pl.pallas_call(
        kern,
        grid=(bsz, npairs // npair),
        in_specs=[tspec] * 7 + [vspec] * 3 + [sspec],
        out_specs=[tspec, sspec],
        out_shape=[jax.ShapeDtypeStruct((m, dm), BF16), jax.ShapeDtypeStruct((bsz, npairs, pw, hd), F32)],
        scratch_shapes=[pltpu.VMEM((npair, pw, pw), F32),
                        pltpu.VMEM((nch, npair, cl, pw), BF16), pltpu.VMEM((nch, npair, cl, pw), F32),
                        pltpu.VMEM((nch, npair, pw, pw), BF16), pltpu.VMEM((nch, npair, pw, pw), F32),
                        pltpu.VMEM((nch, npair, 1, pw), F32)],
        compiler_params=_cparams(("parallel", "parallel")),
        name="rwkv_chunk",
    )(r, lw, k, v, a, b, gate, _row(ln_w), _row(ln_b), _row(r_k), s0p)
    return o, s.reshape(bsz, dm // hd, hd, hd)


def _rwkv_step_kernel(r_ref, lw_ref, k_ref, v_ref, a_ref, b_ref, gate_ref, lnw_ref, lnb_ref, rk_ref, s0_ref,
                      o_ref, sout_ref, oacc, *, hd):
    w = jnp.exp(lw_ref[...])
    a, b, k, r = a_ref[...], b_ref[...], k_ref[...], r_ref[...]

    def value_row(i, carry):
        st = s0_ref[0, i]
        sa = jnp.sum(st * a, axis=0, keepdims=True)
        sn = st * w + sa * b + v_ref[pl.ds(i, 1), :] * k
        sout_ref[0, i] = sn
        oacc[pl.ds(i, 1), :] = jnp.sum(sn * r, axis=0, keepdims=True)
        return carry

    lax.fori_loop(0, hd, value_row, 0, unroll=8)
    o = oacc[...]
    mu = jnp.mean(o, axis=0, keepdims=True)
    dlt = o - mu
    var = jnp.mean(dlt * dlt, axis=0, keepdims=True)
    on = dlt * lax.rsqrt(var + RW_LN_EPS) * lnw_ref[...] + lnb_ref[...]
    bonus = jnp.sum(r * k * rk_ref[...], axis=0, keepdims=True) * v_ref[...]
    o_ref[...] = (on + bonus) * gate_ref[...]


def _rwkv_step_call(r, lw, k, v, a, b, gate, ln_w, ln_b, r_k, s0):
    bsz, dm = r.shape
    hd = RW_HEAD
    nh = dm // hd
    col = lambda x: jnp.broadcast_to(x.reshape(dm, 1), (dm, bsz))
    vspec = pl.BlockSpec((hd, bsz), lambda hi: (hi, 0))
    sspec = pl.BlockSpec((1, hd, hd, bsz), lambda hi: (hi, 0, 0, 0))
    o, s = pl.pallas_call(
        functools.partial(_rwkv_step_kernel, hd=hd),
        grid=(nh,),
        in_specs=[vspec] * 10 + [sspec],
        out_specs=[vspec, sspec],
        out_shape=[jax.ShapeDtypeStruct((dm, bsz), F32), jax.ShapeDtypeStruct((nh, hd, hd, bsz), F32)],
        scratch_shapes=[pltpu.VMEM((hd, bsz), F32)],
        compiler_params=_cparams(("parallel",)),
        name="rwkv_step",
    )(*[x.T for x in (r, lw, k, v, a, b, gate)], col(ln_w), col(ln_b), col(r_k), jnp.transpose(s0, (1, 2, 3, 0)))
    return o.T, jnp.transpose(s, (3, 0, 1, 2))


def _outproj_body(row, full, col, rc, out, scr):
    out[0][...] = rc[0][...] + jnp.dot(row[0][...].astype(BF16), col[0][...], preferred_element_type=F32)


def _outproj_call(a, w, h, t, name):
    m, kdim = a.shape
    n = w.shape[1]
    (o,) = _proj_call(_outproj_body, m=m, n=n, tm=_tiles(m, t), tn=512, row_ins=[a], full_ins=[], col_ins=[w],
                      rc_ins=[h], out_dtypes=[F32], scratch=[], name=name)
    return o


def _rwkv_layer(h, bsz, t, shift, wkv, norm_w, wts):
    m, dm = h.shape
    tm = _tiles(m, t)
    step = t == 1
    if step:
        prev = (shift.astype(F32), pl.BlockSpec((tm, dm), lambda i, j: (i, 0)))
        seq_tiles = 1
    else:
        seq_tiles = t // tm
        prev = (h, pl.BlockSpec((SUBLANES, dm), lambda i, j: (jnp.maximum(i * (tm // SUBLANES) - 1, 0), 0)))
    tn = 256
    bd = (jnp.arange(tn)[:, None] // RW_HEAD == jnp.arange(tn)[None, :] // RW_HEAD).astype(BF16)
    body = functools.partial(_rwkv_proj_body, tm=tm, seq_tiles=seq_tiles, step=step)
    lora = wts['w1'].shape[1]
    glora = wts['g1'].shape[1]
    r, lw, k, v, a, b, g = _proj_call(
        body, m=m, n=dm, tm=tm, tn=tn, row_ins=[h, prev],
        full_ins=[_row(norm_w), wts['mix'], wts['w1'], wts['a1'], wts['g1'], bd],
        col_ins=[wts['w_r'], wts['w_k'], wts['w_v'], wts['w2'], wts['a2'], wts['g2'],
                 _row(wts['w0']), _row(wts['a0']), _row(wts['k_k']), _row(wts['k_a'])],
        rc_ins=[], out_dtypes=[F32] * 7,
        scratch=[pltpu.VMEM((tm, dm), BF16)] * 3 + [pltpu.VMEM((tm, lora), BF16)] * 2
        + [pltpu.VMEM((tm, glora), BF16)],
        name="rwkv_proj")
    if step:
        shift_new = _rmsnorm_call(h, norm_w)
        og, s_new = _rwkv_step_call(r, lw, k, v, a, b, g, wts['ln_w'], wts['ln_b'], wts['r_k'], wkv)
    else:
        shift_new = _rmsnorm_call(h.reshape(bsz, t, dm)[:, -1], norm_w)
        og, s_new = _rwkv_chunk_call(r, lw, k, v, a, b, g, wts['ln_w'], wts['ln_b'], wts['r_k'], wkv, bsz, t,
                                     RW_CHUNK)
    return _outproj_call(og, wts['w_o'], h, t, "rwkv_out"), shift_new, s_new


def _la_chunk_kernel(q_ref, k_ref, v_ref, g_ref, gate_ref, nw_ref, s0_ref, o_ref, sout_ref, s_scr,
                     *, hb, dk, dv, cl, nch):
    s_scr[...] = s0_ref[0]
    rowi = _iota((cl, cl), 0)
    coli = _iota((cl, cl), 1)
    tri = (rowi >= coli).astype(BF16)
    levels = [1 << i for i in range(int(math.log2(cl)))]
    masks = {s: ((rowi // s) % 2 == 1) & ((coli // s) == (rowi // s) - 1) for s in levels}
    rid = _iota((cl, 1), 0)
    nw = nw_ref[...]
    ug = max(1, 8 // hb)
    if nch % ug:
        ug = 1

    def rows(c):
        return pl.ds(pl.multiple_of(c * cl, cl), cl)

    def group(it, carry):
        ch = [(it * ug + j, h) for j in range(ug) for h in range(hb)]
        ldk = lambda ref: [ref[rows(c), h * dk:(h + 1) * dk] for c, h in ch]
        ldv = lambda ref: [ref[rows(c), h * dv:(h + 1) * dv] for c, h in ch]
        q, k, v = ldk(q_ref), ldk(k_ref), ldv(v_ref)
        cum = [_xdot_l(tri, g) for g in ldk(g_ref)]
        att = [jnp.zeros((cl, cl), F32) for _ in ch]
        ends = cum
        for s in levels:
            prev = [jnp.where(rid >= s, pltpu.roll(e, s, 0), 0.0) for e in ends]
            a_s = [_bdot_nt(qq * jnp.exp(c - p), kk * jnp.exp(e - c))
                   for qq, kk, c, p, e in zip(q, k, cum, prev, ends)]
            att = [a + jnp.where(masks[s], x, 0.0) for a, x in zip(att, a_s)]
            if 2 * s < cl:
                ends = [jnp.where((rid % (2 * s)) >= s, e, pltpu.roll(e, cl - s, 0)) for e in ends]
        o_in = [_bdot(a, vv) + jnp.sum(qq * kk, axis=-1, keepdims=True) * vv
                for a, vv, qq, kk in zip(att, v, q, k)]
        last = [c[cl - 1:cl, :] for c in cum]
        upd = [_bdot_tn(kk * jnp.exp(l - c), vv) for kk, l, c, vv in zip(k, last, cum, v)]
        dec = [jnp.exp(jnp.transpose(jnp.broadcast_to(l, (LANES, dk)))) for l in last]
        qe = [qq * jnp.exp(c) for qq, c in zip(q, cum)]
        st = [s_scr[h] for h in range(hb)]
        for i, (c, h) in enumerate(ch):
            o = o_in[i] + _bdot(qe[i], st[h])
            st[h] = jnp.concatenate([st[h][:, j * LANES:(j + 1) * LANES] * dec[i] for j in range(dv // LANES)],
                                    axis=1) + upd[i]
            on = o * lax.rsqrt(jnp.mean(o * o, axis=-1, keepdims=True) + NORM_EPS) * nw
            vs = slice(h * dv, (h + 1) * dv)
            o_ref[rows(c), vs] = (on * gate_ref[rows(c), vs]).astype(o_ref.dtype)
        for h in range(hb):
            s_scr[h] = st[h]
        return carry

    lax.fori_loop(0, nch // ug, group, 0)
    sout_ref[0] = s_scr[...]


def _la_chunk_call(q, k, v, g, gate, norm_w, s0, bsz, t, cl, hb, name):
    m = q.shape[0]
    _, nh, dk, dv = s0.shape
    kern = functools.partial(_la_chunk_kernel, hb=hb, dk=dk, dv=dv, cl=cl, nch=t // cl)
    kspec = pl.BlockSpec((t, hb * dk), lambda bi, hi: (bi, hi))
    vspec = pl.BlockSpec((t, hb * dv), lambda bi, hi: (bi, hi))
    sspec = pl.BlockSpec((1, hb, dk, dv), lambda bi, hi: (bi, hi, 0, 0))
    return pl.pallas_call(
        kern,
        grid=(bsz, nh // hb),
        in_specs=[kspec, kspec, vspec, kspec, vspec, pl.BlockSpec((1, dv), lambda bi, hi: (0, 0)), sspec],
        out_specs=[vspec, sspec],
        out_shape=[jax.ShapeDtypeStruct((m, nh * dv), BF16), jax.ShapeDtypeStruct(s0.shape, F32)],
        scratch_shapes=[pltpu.VMEM((hb, dk, dv), F32)],
        compiler_params=_cparams(("parallel", "parallel")),
        name=name,
    )(q, k, v, g, gate, _row(norm_w), s0)


def _la_step_kernel(q_ref, k_ref, v_ref, g_ref, gate_ref, nw_ref, s0_ref, o_ref, sout_ref, *, nh, dk, dv):
    first = _iota((STEP_PAD, 1), 0) == 0

    def pad(x):
        return jnp.where(first, x, 0.0)

    hg = min(nh, 4)

    def group(gi, carry):
        hs = [gi * hg + j for j in range(hg)]
        rows = lambda ref: [ref[0, pl.ds(h, 1), :] for h in hs]
        dec = [jnp.exp(jnp.transpose(jnp.broadcast_to(g, (LANES, dk)))) for g in rows(g_ref)]
        upd = [_bdot_tn(pad(k), pad(v)) for k, v in zip(rows(k_ref), rows(v_ref))]
        sn = [jnp.concatenate([s0_ref[0, h][:, j * LANES:(j + 1) * LANES] * d for j in range(dv // LANES)], axis=1)
              + u for h, d, u in zip(hs, dec, upd)]
        for h, s in zip(hs, sn):
            sout_ref[0, h] = s
        os_ = [_bdot(pad(q), s)[0:1] for q, s in zip(rows(q_ref), sn)]
        for h, o, gt in zip(hs, os_, rows(gate_ref)):
            on = o * lax.rsqrt(jnp.mean(o * o, axis=-1, keepdims=True) + NORM_EPS) * nw_ref[...]
            o_ref[0, pl.ds(h, 1), :] = (on * gt).astype(o_ref.dtype)
        return carry

    lax.fori_loop(0, nh // hg, group, 0)


def _la_step_call(q, k, v, g, gate, norm_w, s0, name):
    bsz, nh, dk, dv = s0.shape
    kh = lambda x: x.reshape(bsz, nh, dk)
    vh = lambda x: x.reshape(bsz, nh, dv)
    kspec = pl.BlockSpec((1, nh, dk), lambda bi: (bi, 0, 0))
    vspec = pl.BlockSpec((1, nh, dv), lambda bi: (bi, 0, 0))
    sspec = pl.BlockSpec((1, nh, dk, dv), lambda bi: (bi, 0, 0, 0))
    o, s = pl.pallas_call(
        functools.partial(_la_step_kernel, nh=nh, dk=dk, dv=dv),
        grid=(bsz,),
        in_specs=[kspec, kspec, vspec, kspec, vspec, pl.BlockSpec((1, dv), lambda bi: (0, 0)), sspec],
        out_specs=[vspec, sspec],
        out_shape=[jax.ShapeDtypeStruct((bsz, nh, dv), F32), jax.ShapeDtypeStruct(s0.shape, F32)],
        compiler_params=_cparams(("parallel",)),
        name=name,
    )(kh(q), kh(k), vh(v), kh(g), vh(gate), _row(norm_w), s0)
    return o.reshape(bsz, nh * dv), s


def _la_run(q, k, v, g, gate, norm_w, s0, bsz, t, hb, name):
    if t == 1:
        return _la_step_call(q, k, v, g, gate, norm_w, s0, name + "_step")
    return _la_chunk_call(q, k, v, g, gate, norm_w, s0, bsz, t, LA_CHUNK, hb, name)


def _norm_prologue(h_ref, nw_ref, xn_s):
    @pl.when(pl.program_id(1) == 0)
    def _():
        xn_s[...] = _rms(h_ref[...], nw_ref[...]).astype(BF16)


def _gla_qkg_body(row, full, col, rc, out, scr, *, scale):
    nw_ref, gk1_ref = full
    wq, wk, gk2, bgk = col
    xn_s, t1_s = scr

    @pl.when(pl.program_id(1) == 0)
    def _():
        xn = _rms(row[0][...], nw_ref[...]).astype(BF16)
        xn_s[...] = xn
        t1_s[...] = jnp.dot(xn, gk1_ref[...], preferred_element_type=F32).astype(BF16)

    xn = xn_s[...]
    out[0][...] = jnp.dot(xn, wq[...], preferred_element_type=F32) * scale
    out[1][...] = jnp.dot(xn, wk[...], preferred_element_type=F32)
    lg = jnp.dot(t1_s[...], gk2[...], preferred_element_type=F32) + bgk[...]
    out[2][...] = jax.nn.log_sigmoid(lg) / GLA_GATE_TEMP


def _vg_body(row, full, col, rc, out, scr):
    _norm_prologue(row[0], full[0], scr[0])
    xn = scr[0][...]
    tn = col[0].shape[1]
    for cs in (slice(0, tn // 2), slice(tn // 2, tn)):
        out[0][:, cs] = jnp.dot(xn, col[0][:, cs], preferred_element_type=F32)
        out[1][:, cs] = jax.nn.silu(jnp.dot(xn, col[1][:, cs], preferred_element_type=F32))


def _gla_layer(h, bsz, t, s0, norm_w, wts):
    m, dm = h.shape
    tm = _tiles(m, t)
    nh = GLA_HEADS
    dk = wts['w_q'].shape[1] // nh
    rank = wts['gk1'].shape[1]
    q, k, lg = _proj_call(functools.partial(_gla_qkg_body, scale=dk ** -0.5), m=m, n=nh * dk, tm=tm, tn=512,
                          row_ins=[h], full_ins=[_row(norm_w), wts['gk1']],
                          col_ins=[wts['w_q'], wts['w_k'], wts['gk2'], _row(wts['b_gk'])], rc_ins=[],
                          out_dtypes=[F32] * 3,
                          scratch=[pltpu.VMEM((tm, dm), BF16), pltpu.VMEM((tm, rank), BF16)], name="gla_qkg")
    v, gate = _proj_call(_vg_body, m=m, n=wts['w_v'].shape[1], tm=tm, tn=512, row_ins=[h],
                         full_ins=[_row(norm_w)], col_ins=[wts['w_v'], wts['w_g']], rc_ins=[],
                         out_dtypes=[F32] * 2, scratch=[pltpu.VMEM((tm, dm), BF16)], name="gla_vg")
    og, s_new = _la_run(q, k, v, lg, gate, wts['norm'], s0, bsz, t, 1, "gla_chunk")
    return _outproj_call(og, wts['w_o'], h, t, "gla_out"), s_new


def _hgrn_proj_body(row, full, col, rc, out, scr, *, scale, layer_idx):
    wq, wf, wi, wg, lbp = col
    _norm_prologue(row[0], full[0], scr[0])
    xn = scr[0][...]
    tn = wq.shape[1]
    for cs in (slice(0, tn // 2), slice(tn // 2, tn)):
        dot = lambda w: jnp.dot(xn, w[:, cs], preferred_element_type=F32)
        z, qv, iv, gv = dot(wf), dot(wq), dot(wi), dot(wg)
        lbx = lbp[:, cs]
        ex = jnp.exp(lbx - jnp.max(lbx, axis=0, keepdims=True))
        lb = jnp.sum(ex[1:layer_idx + 1], axis=0, keepdims=True) / jnp.sum(ex, axis=0, keepdims=True)
        e = jnp.exp(-jnp.abs(z))
        rcp = 1.0 / (1.0 + e)
        pos = z >= 0
        sig = jnp.where(pos, rcp, e * rcp)
        sig_neg = jnp.where(pos, e * rcp, rcp)
        out[0][:, cs] = qv * scale
        out[1][:, cs] = (1.0 - lb) * sig_neg
        out[2][:, cs] = jnp.log(lb + (1.0 - lb) * sig)
        out[3][:, cs] = iv
        out[4][:, cs] = jax.nn.silu(gv)


def _hgrn_layer(h, bsz, t, s0, norm_w, wts, layer_idx):
    m, dm = h.shape
    tm = _tiles(m, t)
    body = functools.partial(_hgrn_proj_body, scale=HG_EXPAND ** -0.5, layer_idx=layer_idx)
    q, k, lf, v, gate = _proj_call(body, m=m, n=dm, tm=tm, tn=512, row_ins=[h], full_ins=[_row(norm_w)],
                                   col_ins=[wts['w_q'], wts['w_f'], wts['w_i'], wts['w_g'], wts['lb']],
                                   rc_ins=[], out_dtypes=[F32] * 5, scratch=[pltpu.VMEM((tm, dm), BF16)],
                                   name="hgrn_proj")
    og, s_new = _la_run(q, k, v, lf, gate, wts['norm'], s0, bsz, t, 2, "hgrn_chunk")
    return _outproj_call(og, wts['w_o'], h, t, "hgrn_out"), s_new


def _ffn_kernel(*refs, step, tm, tf, nf, seq_tiles, final_norm):
    h_ref, nw_ref, wup_ref, wgate_ref, wc_ref, bc_ref, wdown_ref = refs[:7]
    p = 7
    if step:
        p0_ref, p1_ref = refs[p:p + 2]; p += 2
    if final_norm:
        fnw_ref = refs[p]; p += 1
    out_ref, tail_ref = refs[p:p + 2]
    xn_s, carry_s = refs[p + 2:]
    i = pl.program_id(0)
    f = pl.program_id(1)

    @pl.when(f == 0)
    def _():
        hv = h_ref[...]
        xn_s[...] = _rms(hv, nw_ref[...]).astype(BF16)
        out_ref[...] = hv

    xn = xn_s[...]
    if not step:
        @pl.when(i % seq_tiles == 0)
        def _():
            carry_s[f] = jnp.zeros((SUBLANES, tf), F32)

    hw = tf // 2
    halves = [slice(0, hw), slice(hw, tf)]
    us = [jnp.dot(xn, wup_ref[:, cs], preferred_element_type=F32) for cs in halves]
    gates = [jnp.dot(xn, wgate_ref[:, cs], preferred_element_type=F32) for cs in halves]
    rid = _iota((tm, 1), 0)
    down = None
    for cs, u, gate in zip(halves, us, gates):
        wc = wc_ref[:, cs]
        if step:
            u2, u1 = p0_ref[:, cs], p1_ref[:, cs]
            tail_ref[:, cs] = u
        else:
            prev = carry_s[f][:, cs]
            u1 = jnp.where(rid == 0, prev[SUBLANES - 1:SUBLANES], pltpu.roll(u, 1, 0))
            u2 = jnp.where(rid == 0, prev[SUBLANES - 2:SUBLANES - 1],
                           jnp.where(rid == 1, prev[SUBLANES - 1:SUBLANES], pltpu.roll(u, 2, 0)))
            tail_ref[0, :, cs] = u[tm - SUBLANES:tm]
        c = u2 * wc[0:1] + u1 * wc[1:2] + u * wc[2:3] + bc_ref[:, cs]
        act = (jax.nn.gelu(c) * gate).astype(BF16)
        d = jnp.dot(act, wdown_ref[cs, :], preferred_element_type=F32)
        down = d if down is None else down + d
    if not step:
        carry_s[f] = jnp.concatenate([u[tm - SUBLANES:tm] for u in us], axis=1)
    out_ref[...] += down
    if final_norm:
        @pl.when(f == nf - 1)
        def _():
            out_ref[...] = _rms(out_ref[...], fnw_ref[...])


def _ffn_call(h, bsz, t, conv_prev, norm_w, layer, w_up, w_gate, w_conv, b_conv, w_down, final_norm_w):
    m, dm = h.shape
    dff = w_up.shape[-1]
    step = t == 1
    tm = _tiles(m, t)
    tf = 512
    nf = dff // tf
    seq_tiles = 1 if step else t // tm
    final_norm = final_norm_w is not None
    args = [h, _row(norm_w), w_up, w_gate, w_conv.astype(F32), _row(b_conv.astype(F32)), w_down]
    specs = [pl.BlockSpec((tm, dm), lambda i, f: (i, 0)), pl.BlockSpec((1, dm), lambda i, f: (0, 0)),
             pl.BlockSpec((None, dm, tf), lambda i, f: (layer, 0, f)),
             pl.BlockSpec((None, dm, tf), lambda i, f: (layer, 0, f)),
             pl.BlockSpec((CONV_W, tf), lambda i, f: (0, f)), pl.BlockSpec((1, tf), lambda i, f: (0, f)),
             pl.BlockSpec((None, tf, dm), lambda i, f: (layer, f, 0))]
    if step:
        args += [conv_prev[:, 0].astype(F32), conv_prev[:, 1].astype(F32)]
        specs += [pl.BlockSpec((tm, tf), lambda i, f: (i, f))] * 2
        tail_shape = jax.ShapeDtypeStruct((m, dff), F32)
        tail_spec = pl.BlockSpec((tm, tf), lambda i, f: (i, f))
    else:
        tail_shape = jax.ShapeDtypeStruct((m // tm, SUBLANES, dff), F32)
        tail_spec = pl.BlockSpec((1, SUBLANES, tf), lambda i, f: (i, 0, f))
    if final_norm:
        args.append(_row(final_norm_w))
        specs.append(pl.BlockSpec((1, dm), lambda i, f: (0, 0)))
    kern = functools.partial(_ffn_kernel, step=step, tm=tm, tf=tf, nf=nf, seq_tiles=seq_tiles,
                             final_norm=final_norm)
    out, tail = pl.pallas_call(
        kern,
        grid=(m // tm, nf),
        in_specs=specs,
        out_specs=[pl.BlockSpec((tm, dm), lambda i, f: (i, 0)), tail_spec],
        out_shape=[jax.ShapeDtypeStruct((m, dm), F32), tail_shape],
        scratch_shapes=[pltpu.VMEM((tm, dm), BF16), pltpu.VMEM((nf, SUBLANES, tf), F32)],
        compiler_params=_cparams(("arbitrary", "arbitrary")),
        name="conv_ffn",
    )(*args)
    if step:
        conv_new = jnp.stack([conv_prev[:, 1].astype(F32), tail], axis=1)
    else:
        conv_new = tail.reshape(bsz, seq_tiles, SUBLANES, dff)[:, -1, SUBLANES - (CONV_W - 1):]
    return out, conv_new


def _trunk(x, s5_re, s5_im, rw_shift, rw_wkv, gla_s, hg_s, ffn_conv, p):
    bsz, t, dm = x.shape
    h = x.reshape(bsz * t, dm).astype(F32)
    depth = p['norm_mix'].shape[0]
    conv_new = []
    for i in range(depth):
        kind = i % 4
        nw = p['norm_mix'][i]
        if kind == 0:
            h, s5_re, s5_im = _s5_layer(h, bsz, t, s5_re, s5_im, nw, p['s5'], p['s5_c_re'], p['s5_c_im'],
                                        p['s5_d'], p['s5_w_glu'])
        elif kind == 1:
            h, rw_shift, rw_wkv = _rwkv_layer(h, bsz, t, rw_shift, rw_wkv, nw, p['rw'])
        elif kind == 2:
            h, gla_s = _gla_layer(h, bsz, t, gla_s, nw, p['gla'])
        else:
            h, hg_s = _hgrn_layer(h, bsz, t, hg_s, nw, p['hg'], i)
        fin = p['norm_final'] if i == depth - 1 else None
        h, c = _ffn_call(h, bsz, t, ffn_conv[i], p['norm_ffn'][i], i, p['ffn_w_up'], p['ffn_w_gate'],
                         p['ffn_w_conv'][i], p['ffn_b_conv'][i], p['ffn_w_down'], fin)
        conv_new.append(c)
    return h.reshape(bsz, t, dm), s5_re, s5_im, rw_shift, rw_wkv, gla_s, hg_s, jnp.stack(conv_new)


def _pad_cols(w, n):
    return jnp.pad(w, ((0, 0), (0, n - w.shape[1])))


def _pad_rows(w, n):
    return jnp.pad(w, ((0, n - w.shape[0]), (0, 0)))


def kernel(x_prompt, x_sample, state_s5_re, state_s5_im, state_rwkv_shift, state_rwkv_wkv, state_gla, state_hgrn, state_ffn_conv, norm_mix, norm_ffn, norm_final, s5_lambda_re, s5_lambda_im, s5_log_dt, s5_b_re, s5_b_im, s5_c_re, s5_c_im, s5_d, s5_w_glu, rw_mix, rw_w_r, rw_w_k, rw_w_v, rw_w_o, rw_w0, rw_w1, rw_w2, rw_a0, rw_a1, rw_a2, rw_g1, rw_g2, rw_k_k, rw_k_a, rw_r_k, rw_ln_w, rw_ln_b, gla_w_q, gla_w_k, gla_w_v, gla_w_gk1, gla_w_gk2, gla_b_gk, gla_w_g, gla_norm, gla_w_o, hg_w_q, hg_w_f, hg_w_i, hg_w_g, hg_lb, hg_norm, hg_w_o, ffn_w_up, ffn_w_gate, ffn_w_conv, ffn_b_conv, ffn_w_down):
    bf = lambda w: w.astype(BF16)
    f32 = lambda w: w.astype(F32)
    lora = LANES * pl.cdiv(rw_w1.shape[1], LANES)
    alora = LANES * pl.cdiv(rw_a1.shape[1], LANES)
    grank = LANES * pl.cdiv(gla_w_gk1.shape[1], LANES)
    p = {
        'norm_mix': f32(norm_mix), 'norm_ffn': f32(norm_ffn), 'norm_final': f32(norm_final),
        's5': _s5_params(s5_lambda_re, s5_lambda_im, s5_log_dt, s5_b_re, s5_b_im, s5_c_re, s5_c_im, s5_d),
        's5_c_re': s5_c_re, 's5_c_im': s5_c_im, 's5_d': s5_d, 's5_w_glu': bf(s5_w_glu),
        'rw': dict(mix=_pad_rows(f32(rw_mix), SUBLANES), w_r=bf(rw_w_r), w_k=bf(rw_w_k), w_v=bf(rw_w_v),
                   w_o=bf(rw_w_o), w0=f32(rw_w0), w1=bf(_pad_cols(rw_w1, lora)), w2=bf(_pad_rows(rw_w2, lora)),
                   a0=f32(rw_a0), a1=bf(_pad_cols(rw_a1, alora)), a2=bf(_pad_rows(rw_a2, alora)),
                   g1=bf(rw_g1), g2=bf(rw_g2), k_k=f32(rw_k_k), k_a=f32(rw_k_a), r_k=f32(rw_r_k).reshape(-1),
                   ln_w=f32(rw_ln_w), ln_b=f32(rw_ln_b)),
        'gla': dict(w_q=bf(gla_w_q), w_k=bf(gla_w_k), w_v=bf(gla_w_v), gk1=bf(_pad_cols(gla_w_gk1, grank)),
                    gk2=bf(_pad_rows(gla_w_gk2, grank)), b_gk=f32(gla_b_gk), w_g=bf(gla_w_g), norm=f32(gla_norm),
                    w_o=bf(gla_w_o)),
        'hg': dict(w_q=bf(hg_w_q), w_f=bf(hg_w_f), w_i=bf(hg_w_i), w_g=bf(hg_w_g), lb=f32(hg_lb),
                   norm=f32(hg_norm), w_o=bf(hg_w_o)),
        'ffn_w_up': bf(ffn_w_up), 'ffn_w_gate': bf(ffn_w_gate), 'ffn_w_conv': ffn_w_conv,
        'ffn_b_conv': ffn_b_conv, 'ffn_w_down': bf(ffn_w_down),
    }
    nb, _, dm = x_prompt.shape
    depth = norm_mix.shape[0]
    dff = ffn_w_up.shape[-1]
    z_s5 = jnp.zeros((nb,) + state_s5_re.shape[1:], F32)
    outs_p = _trunk(x_prompt, z_s5, z_s5, jnp.zeros((nb, dm), F32),
                    jnp.zeros((nb,) + state_rwkv_wkv.shape[1:], F32), jnp.zeros((nb,) + state_gla.shape[1:], F32),
                    jnp.zeros((nb,) + state_hgrn.shape[1:], F32), jnp.zeros((depth, nb, CONV_W - 1, dff), F32), p)
    outs_s = _trunk(x_sample, f32(state_s5_re), f32(state_s5_im), state_rwkv_shift, f32(state_rwkv_wkv),
                    f32(state_gla), f32(state_hgrn), state_ffn_conv, p)
    return (outs_p[0], outs_s[0]) + tuple(outs_p[1:]) + tuple(outs_s[1:])
```

```python
import functools
import math

import jax
import jax.numpy as jnp
from jax import lax
from jax.experimental import pallas as pl
from jax.experimental.pallas import tpu as pltpu

F32 = jnp.float32
BF16 = jnp.bfloat16

NORM_EPS = 1e-6
RW_LN_EPS = 64e-5
RW_HEAD = 64
S5_GROUP = 16
S5_STATE = 64
S5_CHUNK = 16
GLA_HEADS = 4
GLA_GATE_TEMP = 16.0
HG_EXPAND = 128
CONV_W = 3
LA_CHUNK = 64
RW_CHUNK = 64
STEP_PAD = 16
LANES = 128
SUBLANES = 8
VMEM_LIMIT = 56 * 1024 * 1024


def _cparams(sem):
    return pltpu.CompilerParams(dimension_semantics=sem, vmem_limit_bytes=VMEM_LIMIT)


def _bdot(a, b):
    return jnp.dot(a.astype(BF16), b.astype(BF16), preferred_element_type=F32)


def _bdot_nt(a, b):
    return lax.dot_general(a.astype(BF16), b.astype(BF16), (((1,), (1,)), ((), ())),
                           preferred_element_type=F32)


def _bdot_tn(a, b):
    return lax.dot_general(a.astype(BF16), b.astype(BF16), (((0,), (0,)), ((), ())),
                           preferred_element_type=F32)


def _split3(x):
    hi = x.astype(BF16)
    r = x - hi.astype(F32)
    mid = r.astype(BF16)
    lo = (r - mid.astype(F32)).astype(BF16)
    return hi, mid, lo


def _xdot_l(m, x):
    hi, mid, lo = _split3(x)
    d = lambda p: jnp.dot(m, p, preferred_element_type=F32)
    return d(hi) + d(mid) + d(lo)


def _xdot_r(x, m):
    hi, mid, lo = _split3(x)
    d = lambda p: jnp.dot(p, m, preferred_element_type=F32)
    return d(hi) + d(mid) + d(lo)


def _rms(x, g):
    y = x * lax.rsqrt(jnp.mean(x * x, axis=-1, keepdims=True) + NORM_EPS)
    return y * g


def _iota(shape, dim):
    return lax.broadcasted_iota(jnp.int32, shape, dim)


def _rmsnorm_kernel(x_ref, g_ref, o_ref):
    o_ref[...] = _rms(x_ref[...], g_ref[...])


def _rmsnorm_call(x, g):
    r, d = x.shape
    tr = 512 if r % 512 == 0 else r
    return pl.pallas_call(
        _rmsnorm_kernel,
        grid=(r // tr,),
        in_specs=[pl.BlockSpec((tr, d), lambda i: (i, 0)), pl.BlockSpec((1, d), lambda i: (0, 0))],
        out_specs=pl.BlockSpec((tr, d), lambda i: (i, 0)),
        out_shape=jax.ShapeDtypeStruct((r, d), F32),
        compiler_params=_cparams(("parallel",)),
        name="rmsnorm",
    )(x, g.reshape(1, d))


def _proj_call(body, *, m, n, tm, tn, row_ins, full_ins, col_ins, rc_ins, out_dtypes, scratch, name):
    args, specs = [], []
    for a in row_ins:
        if isinstance(a, tuple):
            args.append(a[0]); specs.append(a[1])
        else:
            args.append(a); specs.append(pl.BlockSpec((tm, a.shape[1]), lambda i, j: (i, 0)))
    for a in full_ins:
        args.append(a); specs.append(pl.BlockSpec(a.shape, lambda i, j, nd=a.ndim: (0,) * nd))
    for a in col_ins:
        args.append(a); specs.append(pl.BlockSpec((a.shape[0], tn), lambda i, j: (0, j)))
    for a in rc_ins:
        args.append(a); specs.append(pl.BlockSpec((tm, tn), lambda i, j: (i, j)))
    nr, nf, nc, nrc, no = len(row_ins), len(full_ins), len(col_ins), len(rc_ins), len(out_dtypes)

    def kernel(*refs):
        p = 0
        groups = []
        for cnt in (nr, nf, nc, nrc, no):
            groups.append(refs[p:p + cnt]); p += cnt
        body(*groups, refs[p:])

    outs = pl.pallas_call(
        kernel,
        grid=(m // tm, n // tn),
        in_specs=specs,
        out_specs=[pl.BlockSpec((tm, tn), lambda i, j: (i, j)) for _ in out_dtypes],
        out_shape=[jax.ShapeDtypeStruct((m, n), dt) for dt in out_dtypes],
        scratch_shapes=scratch,
        compiler_params=_cparams(("parallel", "arbitrary")),
        name=name,
    )(*args)
    return outs


def _row(v):
    return v.reshape(1, -1)


def _tiles(m, t):
    if t == 1:
        return 512 if m % 512 == 0 else m
    return min(512, t)


def _s5_params(lam_re, lam_im, log_dt, b_re, b_im, c_re, c_im, d):
    hp = lax.Precision.HIGHEST
    g, n = lam_re.shape
    p = b_re.shape[-1]
    cl = S5_CHUNK
    lr = jnp.minimum(lam_re.astype(F32), -1e-4)
    li = lam_im.astype(F32)
    dt = jnp.exp(log_dt.astype(F32))[:, None]
    mag = jnp.exp(lr * dt)
    ab_re = mag * jnp.cos(li * dt)
    ab_im = mag * jnp.sin(li * dt)
    den = lr * lr + li * li
    f_re = ((ab_re - 1.0) * lr + ab_im * li) / den
    f_im = (ab_im * lr - (ab_re - 1.0) * li) / den
    br, bi = b_re.astype(F32), b_im.astype(F32)
    bb_re = f_re[..., None] * br - f_im[..., None] * bi
    bb_im = f_re[..., None] * bi + f_im[..., None] * br
    cr, ci = c_re.astype(F32), c_im.astype(F32)

    pr, pi = [jnp.ones_like(ab_re)], [jnp.zeros_like(ab_im)]
    for _ in range(cl):
        pr.append(pr[-1] * ab_re - pi[-1] * ab_im)
        pi.append(pr[-2] * ab_im + pi[-1] * ab_re)
    pw_re = jnp.stack(pr)
    pw_im = jnp.stack(pi)
    ajb_re = pw_re[:cl, :, :, None] * bb_re - pw_im[:cl, :, :, None] * bb_im
    ajb_im = pw_re[:cl, :, :, None] * bb_im + pw_im[:cl, :, :, None] * bb_re
    kj = (jnp.einsum('gpn,jgnq->jgpq', cr, ajb_re, precision=hp)
          - jnp.einsum('gpn,jgnq->jgpq', ci, ajb_im, precision=hp))
    kq = kj.transpose(1, 3, 0, 2).reshape(g, p, cl * p)
    w_re = ajb_re[::-1].transpose(1, 0, 3, 2).reshape(g, cl * p, n)
    w_im = ajb_im[::-1].transpose(1, 0, 3, 2).reshape(g, cl * p, n)
    wm = jnp.concatenate([w_re, w_im], axis=-1)
    ca_re = cr[None] * pw_re[1:, :, None, :] - ci[None] * pw_im[1:, :, None, :]
    ca_im = cr[None] * pw_im[1:, :, None, :] + ci[None] * pw_re[1:, :, None, :]
    v_re = ca_re.transpose(1, 3, 0, 2).reshape(g, n, cl * p)
    v_im = -ca_im.transpose(1, 3, 0, 2).reshape(g, n, cl * p)
    vm = jnp.concatenate([v_re, v_im], axis=1)
    ar, ai = pw_re[cl], pw_im[cl]
    pcs, qcs = [], []
    for _ in range(SUBLANES):
        pcs.append(jnp.concatenate([ar, ar], axis=-1))
        qcs.append(jnp.concatenate([-ai, ai], axis=-1))
        ar, ai = ar * ar - ai * ai, 2.0 * ar * ai
    pc = jnp.stack(pcs, axis=1)
    qc = jnp.stack(qcs, axis=1)
    return dict(ab_re=ab_re, ab_im=ab_im, bb_re=bb_re, bb_im=bb_im, kq=kq, wm=wm.astype(BF16),
                vm=vm.astype(BF16), pc=pc, qc=qc)


def _s5_seq_kernel(x_ref, kq_ref, w_ref, vm_ref, pc_ref, qc_ref, d_ref, h0_ref, z_ref, hf_ref, mt_scr, *, gb, nc):
    half = S5_STATE
    cl, p = S5_CHUNK, S5_GROUP
    per = LANES // p
    rid = _iota((nc, 1), 0)
    lane_blk = _iota((1, LANES), 1) // p
    nlev = int(math.log2(nc))

    @pl.when(pl.program_id(1) == 0)
    def _():
        lane_w = _iota((1, cl * p), 1)
        for g in range(gb):
            kq = kq_ref[g]
            blocks = [kq] + [jnp.where(lane_w >= s * p, pltpu.roll(kq, s * p, 1), 0.0) for s in range(1, cl)]
            mt_scr[g] = jnp.concatenate(blocks, axis=0).astype(BF16)

    xs = [x_ref[0, :, s, :] for s in range(cl)]
    groups = range(gb)

    def to_chunk(g):
        pieces = []
        for hf in range(cl // per):
            acc = None
            for j in range(per):
                src = xs[hf * per + j]
                sh = ((j - g) * p) % LANES
                src = pltpu.roll(src, sh, 1) if sh else src
                acc = src if acc is None else jnp.where(lane_blk == j, src, acc)
            pieces.append(acc)
        return jnp.concatenate(pieces, axis=1).astype(BF16)

    ub = [to_chunk(g) for g in groups]
    y = [jnp.dot(u, mt_scr[g], preferred_element_type=F32) for g, u in zip(groups, ub)]
    x = [jnp.dot(u, w_ref[g], preferred_element_type=F32) for g, u in zip(groups, ub)]
    hprev = []
    for g in groups:
        pc = pc_ref[g]
        qc = qc_ref[g]
        h0 = h0_ref[0, g]
        h0c = pc[0:1] * h0 + qc[0:1] * pltpu.roll(h0, half, 1)
        xg = x[g] + jnp.where(rid == 0, h0c, 0.0)
        for lv in range(nlev):
            sh = 1 << lv
            xsft = jnp.where(rid >= sh, pltpu.roll(xg, sh, 0), 0.0)
            xg = xg + pc[lv:lv + 1] * xsft + qc[lv:lv + 1] * pltpu.roll(xsft, half, 1)
        hf_ref[0, g] = xg[nc - 1:nc, :]
        hprev.append(jnp.where(rid == 0, h0, pltpu.roll(xg, 1, 0)).astype(BF16))
    y = [yy + jnp.dot(hp, vm_ref[g], preferred_element_type=F32) for g, yy, hp in zip(groups, y, hprev)]
    for t in range(cl):
        hf, j = divmod(t, per)
        acc = None
        for g in groups:
            src = y[g][:, hf * LANES:(hf + 1) * LANES]
            sh = ((g - j) * p) % LANES
            src = pltpu.roll(src, sh, 1) if sh else src
            acc = src if acc is None else jnp.where(lane_blk == g, src, acc)
        z_ref[0, :, t, :] = jax.nn.gelu(acc + d_ref[...] * xs[t])


def _s5_seq_call(x4, prm, d, h0):
    b, nc, cl, dm = x4.shape
    g = dm // S5_GROUP
    w = cl * S5_GROUP
    n2 = 2 * S5_STATE
    gb = LANES // S5_GROUP
    kern = functools.partial(_s5_seq_kernel, gb=gb, nc=nc)
    gspec = lambda shape: pl.BlockSpec((gb,) + shape, lambda gi, bi: (gi, 0, 0))
    xspec = pl.BlockSpec((1, nc, cl, LANES), lambda gi, bi: (bi, 0, 0, gi))
    return pl.pallas_call(
        kern,
        grid=(g // gb, b),
        in_specs=[xspec,
                  gspec((S5_GROUP, w)), gspec((w, n2)), gspec((n2, w)), gspec((SUBLANES, n2)), gspec((SUBLANES, n2)),
                  pl.BlockSpec((1, LANES), lambda gi, bi: (0, gi)),
                  pl.BlockSpec((1, gb, 1, n2), lambda gi, bi: (bi, gi, 0, 0))],
        out_specs=[xspec, pl.BlockSpec((1, gb, 1, n2), lambda gi, bi: (bi, gi, 0, 0))],
        out_shape=[jax.ShapeDtypeStruct(x4.shape, F32), jax.ShapeDtypeStruct((b, g, 1, n2), F32)],
        scratch_shapes=[pltpu.VMEM((gb, w, w), BF16)],
        compiler_params=_cparams(("parallel", "arbitrary")),
        name="s5_seq",
    )(x4, prm['kq'], prm['wm'], prm['vm'], prm['pc'], prm['qc'], _row(d.astype(F32)), h0)


def _s5_step_kernel(u_ref, hr_ref, hi_ref, ar_ref, ai_ref, bdr_ref, bdi_ref, cdr_ref, cdi_ref, d_ref,
                    z_ref, hro_ref, hio_ref):
    u = u_ref[...]
    ar, ai = ar_ref[...], ai_ref[...]
    hr, hi = hr_ref[...], hi_ref[...]
    hr2 = ar * hr - ai * hi + _bdot(u, bdr_ref[0])
    hi2 = ar * hi + ai * hr + _bdot(u, bdi_ref[0])
    y = _bdot(hr2, cdr_ref[0]) - _bdot(hi2, cdi_ref[0]) + d_ref[...] * u
    z_ref[...] = jax.nn.gelu(y)
    hro_ref[...] = hr2
    hio_ref[...] = hi2


def _s5_step_call(u, h_re, h_im, prm, c_re, c_im, d):
    b, dm = u.shape
    g, n = prm['ab_re'].shape
    p = S5_GROUP
    gpb = LANES // p
    nb = g // gpb
    eye = jnp.eye(gpb, dtype=F32)
    bd = lambda bb: jnp.einsum('kgnp,gh->kgphn', bb.reshape(nb, gpb, n, p), eye).reshape(nb, gpb * p, gpb * n)
    cd = lambda cc: jnp.einsum('kgpn,gh->kgnhp', cc.astype(F32).reshape(nb, gpb, p, n), eye).reshape(
        nb, gpb * n, gpb * p)
    bdr, bdi = bd(prm['bb_re']).astype(BF16), bd(prm['bb_im']).astype(BF16)
    cdr, cdi = cd(c_re).astype(BF16), cd(c_im).astype(BF16)
    wn = gpb * n
    cspec = lambda wd: pl.BlockSpec((b, wd), lambda k: (0, k))
    rspec = lambda wd: pl.BlockSpec((1, wd), lambda k: (0, k))
    z, hr2, hi2 = pl.pallas_call(
        _s5_step_kernel,
        grid=(nb,),
        in_specs=[cspec(LANES), cspec(wn), cspec(wn), rspec(wn), rspec(wn),
                  pl.BlockSpec((1, LANES, wn), lambda k: (k, 0, 0)), pl.BlockSpec((1, LANES, wn), lambda k: (k, 0, 0)),
                  pl.BlockSpec((1, wn, LANES), lambda k: (k, 0, 0)), pl.BlockSpec((1, wn, LANES), lambda k: (k, 0, 0)),
                  rspec(LANES)],
        out_specs=[cspec(LANES), cspec(wn), cspec(wn)],
        out_shape=[jax.ShapeDtypeStruct((b, dm), F32), jax.ShapeDtypeStruct((b, g * n), F32),
                   jax.ShapeDtypeStruct((b, g * n), F32)],
        compiler_params=_cparams(("parallel",)),
        name="s5_step",
    )(u, h_re.reshape(b, g * n), h_im.reshape(b, g * n), prm['ab_re'].reshape(1, g * n),
      prm['ab_im'].reshape(1, g * n), bdr, bdi, cdr, cdi, _row(d.astype(F32)))
    return z, hr2.reshape(b, g, n), hi2.reshape(b, g, n)


def _glu_body(row, full, col, rc, out, scr):
    (z_ref,) = row
    (w_ref,) = col
    zt_ref, h_ref = rc
    (zb,) = scr

    @pl.when(pl.program_id(1) == 0)
    def _():
        zb[...] = z_ref[...].astype(BF16)

    zt = zt_ref[...]
    gate = jax.nn.sigmoid(jnp.dot(zb[...], w_ref[...], preferred_element_type=F32))
    out[0][...] = h_ref[...] + zt * gate


def _s5_layer(h, bsz, t, st_re, st_im, norm_w, prm, c_re, c_im, d, w_glu):
    m, dm = h.shape
    g, n, p = dm // S5_GROUP, S5_STATE, S5_GROUP
    xn = _rmsnorm_call(h, norm_w)
    if t == 1:
        z, s_re, s_im = _s5_step_call(xn, st_re, st_im, prm, c_re, c_im, d)
    else:
        cl = S5_CHUNK
        nc = t // cl
        h0 = jnp.concatenate([st_re, st_im], axis=-1).reshape(bsz, g, 1, 2 * n)
        z4, hf = _s5_seq_call(xn.reshape(bsz, nc, cl, dm), prm, d, h0)
        z = z4.reshape(m, dm)
        s_re, s_im = hf[:, :, 0, :n], hf[:, :, 0, n:]
    tm = _tiles(m, t)
    (h_new,) = _proj_call(_glu_body, m=m, n=dm, tm=tm, tn=512, row_ins=[z], full_ins=[], col_ins=[w_glu],
                          rc_ins=[z, h], out_dtypes=[F32], scratch=[pltpu.VMEM((tm, dm), BF16)], name="s5_glu")
    return h_new, s_re, s_im


def _rwkv_proj_body(row, full, col, rc, out, scr, *, tm, seq_tiles, step):
    h_ref, prev_ref = row
    nw_ref, mix_ref, w1_ref, a1_ref, g1_ref, bd_ref = full
    wr, wk, wv, w2, a2, g2, w0, a0, kk_ref, ka_ref = col
    r_o, lw_o, k_o, v_o, a_o, b_o, g_o = out
    xr_s, xk_s, xv_s, tw_s, ta_s, tg_s = scr
    i = pl.program_id(0)

    @pl.when(pl.program_id(1) == 0)
    def _():
        nw = nw_ref[...]
        xn = _rms(h_ref[...], nw)
        if step:
            xprev = prev_ref[...]
        else:
            last = _rms(prev_ref[...], nw)[SUBLANES - 1:SUBLANES, :]
            last = jnp.where(i % seq_tiles == 0, 0.0, last)
            xprev = jnp.where(_iota((tm, 1), 0) == 0, last, pltpu.roll(xn, 1, 0))
        xx = xprev - xn
        mix = mix_ref[...]
        xr_s[...] = (xn + xx * mix[0:1]).astype(BF16)
        xk_s[...] = (xn + xx * mix[2:3]).astype(BF16)
        xv_s[...] = (xn + xx * mix[3:4]).astype(BF16)
        tw_s[...] = jnp.tanh(_bdot(xn + xx * mix[1:2], w1_ref[...])).astype(BF16)
        ta_s[...] = _bdot(xn + xx * mix[4:5], a1_ref[...]).astype(BF16)
        tg_s[...] = jax.nn.sigmoid(_bdot(xn + xx * mix[5:6], g1_ref[...])).astype(BF16)

    dot = lambda a, b: jnp.dot(a[...], b[...], preferred_element_type=F32)
    r = dot(xr_s, wr)
    k = dot(xk_s, wk)
    v = dot(xv_s, wv)
    lw = -jax.nn.sigmoid(w0[...] + dot(tw_s, w2)) * math.exp(-0.5)
    a = jax.nn.sigmoid(a0[...] + dot(ta_s, a2))
    kk = k * kk_ref[...]
    ss = _bdot(kk * kk, bd_ref[...])
    kk = kk / jnp.maximum(jnp.sqrt(ss), 1e-12)
    r_o[...] = r
    lw_o[...] = lw
    k_o[...] = k * (1.0 + (a - 1.0) * ka_ref[...])
    v_o[...] = v
    a_o[...] = -kk
    b_o[...] = kk * a
    g_o[...] = dot(tg_s, g2)


def _rwkv_chunk_kernel(r_ref, lw_ref, k_ref, v_ref, a_ref, b_ref, gate_ref, lnw_ref, lnb_ref, rk_ref, s0_ref,
                       o_ref, sout_ref, p_scr, rp_scr, o0_scr, m_scr, d_scr, g_scr, *, npair, cl, nch):
    hd = RW_HEAD
    pw = 2 * hd
    lane = _iota((1, pw), 1)
    head0 = lane < hd
    bdmask = (_iota((pw, pw), 0) // hd) == (_iota((pw, pw), 1) // hd)
    e_dup = ((_iota((hd, pw), 1) % hd) == _iota((hd, pw), 0)).astype(BF16)
    e_fold = ((_iota((pw, hd), 0) % hd) == _iota((pw, hd), 1)).astype(BF16)
    ones_bd = bdmask.astype(BF16)
    avg = (bdmask.astype(F32) * (1.0 / hd)).astype(BF16)
    for p in range(npair):
        p_scr[p] = jnp.where(bdmask, _xdot_r(s0_ref[0, p], e_dup), 0.0)
    s2 = 2 * cl
    rr = _iota((s2, s2), 0)
    cc = _iota((s2, s2), 1)
    same = (rr // cl) == (cc // cl)
    strict = same & ((rr % cl) > (cc % cl))
    incl = same & ((rr % cl) >= (cc % cl))
    eye_s = (rr == cc).astype(F32)
    rid = _iota((cl, 1), 0)
    zeros_s = jnp.zeros((s2, pw), F32)
    nround = int(math.log2(cl)) - 1

    def stack(x):
        return jnp.concatenate([jnp.where(head0, x, 0.0), jnp.where(head0, 0.0, x)], axis=0)

    ug = 4 if nch % 4 == 0 else 1
    cat = jnp.concatenate

    def tile(ref, c, p):
        return ref[pl.ds(pl.multiple_of(c * cl, cl), cl), p * pw:(p + 1) * pw]

    def phase_a(it, carry):
        ch = [(it * ug + j, p) for j in range(ug) for p in range(npair)]
        ld = lambda ref: [tile(ref, c, p) for c, p in ch]
        lw = ld(lw_ref)
        cs = lw
        sh = 1
        while sh < cl:
            cs = [x + jnp.where(rid >= sh, pltpu.roll(x, sh, 0), 0.0) for x in cs]
            sh *= 2
        gam = [jnp.exp(x) for x in cs]
        ginv = [jnp.exp(-x) for x in cs]
        ats = [stack(a * jnp.exp(x - l)) for a, x, l in zip(ld(a_ref), cs, lw)]
        rts = [stack(r * g) for r, g in zip(ld(r_ref), gam)]
        x1s = [cat([a, r], axis=0).astype(BF16) for a, r in zip(ats, rts)]
        x2s = [cat([stack(b * g), stack(k * g)], axis=0).astype(BF16)
               for b, k, g in zip(ld(b_ref), ld(k_ref), ginv)]
        gs = [_bdot_nt(x1, x2) for x1, x2 in zip(x1s, x2s)]
        nmat = [jnp.where(strict, g[:s2, :s2], 0.0) for g in gs]
        aak = [jnp.where(strict, g[:s2, s2:], 0.0) for g in gs]
        arb = [jnp.where(incl, g[s2:, :s2], 0.0) for g in gs]
        ark = [jnp.where(incl, g[s2:, s2:], 0.0) for g in gs]
        vs = [stack(v) for v in ld(v_ref)]
        akv = [_bdot(x, v) for x, v in zip(aak, vs)]
        rkv = [_bdot(x, v) for x, v in zip(ark, vs)]
        tinv = [eye_s + n for n in nmat]
        pk = nmat
        for _ in range(nround):
            pk = [_bdot(x, x) for x in pk]
            tinv = [t + _bdot(x, t) for x, t in zip(pk, tinv)]
        tz = [_bdot(t, cat([x, a], axis=1)) for t, x, a in zip(tinv, akv, ats)]
        u0s = [x[:, :pw] for x in tz]
        aps = [x[:, pw:] for x in tz]
        z = [_bdot(x, cat([a, u], axis=1)) for x, a, u in zip(arb, aps, u0s)]
        md = [_bdot_tn(cat([cat([a, u], axis=1), cat([zeros_s, v], axis=1)], axis=0), x2)
              for a, u, v, x2 in zip(aps, u0s, vs, x2s)]
        for i, (c, p) in enumerate(ch):
            rps = rts[i] + z[i][:, :pw]
            o0s = z[i][:, pw:] + rkv[i]
            rp_scr[c, p] = (rps[:cl] + rps[cl:]).astype(BF16)
            o0_scr[c, p] = o0s[:cl] + o0s[cl:]
            gl = gam[i][cl - 1:cl, :]
            m_scr[c, p] = (md[i][:pw] * gl).astype(BF16)
            d_scr[c, p] = md[i][pw:] * gl
            g_scr[c, p] = gl
        return carry

    lax.fori_loop(0, nch // ug, phase_a, 0)

    def phase_b(it, carry):
        cs = [it * ug + j for j in range(ug)]
        prs = range(npair)
        bonus = [[_bdot(tile(r_ref, c, p) * tile(k_ref, c, p) * rk_ref[:, p * pw:(p + 1) * pw], ones_bd)
                  * tile(v_ref, c, p) for p in prs] for c in cs]
        st = [p_scr[p] for p in prs]
        outs, mus, dlts, vrs = [], [], [], []
        for j in range(ug + 2):
            if j < ug:
                c = cs[j]
                new = [st[p] * g_scr[c, p] + _bdot(st[p], m_scr[c, p]) + d_scr[c, p] for p in prs]
                outs.append([_bdot_nt(rp_scr[c, p], st[p]) + o0_scr[c, p] for p in prs])
                st = new
            if 1 <= j <= ug:
                mus.append([_bdot(o, avg) for o in outs[j - 1]])
            if 2 <= j:
                dl = [o - m for o, m in zip(outs[j - 2], mus[j - 2])]
                dlts.append(dl)
                vrs.append([_bdot(d * d, avg) for d in dl])
        for p in prs:
            p_scr[p] = st[p]
        for j, c in enumerate(cs):
            for p in prs:
                sl = slice(p * pw, (p + 1) * pw)
                on = dlts[j][p] * lax.rsqrt(vrs[j][p] + RW_LN_EPS) * lnw_ref[:, sl] + lnb_ref[:, sl]
                o_ref[pl.ds(pl.multiple_of(c * cl, cl), cl), sl] = (
                    (on + bonus[j][p]) * tile(gate_ref, c, p)).astype(o_ref.dtype)
        return carry

    lax.fori_loop(0, nch // ug, phase_b, 0)
    for p in range(npair):
        sout_ref[0, p] = _xdot_r(jnp.where(bdmask, p_scr[p], 0.0), e_fold)


def _rwkv_chunk_call(r, lw, k, v, a, b, gate, ln_w, ln_b, r_k, s0, bsz, t, cl):
    m, dm = r.shape
    hd = RW_HEAD
    pw = 2 * hd
    npairs = dm // pw
    npair = 2
    nch = t // cl
    s0p = s0.reshape(bsz, npairs, pw, hd)
    kern = functools.partial(_rwkv_chunk_kernel, npair=npair, cl=cl, nch=nch)
    tspec = pl.BlockSpec((t, npair * pw), lambda bi, pi: (bi, pi))
    vspec = pl.BlockSpec((1, npair * pw), lambda bi, pi: (0, pi))
    sspec = pl.BlockSpec((1, npair, pw, hd), lambda bi, pi: (bi, pi, 0, 0))
    o, s = pl.pallas_call(
        kern,
        grid=(bsz, npairs // npair),
        in_specs=[tspec] * 7 + [vspec] * 3 + [sspec],
        out_specs=[tspec, sspec],
        out_shape=[jax.ShapeDtypeStruct((m, dm), BF16), jax.ShapeDtypeStruct((bsz, npairs, pw, hd), F32)],
        scratch_shapes=[pltpu.VMEM((npair, pw, pw), F32),
                        pltpu.VMEM((nch, npair, cl, pw), BF16), pltpu.VMEM((nch, npair, cl, pw), F32),
                        pltpu.VMEM((nch, npair, pw, pw), BF16), pltpu.VMEM((nch, npair, pw, pw), F32),
                        pltpu.VMEM((nch, npair, 1, pw), F32)],
        compiler_params=_cparams(("parallel", "parallel")),
        name="rwkv_chunk",
    )(r, lw, k, v, a, b, gate, _row(ln_w), _row(ln_b), _row(r_k), s0p)
    return o, s.reshape(bsz, dm // hd, hd, hd)


def _rwkv_step_kernel(r_ref, lw_ref, k_ref, v_ref, a_ref, b_ref, gate_ref, lnw_ref, lnb_ref, rk_ref, s0_ref,
                      o_ref, sout_ref, oacc, *, hd):
    w = jnp.exp(lw_ref[...])
    a, b, k, r = a_ref[...], b_ref[...], k_ref[...], r_ref[...]

    def value_row(i, carry):
        st = s0_ref[0, i]
        sa = jnp.sum(st * a, axis=0, keepdims=True)
        sn = st * w + sa * b + v_ref[pl.ds(i, 1), :] * k
        sout_ref[0, i] = sn
        oacc[pl.ds(i, 1), :] = jnp.sum(sn * r, axis=0, keepdims=True)
        return carry

    lax.fori_loop(0, hd, value_row, 0, unroll=8)
    o = oacc[...]
    mu = jnp.mean(o, axis=0, keepdims=True)
    dlt = o - mu
    var = jnp.mean(dlt * dlt, axis=0, keepdims=True)
    on = dlt * lax.rsqrt(var + RW_LN_EPS) * lnw_ref[...] + lnb_ref[...]
    bonus = jnp.sum(r * k * rk_ref[...], axis=0, keepdims=True) * v_ref[...]
    o_ref[...] = (on + bonus) * gate_ref[...]


def _rwkv_step_call(r, lw, k, v, a, b, gate, ln_w, ln_b, r_k, s0):
    bsz, dm = r.shape
    hd = RW_HEAD
    nh = dm // hd
    col = lambda x: jnp.broadcast_to(x.reshape(dm, 1), (dm, bsz))
    vspec = pl.BlockSpec((hd, bsz), lambda hi: (hi, 0))
    sspec = pl.BlockSpec((1, hd, hd, bsz), lambda hi: (hi, 0, 0, 0))
    o, s = pl.pallas_call(
        functools.partial(_rwkv_step_kernel, hd=hd),
        grid=(nh,),
        in_specs=[vspec] * 10 + [sspec],
        out_specs=[vspec, sspec],
        out_shape=[jax.ShapeDtypeStruct((dm, bsz), F32), jax.ShapeDtypeStruct((nh, hd, hd, bsz), F32)],
        scratch_shapes=[pltpu.VMEM((hd, bsz), F32)],
        compiler_params=_cparams(("parallel",)),
        name="rwkv_step",
    )(*[x.T for x in (r, lw, k, v, a, b, gate)], col(ln_w), col(ln_b), col(r_k), jnp.transpose(s0, (1, 2, 3, 0)))
    return o.T, jnp.transpose(s, (3, 0, 1, 2))


def _outproj_body(row, full, col, rc, out, scr):
    out[0][...] = rc[0][...] + jnp.dot(row[0][...].astype(BF16), col[0][...], preferred_element_type=F32)


def _outproj_call(a, w, h, t, name):
    m, kdim = a.shape
    n = w.shape[1]
    (o,) = _proj_call(_outproj_body, m=m, n=n, tm=_tiles(m, t), tn=512, row_ins=[a], full_ins=[], col_ins=[w],
                      rc_ins=[h], out_dtypes=[F32], scratch=[], name=name)
    return o


def _rwkv_layer(h, bsz, t, shift, wkv, norm_w, wts):
    m, dm = h.shape
    tm = _tiles(m, t)
    step = t == 1
    if step:
        prev = (shift.astype(F32), pl.BlockSpec((tm, dm), lambda i, j: (i, 0)))
        seq_tiles = 1
    else:
        seq_tiles = t // tm
        prev = (h, pl.BlockSpec((SUBLANES, dm), lambda i, j: (jnp.maximum(i * (tm // SUBLANES) - 1, 0), 0)))
    tn = 256
    bd = (jnp.arange(tn)[:, None] // RW_HEAD == jnp.arange(tn)[None, :] // RW_HEAD).astype(BF16)
    body = functools.partial(_rwkv_proj_body, tm=tm, seq_tiles=seq_tiles, step=step)
    lora = wts['w1'].shape[1]
    glora = wts['g1'].shape[1]
    r, lw, k, v, a, b, g = _proj_call(
        body, m=m, n=dm, tm=tm, tn=tn, row_ins=[h, prev],
        full_ins=[_row(norm_w), wts['mix'], wts['w1'], wts['a1'], wts['g1'], bd],
        col_ins=[wts['w_r'], wts['w_k'], wts['w_v'], wts['w2'], wts['a2'], wts['g2'],
                 _row(wts['w0']), _row(wts['a0']), _row(wts['k_k']), _row(wts['k_a'])],
        rc_ins=[], out_dtypes=[F32] * 7,
        scratch=[pltpu.VMEM((tm, dm), BF16)] * 3 + [pltpu.VMEM((tm, lora), BF16)] * 2
        + [pltpu.VMEM((tm, glora), BF16)],
        name="rwkv_proj")
    if step:
        shift_new = _rmsnorm_call(h, norm_w)
        og, s_new = _rwkv_step_call(r, lw, k, v, a, b, g, wts['ln_w'], wts['ln_b'], wts['r_k'], wkv)
    else:
        shift_new = _rmsnorm_call(h.reshape(bsz, t, dm)[:, -1], norm_w)
        og, s_new = _rwkv_chunk_call(r, lw, k, v, a, b, g, wts['ln_w'], wts['ln_b'], wts['r_k'], wkv, bsz, t,
                                     RW_CHUNK)
    return _outproj_call(og, wts['w_o'], h, t, "rwkv_out"), shift_new, s_new


def _la_chunk_kernel(q_ref, k_ref, v_ref, g_ref, gate_ref, nw_ref, s0_ref, o_ref, sout_ref, s_scr,
                     *, hb, dk, dv, cl, nch):
    s_scr[...] = s0_ref[0]
    rowi = _iota((cl, cl), 0)
    coli = _iota((cl, cl), 1)
    tri = (rowi >= coli).astype(BF16)
    levels = [1 << i for i in range(int(math.log2(cl)))]
    masks = {s: ((rowi // s) % 2 == 1) & ((coli // s) == (rowi // s) - 1) for s in levels}
    rid = _iota((cl, 1), 0)
    nw = nw_ref[...]
    ug = max(1, 8 // hb)
    if nch % ug:
        ug = 1

    def rows(c):
        return pl.ds(pl.multiple_of(c * cl, cl), cl)

    def group(it, carry):
        ch = [(it * ug + j, h) for j in range(ug) for h in range(hb)]
        ldk = lambda ref: [ref[rows(c), h * dk:(h + 1) * dk] for c, h in ch]
        ldv = lambda ref: [ref[rows(c), h * dv:(h + 1) * dv] for c, h in ch]
        q, k, v = ldk(q_ref), ldk(k_ref), ldv(v_ref)
        cum = [_xdot_l(tri, g) for g in ldk(g_ref)]
        att = [jnp.zeros((cl, cl), F32) for _ in ch]
        ends = cum
        for s in levels:
            prev = [jnp.where(rid >= s, pltpu.roll(e, s, 0), 0.0) for e in ends]
            a_s = [_bdot_nt(qq * jnp.exp(c - p), kk * jnp.exp(e - c))
                   for qq, kk, c, p, e in zip(q, k, cum, prev, ends)]
            att = [a + jnp.where(masks[s], x, 0.0) for a, x in zip(att, a_s)]
            if 2 * s < cl:
                ends = [jnp.where((rid % (2 * s)) >= s, e, pltpu.roll(e, cl - s, 0)) for e in ends]
        o_in = [_bdot(a, vv) + jnp.sum(qq * kk, axis=-1, keepdims=True) * vv
                for a, vv, qq, kk in zip(att, v, q, k)]
        last = [c[cl - 1:cl, :] for c in cum]
        upd = [_bdot_tn(kk * jnp.exp(l - c), vv) for kk, l, c, vv in zip(k, last, cum, v)]
        dec = [jnp.exp(jnp.transpose(jnp.broadcast_to(l, (LANES, dk)))) for l in last]
        qe = [qq * jnp.exp(c) for qq, c in zip(q, cum)]
        st = [s_scr[h] for h in range(hb)]
        for i, (c, h) in enumerate(ch):
            o = o_in[i] + _bdot(qe[i], st[h])
            st[h] = jnp.concatenate([st[h][:, j * LANES:(j + 1) * LANES] * dec[i] for j in range(dv // LANES)],
                                    axis=1) + upd[i]
            on = o * lax.rsqrt(jnp.mean(o * o, axis=-1, keepdims=True) + NORM_EPS) * nw
            vs = slice(h * dv, (h + 1) * dv)
            o_ref[rows(c), vs] = (on * gate_ref[rows(c), vs]).astype(o_ref.dtype)
        for h in range(hb):
            s_scr[h] = st[h]
        return carry

    lax.fori_loop(0, nch // ug, group, 0)
    sout_ref[0] = s_scr[...]


def _la_chunk_call(q, k, v, g, gate, norm_w, s0, bsz, t, cl, hb, name):
    m = q.shape[0]
    _, nh, dk, dv = s0.shape
    kern = functools.partial(_la_chunk_kernel, hb=hb, dk=dk, dv=dv, cl=cl, nch=t // cl)
    kspec = pl.BlockSpec((t, hb * dk), lambda bi, hi: (bi, hi))
    vspec = pl.BlockSpec((t, hb * dv), lambda bi, hi: (bi, hi))
    sspec = pl.BlockSpec((1, hb, dk, dv), lambda bi, hi: (bi, hi, 0, 0))
    return pl.pallas_call(
        kern,
        grid=(bsz, nh // hb),
        in_specs=[kspec, kspec, vspec, kspec, vspec, pl.BlockSpec((1, dv), lambda bi, hi: (0, 0)), sspec],
        out_specs=[vspec, sspec],
        out_shape=[jax.ShapeDtypeStruct((m, nh * dv), BF16), jax.ShapeDtypeStruct(s0.shape, F32)],
        scratch_shapes=[pltpu.VMEM((hb, dk, dv), F32)],
        compiler_params=_cparams(("parallel", "parallel")),
        name=name,
    )(q, k, v, g, gate, _row(norm_w), s0)


def _la_step_kernel(q_ref, k_ref, v_ref, g_ref, gate_ref, nw_ref, s0_ref, o_ref, sout_ref, *, nh, dk, dv):
    first = _iota((STEP_PAD, 1), 0) == 0

    def pad(x):
        return jnp.where(first, x, 0.0)

    hg = min(nh, 8)

    def group(gi, carry):
        hs = [gi * hg + j for j in range(hg)]
        rows = lambda ref: [ref[0, pl.ds(h, 1), :] for h in hs]
        dec = [jnp.exp(jnp.transpose(jnp.broadcast_to(g, (LANES, dk)))) for g in rows(g_ref)]
        upd = [_bdot_tn(pad(k), pad(v)) for k, v in zip(rows(k_ref), rows(v_ref))]
        sn = [jnp.concatenate([s0_ref[0, h][:, j * LANES:(j + 1) * LANES] * d for j in range(dv // LANES)], axis=1)
              + u for h, d, u in zip(hs, dec, upd)]
        for h, s in zip(hs, sn):
            sout_ref[0, h] = s
        os_ = [_bdot(pad(q), s)[0:1] for q, s in zip(rows(q_ref), sn)]
        for h, o, gt in zip(hs, os_, rows(gate_ref)):
            on = o * lax.rsqrt(jnp.mean(o * o, axis=-1, keepdims=True) + NORM_EPS) * nw_ref[...]
            o_ref[0, pl.ds(h, 1), :] = (on * gt).astype(o_ref.dtype)
        return carry

    lax.fori_loop(0, nh // hg, group, 0)


def _la_step_call(q, k, v, g, gate, norm_w, s0, name):
    bsz, nh, dk, dv = s0.shape
    kh = lambda x: x.reshape(bsz, nh, dk)
    vh = lambda x: x.reshape(bsz, nh, dv)
    kspec = pl.BlockSpec((1, nh, dk), lambda bi: (bi, 0, 0))
    vspec = pl.BlockSpec((1, nh, dv), lambda bi: (bi, 0, 0))
    sspec = pl.BlockSpec((1, nh, dk, dv), lambda bi: (bi, 0, 0, 0))
    o, s = pl.pallas_call(
        functools.partial(_la_step_kernel, nh=nh, dk=dk, dv=dv),
        grid=(bsz,),
        in_specs=[kspec, kspec, vspec, kspec, vspec, pl.BlockSpec((1, dv), lambda bi: (0, 0)), sspec],
        out_specs=[vspec, sspec],
        out_shape=[jax.ShapeDtypeStruct((bsz, nh, dv), F32), jax.ShapeDtypeStruct(s0.shape, F32)],
        compiler_params=_cparams(("parallel",)),
        name=name,
    )(kh(q), kh(k), vh(v), kh(g), vh(gate), _row(norm_w), s0)
    return o.reshape(bsz, nh * dv), s


def _la_run(q, k, v, g, gate, norm_w, s0, bsz, t, hb, name):
    if t == 1:
        return _la_step_call(q, k, v, g, gate, norm_w, s0, name + "_step")
    return _la_chunk_call(q, k, v, g, gate, norm_w, s0, bsz, t, LA_CHUNK, hb, name)


def _norm_prologue(h_ref, nw_ref, xn_s):
    @pl.when(pl.program_id(1) == 0)
    def _():
        xn_s[...] = _rms(h_ref[...], nw_ref[...]).astype(BF16)


def _gla_qkg_body(row, full, col, rc, out, scr, *, scale):
    nw_ref, gk1_ref = full
    wq, wk, gk2, bgk = col
    xn_s, t1_s = scr

    @pl.when(pl.program_id(1) == 0)
    def _():
        xn = _rms(row[0][...], nw_ref[...]).astype(BF16)
        xn_s[...] = xn
        t1_s[...] = jnp.dot(xn, gk1_ref[...], preferred_element_type=F32).astype(BF16)

    xn = xn_s[...]
    out[0][...] = jnp.dot(xn, wq[...], preferred_element_type=F32) * scale
    out[1][...] = jnp.dot(xn, wk[...], preferred_element_type=F32)
    lg = jnp.dot(t1_s[...], gk2[...], preferred_element_type=F32) + bgk[...]
    out[2][...] = jax.nn.log_sigmoid(lg) / GLA_GATE_TEMP


def _vg_body(row, full, col, rc, out, scr):
    _norm_prologue(row[0], full[0], scr[0])
    xn = scr[0][...]
    tn = col[0].shape[1]
    for cs in (slice(0, tn // 2), slice(tn // 2, tn)):
        out[0][:, cs] = jnp.dot(xn, col[0][:, cs], preferred_element_type=F32)
        out[1][:, cs] = jax.nn.silu(jnp.dot(xn, col[1][:, cs], preferred_element_type=F32))


def _gla_layer(h, bsz, t, s0, norm_w, wts):
    m, dm = h.shape
    tm = _tiles(m, t)
    nh = GLA_HEADS
    dk = wts['w_q'].shape[1] // nh
    rank = wts['gk1'].shape[1]
    q, k, lg = _proj_call(functools.partial(_gla_qkg_body, scale=dk ** -0.5), m=m, n=nh * dk, tm=tm, tn=512,
                          row_ins=[h], full_ins=[_row(norm_w), wts['gk1']],
                          col_ins=[wts['w_q'], wts['w_k'], wts['gk2'], _row(wts['b_gk'])], rc_ins=[],
                          out_dtypes=[F32] * 3,
                          scratch=[pltpu.VMEM((tm, dm), BF16), pltpu.VMEM((tm, rank), BF16)], name="gla_qkg")
    v, gate = _proj_call(_vg_body, m=m, n=wts['w_v'].shape[1], tm=tm, tn=512, row_ins=[h],
                         full_ins=[_row(norm_w)], col_ins=[wts['w_v'], wts['w_g']], rc_ins=[],
                         out_dtypes=[F32] * 2, scratch=[pltpu.VMEM((tm, dm), BF16)], name="gla_vg")
    og, s_new = _la_run(q, k, v, lg, gate, wts['norm'], s0, bsz, t, 1, "gla_chunk")
    return _outproj_call(og, wts['w_o'], h, t, "gla_out"), s_new


def _hgrn_proj_body(row, full, col, rc, out, scr, *, scale, layer_idx):
    wq, wf, wi, wg, lbp = col
    _norm_prologue(row[0], full[0], scr[0])
    xn = scr[0][...]
    tn = wq.shape[1]
    for cs in (slice(0, tn // 2), slice(tn // 2, tn)):
        dot = lambda w: jnp.dot(xn, w[:, cs], preferred_element_type=F32)
        z, qv, iv, gv = dot(wf), dot(wq), dot(wi), dot(wg)
        lbx = lbp[:, cs]
        ex = jnp.exp(lbx - jnp.max(lbx, axis=0, keepdims=True))
        lb = jnp.sum(ex[1:layer_idx + 1], axis=0, keepdims=True) / jnp.sum(ex, axis=0, keepdims=True)
        e = jnp.exp(-jnp.abs(z))
        rcp = 1.0 / (1.0 + e)
        pos = z >= 0
        sig = jnp.where(pos, rcp, e * rcp)
        sig_neg = jnp.where(pos, e * rcp, rcp)
        out[0][:, cs] = qv * scale
        out[1][:, cs] = (1.0 - lb) * sig_neg
        out[2][:, cs] = jnp.log(lb + (1.0 - lb) * sig)
        out[3][:, cs] = iv
        out[4][:, cs] = jax.nn.silu(gv)


def _hgrn_layer(h, bsz, t, s0, norm_w, wts, layer_idx):
    m, dm = h.shape
    tm = _tiles(m, t)
    body = functools.partial(_hgrn_proj_body, scale=HG_EXPAND ** -0.5, layer_idx=layer_idx)
    q, k, lf, v, gate = _proj_call(body, m=m, n=dm, tm=tm, tn=512, row_ins=[h], full_ins=[_row(norm_w)],
                                   col_ins=[wts['w_q'], wts['w_f'], wts['w_i'], wts['w_g'], wts['lb']],
                                   rc_ins=[], out_dtypes=[F32] * 5, scratch=[pltpu.VMEM((tm, dm), BF16)],
                                   name="hgrn_proj")
    og, s_new = _la_run(q, k, v, lf, gate, wts['norm'], s0, bsz, t, 2, "hgrn_chunk")
    return _outproj_call(og, wts['w_o'], h, t, "hgrn_out"), s_new


def _ffn_kernel(*refs, step, tm, tf, nf, seq_tiles, final_norm):
    h_ref, nw_ref, wup_ref, wgate_ref, wc_ref, bc_ref, wdown_ref = refs[:7]
    p = 7
    if step:
        p0_ref, p1_ref = refs[p:p + 2]; p += 2
    if final_norm:
        fnw_ref = refs[p]; p += 1
    out_ref, tail_ref = refs[p:p + 2]
    xn_s, carry_s = refs[p + 2:]
    i = pl.program_id(0)
    f = pl.program_id(1)

    @pl.when(f == 0)
    def _():
        hv = h_ref[...]
        xn_s[...] = _rms(hv, nw_ref[...]).astype(BF16)
        out_ref[...] = hv

    xn = xn_s[...]
    if not step:
        @pl.when(i % seq_tiles == 0)
        def _():
            carry_s[f] = jnp.zeros((SUBLANES, tf), F32)

    hw = tf // 2
    halves = [slice(0, hw), slice(hw, tf)]
    us = [jnp.dot(xn, wup_ref[:, cs], preferred_element_type=F32) for cs in halves]
    gates = [jnp.dot(xn, wgate_ref[:, cs], preferred_element_type=F32) for cs in halves]
    rid = _iota((tm, 1), 0)
    down = None
    for cs, u, gate in zip(halves, us, gates):
        wc = wc_ref[:, cs]
        if step:
            u2, u1 = p0_ref[:, cs], p1_ref[:, cs]
            tail_ref[:, cs] = u
        else:
            prev = carry_s[f][:, cs]
            u1 = jnp.where(rid == 0, prev[SUBLANES - 1:SUBLANES], pltpu.roll(u, 1, 0))
            u2 = jnp.where(rid == 0, prev[SUBLANES - 2:SUBLANES - 1],
                           jnp.where(rid == 1, prev[SUBLANES - 1:SUBLANES], pltpu.roll(u, 2, 0)))
            tail_ref[0, :, cs] = u[tm - SUBLANES:tm]
        c = u2 * wc[0:1] + u1 * wc[1:2] + u * wc[2:3] + bc_ref[:, cs]
        act = (jax.nn.gelu(c) * gate).astype(BF16)
        d = jnp.dot(act, wdown_ref[cs, :], preferred_element_type=F32)
        down = d if down is None else down + d
    if not step:
        carry_s[f] = jnp.concatenate([u[tm - SUBLANES:tm] for u in us], axis=1)
    out_ref[...] += down
    if final_norm:
        @pl.when(f == nf - 1)
        def _():
            out_ref[...] = _rms(out_ref[...], fnw_ref[...])


def _ffn_call(h, bsz, t, conv_prev, norm_w, layer, w_up, w_gate, w_conv, b_conv, w_down, final_norm_w):
    m, dm = h.shape
    dff = w_up.shape[-1]
    step = t == 1
    tm = _tiles(m, t)
    tf = 512
    nf = dff // tf
    seq_tiles = 1 if step else t // tm
    final_norm = final_norm_w is not None
    args = [h, _row(norm_w), w_up, w_gate, w_conv.astype(F32), _row(b_conv.astype(F32)), w_down]
    specs = [pl.BlockSpec((tm, dm), lambda i, f: (i, 0)), pl.BlockSpec((1, dm), lambda i, f: (0, 0)),
             pl.BlockSpec((None, dm, tf), lambda i, f: (layer, 0, f)),
             pl.BlockSpec((None, dm, tf), lambda i, f: (layer, 0, f)),
             pl.BlockSpec((CONV_W, tf), lambda i, f: (0, f)), pl.BlockSpec((1, tf), lambda i, f: (0, f)),
             pl.BlockSpec((None, tf, dm), lambda i, f: (layer, f, 0))]
    if step:
        args += [conv_prev[:, 0].astype(F32), conv_prev[:, 1].astype(F32)]
        specs += [pl.BlockSpec((tm, tf), lambda i, f: (i, f))] * 2
        tail_shape = jax.ShapeDtypeStruct((m, dff), F32)
        tail_spec = pl.BlockSpec((tm, tf), lambda i, f: (i, f))
    else:
        tail_shape = jax.ShapeDtypeStruct((m // tm, SUBLANES, dff), F32)
        tail_spec = pl.BlockSpec((1, SUBLANES, tf), lambda i, f: (i, 0, f))
    if final_norm:
        args.append(_row(final_norm_w))
        specs.append(pl.BlockSpec((1, dm), lambda i, f: (0, 0)))
    kern = functools.partial(_ffn_kernel, step=step, tm=tm, tf=tf, nf=nf, seq_tiles=seq_tiles,
                             final_norm=final_norm)
    out, tail = pl.pallas_call(
        kern,
        grid=(m // tm, nf),
        in_specs=specs,
        out_specs=[pl.BlockSpec((tm, dm), lambda i, f: (i, 0)), tail_spec],
        out_shape=[jax.ShapeDtypeStruct((m, dm), F32), tail_shape],
        scratch_shapes=[pltpu.VMEM((tm, dm), BF16), pltpu.VMEM((nf, SUBLANES, tf), F32)],
        compiler_params=_cparams(("arbitrary", "arbitrary")),
        name="conv_ffn",
    )(*args)
    if step:
        conv_new = jnp.stack([conv_prev[:, 1].astype(F32), tail], axis=1)
    else:
        conv_new = tail.reshape(bsz, seq_tiles, SUBLANES, dff)[:, -1, SUBLANES - (CONV_W - 1):]
    return out, conv_new


def _trunk(x, s5_re, s5_im, rw_shift, rw_wkv, gla_s, hg_s, ffn_conv, p):
    bsz, t, dm = x.shape
    h = x.reshape(bsz * t, dm).astype(F32)
    depth = p['norm_mix'].shape[0]
    conv_new = []
    for i in range(depth):
        kind = i % 4
        nw = p['norm_mix'][i]
        if kind == 0:
            h, s5_re, s5_im = _s5_layer(h, bsz, t, s5_re, s5_im, nw, p['s5'], p['s5_c_re'], p['s5_c_im'],
                                        p['s5_d'], p['s5_w_glu'])
        elif kind == 1:
            h, rw_shift, rw_wkv = _rwkv_layer(h, bsz, t, rw_shift, rw_wkv, nw, p['rw'])
        elif kind == 2:
            h, gla_s = _gla_layer(h, bsz, t, gla_s, nw, p['gla'])
        else:
            h, hg_s = _hgrn_layer(h, bsz, t, hg_s, nw, p['hg'], i)
        fin = p['norm_final'] if i == depth - 1 else None
        h, c = _ffn_call(h, bsz, t, ffn_conv[i], p['norm_ffn'][i], i, p['ffn_w_up'], p['ffn_w_gate'],
                         p['ffn_w_conv'][i], p['ffn_b_conv'][i], p['ffn_w_down'], fin)
        conv_new.append(c)
    return h.reshape(bsz, t, dm), s5_re, s5_im, rw_shift, rw_wkv, gla_s, hg_s, jnp.stack(conv_new)


def _pad_cols(w, n):
    return jnp.pad(w, ((0, 0), (0, n - w.shape[1])))


def _pad_rows(w, n):
    return jnp.pad(w, ((0, n - w.shape[0]), (0, 0)))


def kernel(x_prompt, x_sample, state_s5_re, state_s5_im, state_rwkv_shift, state_rwkv_wkv, state_gla, state_hgrn, state_ffn_conv, norm_mix, norm_ffn, norm_final, s5_lambda_re, s5_lambda_im, s5_log_dt, s5_b_re, s5_b_im, s5_c_re, s5_c_im, s5_d, s5_w_glu, rw_mix, rw_w_r, rw_w_k, rw_w_v, rw_w_o, rw_w0, rw_w1, rw_w2, rw_a0, rw_a1, rw_a2, rw_g1, rw_g2, rw_k_k, rw_k_a, rw_r_k, rw_ln_w, rw_ln_b, gla_w_q, gla_w_k, gla_w_v, gla_w_gk1, gla_w_gk2, gla_b_gk, gla_w_g, gla_norm, gla_w_o, hg_w_q, hg_w_f, hg_w_i, hg_w_g, hg_lb, hg_norm, hg_w_o, ffn_w_up, ffn_w_gate, ffn_w_conv, ffn_b_conv, ffn_w_down):
    bf = lambda w: w.astype(BF16)
    f32 = lambda w: w.astype(F32)
    lora = LANES * pl.cdiv(rw_w1.shape[1], LANES)
    alora = LANES * pl.cdiv(rw_a1.shape[1], LANES)
    grank = LANES * pl.cdiv(gla_w_gk1.shape[1], LANES)
    p = {
        'norm_mix': f32(norm_mix), 'norm_ffn': f32(norm_ffn), 'norm_final': f32(norm_final),
        's5': _s5_params(s5_lambda_re, s5_lambda_im, s5_log_dt, s5_b_re, s5_b_im, s5_c_re, s5_c_im, s5_d),
        's5_c_re': s5_c_re, 's5_c_im': s5_c_im, 's5_d': s5_d, 's5_w_glu': bf(s5_w_glu),
        'rw': dict(mix=_pad_rows(f32(rw_mix), SUBLANES), w_r=bf(rw_w_r), w_k=bf(rw_w_k), w_v=bf(rw_w_v),
                   w_o=bf(rw_w_o), w0=f32(rw_w0), w1=bf(_pad_cols(rw_w1, lora)), w2=bf(_pad_rows(rw_w2, lora)),
                   a0=f32(rw_a0), a1=bf(_pad_cols(rw_a1, alora)), a2=bf(_pad_rows(rw_a2, alora)),
                   g1=bf(rw_g1), g2=bf(rw_g2), k_k=f32(rw_k_k), k_a=f32(rw_k_a), r_k=f32(rw_r_k).reshape(-1),
                   ln_w=f32(rw_ln_w), ln_b=f32(rw_ln_b)),
        'gla': dict(w_q=bf(gla_w_q), w_k=bf(gla_w_k), w_v=bf(gla_w_v), gk1=bf(_pad_cols(gla_w_gk1, grank)),
                    gk2=bf(_pad_rows(gla_w_gk2, grank)), b_gk=f32(gla_b_gk), w_g=bf(gla_w_g), norm=f32(gla_norm),
                    w_o=bf(gla_w_o)),
        'hg': dict(w_q=bf(hg_w_q), w_f=bf(hg_w_f), w_i=bf(hg_w_i), w_g=bf(hg_w_g), lb=f32(hg_lb),
                   norm=f32(hg_norm), w_o=bf(hg_w_o)),
        'ffn_w_up': bf(ffn_w_up), 'ffn_w_gate': bf(ffn_w_gate), 'ffn_w_conv': ffn_w_conv,
        'ffn_b_conv': ffn_b_conv, 'ffn_w_down': bf(ffn_w_down),
    }
    nb, _, dm = x_prompt.shape
    depth = norm_mix.shape[0]
    dff = ffn_w_up.shape[-1]
    z_s5 = jnp.zeros((nb,) + state_s5_re.shape[1:], F32)
    outs_p = _trunk(x_prompt, z_s5, z_s5, jnp.zeros((nb, dm), F32),
                    jnp.zeros((nb,) + state_rwkv_wkv.shape[1:], F32), jnp.zeros((nb,) + state_gla.shape[1:], F32),
                    jnp.zeros((nb,) + state_hgrn.shape[1:], F32), jnp.zeros((depth, nb, CONV_W - 1, dff), F32), p)
    outs_s = _trunk(x_sample, f32(state_s5_re), f32(state_s5_im), state_rwkv_shift, f32(state_rwkv_wkv),
                    f32(state_gla), f32(state_hgrn), state_ffn_conv, p)
    return (outs_p[0], outs_s[0]) + tuple(outs_p[1:]) + tuple(outs_s[1:])
```
